```python
import math
import jax, jax.numpy as jnp
from jax import lax
import numpy as np

D_MODEL = 1024
BATCH = 32
SEQ = 256
DEPTH = 2
DEC_BATCH = 8
DEC_SEQ = 1024
PAST_LEN = 256

F32 = jnp.float32
GRID_W = 64
POS_BASE = 10000.0
EPS = 1e-6
N_DIR = 2
D_MIX = D_MODEL
N_MIXERS = 4
GROUP_W = D_MIX // N_MIXERS
D_FF = -(-8 * D_MODEL // (3 * 256)) * 256

HY_W = GROUP_W
HY_IN = 3 * HY_W
HY_SHORT = 3
HY_BANDS = 16
HY_EMB = 1 + 2 * HY_BANDS
HY_HIDDEN = 64
HY_DECAY_TARGET = 1e-2
HY_FAST = 0.3
HY_SLOW = 1.5

SSD_HEADDIM = 64
SSD_INNER = GROUP_W
SSD_HEADS = SSD_INNER // SSD_HEADDIM
SSD_GROUPS = 2
SSD_STATE = 64
SSD_CONV = 4
SSD_CHUNK = 64
SSD_XBC = SSD_INNER + 2 * SSD_GROUPS * SSD_STATE
SSD_IN = SSD_INNER + SSD_XBC + N_DIR * SSD_HEADS

LRU_W = GROUP_W
LRU_HEADS = 4
LRU_HD = LRU_W // LRU_HEADS
LRU_CONV = 4
LRU_C = 8.0
LRU_IN = 2 * LRU_W

GDN_HEAD_DIM = 64
GDN_HEADS = GROUP_W // GDN_HEAD_DIM
GDN_CONV = 4
GDN_CHUNK = 64
GDN_IN = 4 * GROUP_W + 2 * N_DIR * GDN_HEADS

IN_W = HY_IN + SSD_IN + LRU_IN + GDN_IN
IN_SPLITS = (HY_IN, HY_IN + SSD_IN, HY_IN + SSD_IN + LRU_IN)

kernel_name = 'hybrid_bidir_diffusion_prefix_step'


def rmsnorm(x, g):
    xf = x.astype(F32)
    y = xf * lax.rsqrt(jnp.mean(xf * xf, axis=-1, keepdims=True) + EPS)
    return (y * g).astype(x.dtype)


def l2norm(x):
    return x * lax.rsqrt(jnp.sum(x * x, axis=-1, keepdims=True) + EPS)


def dwconv(x, w):
    K = w.shape[0]
    L = x.shape[1]
    left = K // 2
    xp = jnp.pad(x, ((0, 0), (left, K - 1 - left), (0, 0)))
    return sum(xp[:, j:j + L] * w[j] for j in range(K))


def grid_pos_embed(n_tokens):
    rows = n_tokens // GRID_W
    rr, cc = jnp.meshgrid(jnp.arange(rows, dtype=F32), jnp.arange(GRID_W, dtype=F32), indexing='ij')
    quarter = D_MODEL // 4
    omega = 1.0 / (POS_BASE ** (jnp.arange(quarter, dtype=F32) / quarter))
    def enc(pos):
        ang = pos.reshape(-1)[:, None] * omega[None, :]
        return jnp.concatenate([jnp.sin(ang), jnp.cos(ang)], axis=-1)
    return jnp.concatenate([enc(rr), enc(cc)], axis=-1)


def hyena_filters(L, p):
    t = jnp.linspace(0.0, 1.0, L, dtype=F32)[:, None]
    w = (2.0 * math.pi / L) * jnp.arange(L, dtype=F32)[:, None]
    bands = jnp.linspace(1e-4, HY_BANDS - 1, HY_BANDS, dtype=F32)[None, :]
    feats = jnp.concatenate([t, jnp.cos(bands * w), -jnp.sin(bands * w)], axis=-1)
    h = jnp.sin(p['hy_freq'][0] * (feats @ p['hy_w1'] + p['hy_b1']))
    h = jnp.sin(p['hy_freq'][1] * (h @ p['hy_w2'] + p['hy_b2']))
    h = (h @ p['hy_w3']).reshape(L, N_DIR, HY_W).astype(F32)
    max_decay = math.log(HY_DECAY_TARGET) / HY_FAST
    min_decay = math.log(HY_DECAY_TARGET) / HY_SLOW
    deltas = jnp.abs(jnp.linspace(min_decay, max_decay, HY_W, dtype=F32))
    h = h * jnp.exp(-t * deltas)[:, None, :]
    return h / jnp.sum(jnp.abs(h), axis=(0, 1), keepdims=True)


def hyena_mixer(u, p):
    L = u.shape[1]
    x0, x1, v = jnp.split(dwconv(u, p['hy_conv']), 3, axis=-1)
    filt = hyena_filters(L, p)
    z = v * x1
    n = 2 * L
    def conv_fft(s, h):
        spec = jnp.fft.rfft(s, n=n, axis=1) * jnp.fft.rfft(h, n=n, axis=0)[None]
        return jnp.fft.irfft(spec, n=n, axis=1)[:, :L]
    y = conv_fft(z, filt[:, 0]) + conv_fft(z[:, ::-1], filt[:, 1])[:, ::-1] + z * p['hy_bias']
    return x0 * y


def segsum(a):
    T = a.shape[-1]
    cs = jnp.cumsum(a, axis=-1)
    diff = cs[..., :, None] - cs[..., None, :]
    return jnp.where(jnp.tril(jnp.ones((T, T), dtype=bool)), diff, -jnp.inf)


def ssd_scan(x, a, b, c, s0):
    bsz, L, H, P = x.shape
    N = b.shape[-1]
    nc = L // SSD_CHUNK
    x = x.reshape(bsz, nc, SSD_CHUNK, H, P)
    b = b.reshape(bsz, nc, SSD_CHUNK, H, N)
    c = c.reshape(bsz, nc, SSD_CHUNK, H, N)
    a = a.reshape(bsz, nc, SSD_CHUNK, H).transpose(0, 3, 1, 2)
    acs = jnp.cumsum(a, axis=-1)
    y_diag = jnp.einsum('bclhn,bcshn,bhcls,bcshp->bclhp', c, b, jnp.exp(segsum(a)), x)
    decay_in = jnp.exp(acs[..., -1:] - acs)
    chunk_states = jnp.einsum('bclhn,bhcl,bclhp->bchpn', b, decay_in, x)
    chunk_states = jnp.concatenate([s0[:, None], chunk_states], axis=1)
    chunk_decay = jnp.exp(segsum(jnp.pad(acs[..., -1], ((0, 0), (0, 0), (1, 0)))))
    states = jnp.einsum('bhzc,bchpn->bzhpn', chunk_decay, chunk_states)
    y_off = jnp.einsum('bclhn,bchpn,bhcl->bclhp', c, states[:, :-1], jnp.exp(acs))
    return (y_diag + y_off).reshape(bsz, L, H, P), states[:, -1]


def ssd_mixer(u, s0, p):
    bsz, L, _ = u.shape
    s0 = s0.astype(F32)
    z, xbc, dt_raw = jnp.split(u, [SSD_INNER, SSD_INNER + SSD_XBC], axis=-1)
    xbc = jax.nn.silu(dwconv(xbc, p['ssd_conv']))
    xs, bm, cm = jnp.split(xbc, [SSD_INNER, SSD_INNER + SSD_GROUPS * SSD_STATE], axis=-1)
    xs = xs.reshape(bsz, L, SSD_HEADS, SSD_HEADDIM)
    rep = SSD_HEADS // SSD_GROUPS
    bm = jnp.repeat(bm.reshape(bsz, L, SSD_GROUPS, SSD_STATE), rep, axis=2)
    cm = jnp.repeat(cm.reshape(bsz, L, SSD_GROUPS, SSD_STATE), rep, axis=2)
    dt = jax.nn.softplus(dt_raw.reshape(bsz, L, N_DIR, SSD_HEADS) + p['ssd_dt_bias'])
    a = -jnp.exp(p['ssd_a_log']) * dt
    xdt = xs[:, :, None] * dt[..., None]
    y_f, s_f = ssd_scan(xdt[:, :, 0], a[:, :, 0], bm, cm, s0[:, 0])
    y_b, s_b = ssd_scan(xdt[:, ::-1, 1], a[:, ::-1, 1], bm[:, ::-1], cm[:, ::-1], s0[:, 1])
    y = y_f + y_b[:, ::-1] + xs * p['ssd_d'][:, None]
    y = y.reshape(bsz, L, SSD_INNER) * jax.nn.silu(z)
    return rmsnorm(y, p['ssd_norm']), jnp.stack([s_f, s_b], axis=1)


def linear_scan(a, b, h0):
    b = b.at[:, 0].add(a[:, 0] * h0)
    def combine(left, right):
        return left[0] * right[0], right[0] * left[1] + right[1]
    return lax.associative_scan(combine, (a, b), axis=1)[1]


def rglru_mixer(u, s0, p):
    bsz, L, _ = u.shape
    s0 = s0.astype(F32)
    xr, gate = jnp.split(u, 2, axis=-1)
    xc = dwconv(xr, p['lru_conv'])
    xh = xc.reshape(bsz, L, LRU_HEADS, LRU_HD)
    r = jax.nn.sigmoid(jnp.einsum('blhi,dhij->bldhj', xh, p['lru_w_r']).reshape(bsz, L, N_DIR, LRU_W) + p['lru_b_r'])
    i = jax.nn.sigmoid(jnp.einsum('blhi,dhij->bldhj', xh, p['lru_w_i']).reshape(bsz, L, N_DIR, LRU_W) + p['lru_b_i'])
    log_a = -LRU_C * r * jax.nn.softplus(-p['lru_lambda'])
    a = jnp.exp(log_a)
    bterm = jnp.sqrt(-jnp.expm1(2.0 * log_a)) * i * xc[:, :, None]
    h_f = linear_scan(a[:, :, 0], bterm[:, :, 0], s0[:, 0])
    h_b = linear_scan(a[:, ::-1, 1], bterm[:, ::-1, 1], s0[:, 1])
    y = (h_f + h_b[:, ::-1]) * jax.nn.gelu(gate)
    return y, jnp.stack([h_f[:, -1], h_b[:, -1]], axis=1)


def gated_delta_chunked(q, k, v, g, beta, s0):
    bsz, H, L, _ = q.shape
    Dv = v.shape[-1]
    C = GDN_CHUNK
    nc = L // C
    q, k, v = (t.reshape(bsz, H, nc, C, -1) for t in (q, k, v))
    gc = jnp.cumsum(g.reshape(bsz, H, nc, C), axis=-1)
    beta = beta.reshape(bsz, H, nc, C, 1)
    incl = jnp.tril(jnp.ones((C, C), dtype=bool))
    strict = jnp.tril(jnp.ones((C, C), dtype=bool), k=-1)
    diff = gc[..., :, None] - gc[..., None, :]
    decay = jnp.where(incl, jnp.exp(jnp.where(incl, diff, 0.0)), 0.0)
    kb = k * beta
    a_mat = jnp.where(strict, jnp.einsum('bhncd,bhnsd->bhncs', kb, k) * decay, 0.0)
    rhs = jnp.concatenate([v * beta, kb * jnp.exp(gc)[..., None]], axis=-1)
    sol = lax.linalg.triangular_solve(a_mat + jnp.eye(C, dtype=a_mat.dtype), rhs,
                                      left_side=True, lower=True, unit_diagonal=True)
    u, w = sol[..., :Dv], sol[..., Dv:]
    attn = jnp.where(incl, jnp.einsum('bhncd,bhnsd->bhncs', q, k) * decay, 0.0)
    def step(s, inp):
        q_i, k_i, u_i, w_i, gc_i, attn_i = inp
        v_new = u_i - jnp.einsum('bhcd,bhde->bhce', w_i, s)
        o = (jnp.einsum('bhcd,bhde->bhce', q_i * jnp.exp(gc_i)[..., None], s)
             + jnp.einsum('bhcs,bhse->bhce', attn_i, v_new))
        g_last = gc_i[..., -1:]
        s = (s * jnp.exp(g_last)[..., None]
             + jnp.einsum('bhcd,bhce->bhde', k_i * jnp.exp(g_last - gc_i)[..., None], v_new))
        return s, o
    xs = tuple(jnp.moveaxis(t, 2, 0) for t in (q, k, u, w, gc, attn))
    s_final, o = lax.scan(step, s0, xs)
    return jnp.moveaxis(o, 0, 2).reshape(bsz, H, L, Dv), s_final


def gdn_mixer(u, s0, p):
    bsz, L, _ = u.shape
    s0 = s0.astype(F32)
    qkv, zg, b_raw, a_raw = jnp.split(u, [3 * GROUP_W, 4 * GROUP_W, 4 * GROUP_W + N_DIR * GDN_HEADS], axis=-1)
    q, k, v = jnp.split(jax.nn.silu(dwconv(qkv, p['gdn_conv'])), 3, axis=-1)
    heads = lambda t: t.reshape(bsz, L, GDN_HEADS, GDN_HEAD_DIM).transpose(0, 2, 1, 3)
    q = l2norm(heads(q)) * (GDN_HEAD_DIM ** -0.5)
    k = l2norm(heads(k))
    v = heads(v)
    beta = jax.nn.sigmoid(b_raw.reshape(bsz, L, N_DIR, GDN_HEADS)).transpose(0, 2, 3, 1)
    g = (-jnp.exp(p['gdn_a_log'])
         * jax.nn.softplus(a_raw.reshape(bsz, L, N_DIR, GDN_HEADS) + p['gdn_dt_bias'])).transpose(0, 2, 3, 1)
    o_f, s_f = gated_delta_chunked(q, k, v, g[:, 0], beta[:, 0], s0[:, 0])
    rev = lambda t: t[:, :, ::-1]
    o_b, s_b = gated_delta_chunked(rev(q), rev(k), rev(v), rev(g[:, 1]), rev(beta[:, 1]), s0[:, 1])
    o = (o_f + rev(o_b)).transpose(0, 2, 1, 3)
    o = rmsnorm(o, p['gdn_norm']) * jax.nn.silu(zg.reshape(bsz, L, GDN_HEADS, GDN_HEAD_DIM))
    return o.reshape(bsz, L, GROUP_W), jnp.stack([s_f, s_b], axis=1)


def trunk_layer(x, mod, s_lru, s_ssd, s_gdn, p):
    sh_m, sc_m, ga_m, sh_f, sc_f, ga_f = jnp.split(mod, 6, axis=-1)
    h = rmsnorm(x, p['g_mix']) * (1 + sc_m) + sh_m
    proj = (h @ p['w_in']).astype(F32)
    u_hy, u_ssd, u_lru, u_gdn = jnp.split(proj, IN_SPLITS, axis=-1)
    o_hy = hyena_mixer(u_hy, p)
    o_ssd, s_ssd = ssd_mixer(u_ssd, s_ssd, p)
    o_lru, s_lru = rglru_mixer(u_lru, s_lru, p)
    o_gdn, s_gdn = gdn_mixer(u_gdn, s_gdn, p)
    mixed = jnp.concatenate([o_hy, o_ssd, o_lru, o_gdn], axis=-1).astype(x.dtype)
    x = x + ga_m * (mixed @ p['w_out'])
    h = rmsnorm(x, p['g_ffn']) * (1 + sc_f) + sh_f
    x = x + ga_f * ((jax.nn.silu(h @ p['w_gate']) * (h @ p['w_up'])) @ p['w_down'])
    return x, s_lru, s_ssd, s_gdn


def setup_inputs(seed: int = 0) -> dict:
    key = jax.random.key(seed)
    keys = jax.random.split(key, 64)
    ctr = [0]
    def nk():
        ctr[0] += 1
        return keys[ctr[0] - 1]
    def nrm(shape, scale=1.0):
        return scale * jax.random.normal(nk(), shape, F32)
    def unif(shape, lo, hi):
        return jax.random.uniform(nk(), shape, F32, lo, hi)
    def gain(shape):
        return 1.0 + nrm(shape, 0.01)
    def dt_bias(shape):
        dt = jnp.exp(unif(shape, math.log(1e-3), math.log(1e-1)))
        return dt + jnp.log(-jnp.expm1(-dt))
    a0 = unif((DEPTH, N_DIR, LRU_W), 0.9, 0.999)
    s_lam = a0 ** (1.0 / LRU_C)
    return {
        'x_prompt': nrm((BATCH, SEQ, D_MODEL)),
        'x_sample': nrm((DEC_BATCH, DEC_SEQ, D_MODEL)),
        'state_lru': nrm((DEC_BATCH, DEPTH, N_DIR, LRU_W), 0.5),
        'state_ssd': nrm((DEC_BATCH, DEPTH, N_DIR, SSD_HEADS, SSD_HEADDIM, SSD_STATE), 0.3),
        'state_gdn': nrm((DEC_BATCH, DEPTH, N_DIR, GDN_HEADS, GDN_HEAD_DIM, GDN_HEAD_DIM), 0.3),
        'c': nrm((DEC_BATCH, D_MODEL)),
        'c_ctx': nrm((D_MODEL,)),
        'w_mod': nrm((DEPTH, D_MODEL, 6 * D_MODEL), 0.5 * D_MODEL ** -0.5),
        'b_mod': nrm((DEPTH, 6 * D_MODEL), 0.01),
        'g_mix': gain((DEPTH, D_MODEL)),
        'g_ffn': gain((DEPTH, D_MODEL)),
        'g_final': gain((D_MODEL,)),
        'w_in': nrm((DEPTH, D_MODEL, IN_W), D_MODEL ** -0.5),
        'w_out': nrm((DEPTH, D_MIX, D_MODEL), D_MIX ** -0.5),
        'hy_conv': nrm((DEPTH, HY_SHORT, HY_IN), HY_SHORT ** -0.5),
        'hy_w1': nrm((DEPTH, HY_EMB, HY_HIDDEN), HY_EMB ** -0.5),
        'hy_b1': nrm((DEPTH, HY_HIDDEN), 0.02),
        'hy_w2': nrm((DEPTH, HY_HIDDEN, HY_HIDDEN), HY_HIDDEN ** -0.5),
        'hy_b2': nrm((DEPTH, HY_HIDDEN), 0.02),
        'hy_w3': nrm((DEPTH, HY_HIDDEN, N_DIR * HY_W), HY_HIDDEN ** -0.5),
        'hy_freq': gain((DEPTH, 2, HY_HIDDEN)),
        'hy_bias': nrm((DEPTH, HY_W)),
        'ssd_conv': nrm((DEPTH, SSD_CONV, SSD_XBC), SSD_CONV ** -0.5),
        'ssd_dt_bias': dt_bias((DEPTH, N_DIR, SSD_HEADS)),
        'ssd_a_log': jnp.log(unif((DEPTH, N_DIR, SSD_HEADS), 1.0, 16.0)),
        'ssd_d': gain((DEPTH, SSD_HEADS)),
        'ssd_norm': gain((DEPTH, SSD_INNER)),
        'lru_conv': nrm((DEPTH, LRU_CONV, LRU_W), LRU_CONV ** -0.5),
        'lru_w_r': nrm((DEPTH, N_DIR, LRU_HEADS, LRU_HD, LRU_HD), LRU_HD ** -0.5),
        'lru_b_r': nrm((DEPTH, N_DIR, LRU_W), 0.01),
        'lru_w_i': nrm((DEPTH, N_DIR, LRU_HEADS, LRU_HD, LRU_HD), LRU_HD ** -0.5),
        'lru_b_i': nrm((DEPTH, N_DIR, LRU_W), 0.01),
        'lru_lambda': jnp.log(s_lam) - jnp.log1p(-s_lam),
        'gdn_conv': nrm((DEPTH, GDN_CONV, 3 * GROUP_W), GDN_CONV ** -0.5),
        'gdn_dt_bias': dt_bias((DEPTH, N_DIR, GDN_HEADS)),
        'gdn_a_log': jnp.log(unif((DEPTH, N_DIR, GDN_HEADS), 1.0, 16.0)),
        'gdn_norm': gain((DEPTH, GDN_HEAD_DIM)),
        'w_gate': nrm((DEPTH, D_MODEL, D_FF), D_MODEL ** -0.5),
        'w_up': nrm((DEPTH, D_MODEL, D_FF), D_MODEL ** -0.5),
        'w_down': nrm((DEPTH, D_FF, D_MODEL), D_FF ** -0.5),
    }


def reference(x_prompt, x_sample, state_lru, state_ssd, state_gdn, c, c_ctx, w_mod, b_mod,
              g_mix, g_ffn, g_final, w_in, w_out, hy_conv, hy_w1, hy_b1, hy_w2, hy_b2, hy_w3,
              hy_freq, hy_bias, ssd_conv, ssd_dt_bias, ssd_a_log, ssd_d, ssd_norm, lru_conv,
              lru_w_r, lru_b_r, lru_w_i, lru_b_i, lru_lambda, gdn_conv, gdn_dt_bias, gdn_a_log,
              gdn_norm, w_gate, w_up, w_down):
    b_ctx = x_prompt.shape[0]
    n_lat = x_sample.shape[1]
    x_ctx = x_prompt
    x_lat = x_sample + grid_pos_embed(n_lat).astype(x_sample.dtype)
    zero_lru = jnp.zeros((b_ctx, N_DIR, LRU_W), F32)
    zero_ssd = jnp.zeros((b_ctx, N_DIR, SSD_HEADS, SSD_HEADDIM, SSD_STATE), F32)
    zero_gdn = jnp.zeros((b_ctx, N_DIR, GDN_HEADS, GDN_HEAD_DIM, GDN_HEAD_DIM), F32)
    lru_out, ssd_out, gdn_out = [], [], []
    for l in range(DEPTH):
        p = {
            'g_mix': g_mix[l], 'g_ffn': g_ffn[l], 'w_in': w_in[l], 'w_out': w_out[l],
            'hy_conv': hy_conv[l], 'hy_w1': hy_w1[l], 'hy_b1': hy_b1[l], 'hy_w2': hy_w2[l],
            'hy_b2': hy_b2[l], 'hy_w3': hy_w3[l], 'hy_freq': hy_freq[l], 'hy_bias': hy_bias[l],
            'ssd_conv': ssd_conv[l], 'ssd_dt_bias': ssd_dt_bias[l], 'ssd_a_log': ssd_a_log[l],
            'ssd_d': ssd_d[l], 'ssd_norm': ssd_norm[l],
            'lru_conv': lru_conv[l], 'lru_w_r': lru_w_r[l], 'lru_b_r': lru_b_r[l],
            'lru_w_i': lru_w_i[l], 'lru_b_i': lru_b_i[l], 'lru_lambda': lru_lambda[l],
            'gdn_conv': gdn_conv[l], 'gdn_dt_bias': gdn_dt_bias[l], 'gdn_a_log': gdn_a_log[l],
            'gdn_norm': gdn_norm[l],
            'w_gate': w_gate[l], 'w_up': w_up[l], 'w_down': w_down[l],
        }
        mod_ctx = (jax.nn.silu(c_ctx)[None] @ w_mod[l] + b_mod[l])[:, None].astype(x_ctx.dtype)
        mod_lat = (jax.nn.silu(c) @ w_mod[l] + b_mod[l])[:, None].astype(x_lat.dtype)
        x_ctx, s_lru, s_ssd, s_gdn = trunk_layer(x_ctx, mod_ctx, zero_lru, zero_ssd, zero_gdn, p)
        lru_out.append(s_lru)
        ssd_out.append(s_ssd)
        gdn_out.append(s_gdn)
        x_lat, _, _, _ = trunk_layer(x_lat, mod_lat, state_lru[:, l], state_ssd[:, l], state_gdn[:, l], p)
    y_prompt = rmsnorm(x_ctx, g_final)
    y_sample = rmsnorm(x_lat, g_final)
    new_state_lru = jnp.stack(lru_out, axis=1)
    new_state_ssd = jnp.stack(ssd_out, axis=1)
    new_state_gdn = jnp.stack(gdn_out, axis=1)
    return (y_prompt, y_sample, new_state_lru, new_state_ssd, new_state_gdn)
```

```python
import functools
import math

import ml_dtypes
import numpy as np
import jax
import jax.numpy as jnp
from jax import lax
from jax.experimental import pallas as pl
from jax.experimental.pallas import tpu as pltpu

F32 = jnp.float32
BF16 = jnp.bfloat16
HI = lax.Precision.HIGHEST

D_MODEL = 1024
DEPTH = 2
EPS = 1e-6
GRID_W = 64
POS_BASE = 10000.0
GW = 256
D_FF = 2816
HD = 64
NH = 4
HY_BANDS = 16
HY_EMB = 1 + 2 * HY_BANDS
HY_HIDDEN = 64
LRU_C = 8.0

COL_HY = 0
COL_SSD = 768
COL_LRU = 1536
COL_GDN = 2048
COL_SMALL = 3072
IN_PACK = 3200
SM_DT, SM_B, SM_A = 0, 8, 16

SSD_Q = 128
GDN_C = 128
ROW_TILE = 512
V7X_VMEM_LIMIT = 56 * 1024 * 1024


def _cparams(n_axes=1):
    return pltpu.CompilerParams(dimension_semantics=("arbitrary",) * n_axes,
                                vmem_limit_bytes=V7X_VMEM_LIMIT)


def _silu(x):
    return x * jax.nn.sigmoid(x)


def _softplus(x):
    return jnp.maximum(x, 0.0) + jnp.log1p(jnp.exp(-jnp.abs(x)))


def _gelu_tanh(x):
    return 0.5 * x * (1.0 + jnp.tanh(math.sqrt(2.0 / math.pi) * (x + 0.044715 * (x * x * x))))


def _split_bf16(x):
    hi = x.astype(BF16)
    lo = (x - hi.astype(F32)).astype(BF16)
    return hi, lo


def _dot(a, b):
    return jnp.dot(a, b, preferred_element_type=F32)


def _dot_hi(a, b):
    return jnp.dot(a, b, precision=HI, preferred_element_type=F32)


def _dot3_const(ah, al, x):
    xh, xl = _split_bf16(x)
    return _dot(ah, xh) + _dot(ah, xl) + _dot(al, xh)


def _seg_sum(x, ones_bd):
    xh, xl = _split_bf16(x)
    return _dot(xh, ones_bd) + _dot(xl, ones_bd)


def _lane_iota(shape):
    return lax.broadcasted_iota(jnp.int32, shape, len(shape) - 1)


def _head_lanes(cols, rows):
    lane = _lane_iota((rows, GW))
    out = jnp.broadcast_to(cols[NH - 1], (rows, GW))
    for h in range(NH - 2, -1, -1):
        out = jnp.where(lane < (h + 1) * HD, cols[h], out)
    return out


def _dwconv(x, w_ref, pad_ref, n_rows, taps):
    ch = x.shape[1]
    left = taps // 2
    pad_ref[0:8, 0:ch] = jnp.zeros((8, ch), F32)
    pad_ref[8 + n_rows:16 + n_rows, 0:ch] = jnp.zeros((8, ch), F32)
    pad_ref[8:8 + n_rows, 0:ch] = x
    acc = None
    for j in range(taps):
        off = 8 - left + j
        term = pad_ref[off:off + n_rows, 0:ch] * w_ref[j:j + 1, :]
        acc = term if acc is None else acc + term
    return acc


def _mod_kernel(c_ref, w_ref, b_ref, o_ref):
    s = _silu(c_ref[...])
    o_ref[...] = _dot_hi(s, w_ref[...]) + b_ref[...]


def _mod_call(cvec, w_mod, b_mod):
    tn = 1536
    n = w_mod.shape[-1]
    return pl.pallas_call(
        _mod_kernel,
        grid=(DEPTH, n // tn),
        in_specs=[pl.BlockSpec((16, D_MODEL), lambda l, j: (0, 0)),
                  pl.BlockSpec((None, D_MODEL, tn), lambda l, j: (l, 0, j)),
                  pl.BlockSpec((None, 1, tn), lambda l, j: (l, 0, j))],
        out_specs=pl.BlockSpec((None, 16, tn), lambda l, j: (l, 0, j)),
        out_shape=jax.ShapeDtypeStruct((DEPTH, 16, n), F32),
        compiler_params=_cparams(2),
        name="adaln_mod",
    )(cvec, w_mod, b_mod.reshape(DEPTH, 1, n))


def _inproj_kernel(*refs, add_pos):
    if add_pos:
        x_ref, pos_ref, mod_ref, g_ref, w_ref, o_ref, x0_ref = refs
        x = x_ref[...] + pos_ref[...]
        x0_ref[...] = x
    else:
        x_ref, mod_ref, g_ref, w_ref, o_ref = refs
        x = x_ref[...]
    mod = mod_ref[...]
    sh = mod[:, 0:D_MODEL]
    sc = mod[:, D_MODEL:2 * D_MODEL]
    h = x * lax.rsqrt(jnp.mean(x * x, axis=-1, keepdims=True) + EPS) * g_ref[...]
    h = h * (1.0 + sc) + sh
    o_ref[...] = _dot(h.astype(BF16), w_ref[...])


def _inproj_call(x2d, pos, mods_l, g, w_pack, seq_len, is_lat):
    n = x2d.shape[0]
    tm = ROW_TILE
    per_seq = max(seq_len // tm, 1)
    mod_map = (lambda t: (1 + t // per_seq, 0, 0)) if is_lat else (lambda t: (0, 0, 0))
    in_specs = [pl.BlockSpec((tm, D_MODEL), lambda t: (t, 0))]
    args = [x2d]
    if pos is not None:
        in_specs.append(pl.BlockSpec((tm, D_MODEL), lambda t: (t % per_seq, 0)))
        args.append(pos)
    in_specs += [pl.BlockSpec((None, 1, 6 * D_MODEL), mod_map),
                 pl.BlockSpec((1, D_MODEL), lambda t: (0, 0)),
                 pl.BlockSpec((D_MODEL, IN_PACK), lambda t: (0, 0))]
    args += [mods_l, g.reshape(1, D_MODEL), w_pack]
    out_specs = pl.BlockSpec((tm, IN_PACK), lambda t: (t, 0))
    out_shape = jax.ShapeDtypeStruct((n, IN_PACK), F32)
    if pos is not None:
        out_specs = [out_specs, pl.BlockSpec((tm, D_MODEL), lambda t: (t, 0))]
        out_shape = [out_shape, jax.ShapeDtypeStruct((n, D_MODEL), F32)]
    return pl.pallas_call(
        functools.partial(_inproj_kernel, add_pos=pos is not None),
        grid=(n // tm,),
        in_specs=in_specs,
        out_specs=out_specs,
        out_shape=out_shape,
        compiler_params=_cparams(1),
        name="inproj_lat0" if pos is not None else "inproj",
    )(*args)


@functools.lru_cache(maxsize=None)
def _dft_tables(seq_len):
    n = 2 * seq_len - 1
    idx = np.arange(seq_len, dtype=np.int64)
    ang = (2.0 * np.pi / n) * ((idx[:, None] * idx[None, :]) % n).astype(np.float64)
    out = []
    for tab in (np.cos(ang), np.sin(ang)):
        hi = tab.astype(ml_dtypes.bfloat16)
        lo = (tab - hi.astype(np.float64)).astype(ml_dtypes.bfloat16)
        out += [hi, lo]
    return tuple(out)


@functools.lru_cache(maxsize=None)
def _hyena_consts(seq_len):
    t = np.linspace(0.0, 1.0, seq_len, dtype=np.float32)[:, None]
    w = np.float32(2.0 * math.pi / seq_len) * np.arange(seq_len, dtype=np.float32)[:, None]
    bands = np.linspace(1e-4, HY_BANDS - 1, HY_BANDS, dtype=np.float32)[None, :]
    feats = np.concatenate([t, np.cos(bands * w), -np.sin(bands * w)], axis=-1).astype(np.float32)
    feats_p = np.zeros((seq_len, 128), np.float32)
    feats_p[:, :HY_EMB] = feats
    max_decay = math.log(1e-2) / 0.3
    min_decay = math.log(1e-2) / 1.5
    deltas = np.abs(np.linspace(min_decay, max_decay, GW, dtype=np.float32))
    decay = np.exp(-t * deltas).astype(np.float32)
    return feats_p, np.concatenate([decay, decay], axis=1)


def _hyfilt_kernel(feat_ref, dec_ref, w1_ref, b1_ref, w2_ref, b2_ref, w3_ref, fr_ref,
                   ch_ref, cl_ref, sh_ref, sl_ref, hr_ref, hi_ref, *, seq_len):
    fr = fr_ref[...]
    h = jnp.sin(fr[0:1, :] * (_dot_hi(feat_ref[...], w1_ref[...]) + b1_ref[...]))
    h = jnp.sin(fr[1:2, :] * (_dot_hi(h, w2_ref[...]) + b2_ref[...]))
    h = _dot_hi(h, w3_ref[...]) * dec_ref[...]
    den = jnp.sum(jnp.abs(h[:, :GW]) + jnp.abs(h[:, GW:]), axis=0, keepdims=True)
    h = h / jnp.concatenate([den, den], axis=1)
    hc = _dot3_const(ch_ref[...], cl_ref[...], h)
    hs = _dot3_const(sh_ref[...], sl_ref[...], h)
    n = 2 * seq_len - 1
    row = lax.broadcasted_iota(jnp.int32, (seq_len, GW), 0)
    wgt = jnp.where(row == 0, 1.0 / n, 2.0 / n)
    hr_ref[...] = (hc[:, :GW] + hc[:, GW:]) * wgt
    hi_ref[...] = (hs[:, GW:] - hs[:, :GW]) * wgt


def _hyfilt_call(seq_len, hw):
    feats, decay = _hyena_consts(seq_len)
    tabs = _dft_tables(seq_len)
    full = lambda shape: pl.BlockSpec(shape, lambda l: (0,) * len(shape))
    lay = lambda shape: pl.BlockSpec((None,) + shape, lambda l: (l,) + (0,) * len(shape))
    return pl.pallas_call(
        functools.partial(_hyfilt_kernel, seq_len=seq_len),
        grid=(DEPTH,),
        in_specs=[full((seq_len, 128)), full((seq_len, 2 * GW)),
                  lay((128, 128)), lay((1, 128)), lay((128, 128)), lay((1, 128)), lay((128, 2 * GW)),
                  lay((2, 128))] + [full((seq_len, seq_len))] * 4,
        out_specs=[lay((seq_len, GW)), lay((seq_len, GW))],
        out_shape=[jax.ShapeDtypeStruct((DEPTH, seq_len, GW), F32)] * 2,
        compiler_params=_cparams(1),
        name=f"hyena_filter_{seq_len}",
    )(feats, decay, hw["w1"], hw["b1"], hw["w2"], hw["b2"], hw["w3"], hw["freq"], *tabs)


def _hyena_kernel(u_ref, cw_ref, bias_ref, hr_ref, hi_ref, ch_ref, cl_ref, sh_ref, sl_ref,
                  o_ref, pad_ref, *, seq_len, n_seq):
    for b in range(n_seq):
        uc = _dwconv(u_ref[b], cw_ref, pad_ref, seq_len, 3)
        x0 = uc[:, 0:GW]
        z = uc[:, 2 * GW:3 * GW] * uc[:, GW:2 * GW]
        za = _dot3_const(ch_ref[...], cl_ref[...], z)
        zb = _dot3_const(sh_ref[...], sl_ref[...], z)
        hr = hr_ref[...]
        hi = hi_ref[...]
        yr = za * hr + zb * hi
        yi = za * hi - zb * hr
        y = _dot3_const(ch_ref[...], cl_ref[...], yr) - _dot3_const(sh_ref[...], sl_ref[...], yi)
        o_ref[b] = x0 * (y + z * bias_ref[...])


def _hyena_call(proj3, conv_w, bias, hr, hi, seq_len, n_seq):
    bsz = proj3.shape[0]
    tabs = _dft_tables(seq_len)
    full = lambda shape: pl.BlockSpec(shape, lambda b: (0,) * len(shape))
    return pl.pallas_call(
        functools.partial(_hyena_kernel, seq_len=seq_len, n_seq=n_seq),
        grid=(bsz // n_seq,),
        in_specs=[pl.BlockSpec((n_seq, seq_len, 3 * GW), lambda b: (b, 0, COL_HY // (3 * GW))),
                  full((3, 3 * GW)), full((1, GW)), full((seq_len, GW)), full((seq_len, GW))]
                 + [full((seq_len, seq_len))] * 4,
        out_specs=pl.BlockSpec((n_seq, seq_len, GW), lambda b: (b, 0, 0)),
        out_shape=jax.ShapeDtypeStruct((bsz, seq_len, GW), F32),
        scratch_shapes=[pltpu.VMEM((seq_len + 16, 3 * GW), F32)],
        compiler_params=_cparams(1),
        name=f"hyena_{seq_len}",
    )(proj3, conv_w, bias.reshape(1, GW), hr, hi, *tabs)


def _tri_masks(q):
    row = lax.broadcasted_iota(jnp.int32, (q, q), 0)
    col = lax.broadcasted_iota(jnp.int32, (q, q), 1)
    return col <= row, col >= row


def _ssd_kernel(*refs, seq_len, has_s0, emit_state):
    it = iter(refs)
    zx_ref, sm_ref, conv_ref, dtb_r_ref, dtb_c_ref, an_r_ref, an_c_ref, dvec_ref, gn_ref = (
        next(it) for _ in range(9))
    s0_ref = next(it) if has_s0 else None
    o_ref = next(it)
    so_ref = next(it) if emit_state else None
    pad_ref, xs_ref, bm_ref, cm_ref, dtc_ref, ac_ref, ar_ref, y_ref, st_ref = it
    q = SSD_Q
    nc = seq_len // q

    xbc = _silu(_dwconv(zx_ref[:, GW:3 * GW], conv_ref, pad_ref, seq_len, 4))
    xs_ref[...] = xbc[:, 0:GW]
    bm_ref[...] = xbc[:, GW:GW + 128]
    cm_ref[...] = xbc[:, GW + 128:GW + 256]
    y_ref[...] = xbc[:, 0:GW] * dvec_ref[...]
    sm = sm_ref[...]
    dtc = _softplus(sm + dtb_r_ref[...])
    dtc_ref[...] = dtc
    ac_ref[...] = dtc * an_r_ref[...]
    sm_t = sm.T
    ar = _softplus(sm_t[SM_DT:SM_DT + 8, :] + dtb_c_ref[...]) * an_c_ref[...]
    for c in range(nc):
        ar_ref[c] = ar[:, c * q:(c + 1) * q]
    if has_s0:
        st_ref[...] = s0_ref[...]
    else:
        st_ref[...] = jnp.zeros(st_ref.shape, F32)

    lower, upper = _tri_masks(q)
    lo_f = lower.astype(F32)
    up_f = upper.astype(F32)
    lane128 = _lane_iota((q, 128))
    lane256 = _lane_iota((q, GW))
    st_row = lax.broadcasted_iota(jnp.int32, (128, GW), 0)
    st_col = lax.broadcasted_iota(jnp.int32, (128, GW), 1)
    st_mask = (st_row // HD) == (st_col // 128)

    def chunk_dir(c, d):
        rows = pl.ds(pl.multiple_of(c * q, q), q)
        xs_c = xs_ref[rows, :]
        b_c = bm_ref[rows, :]
        c_c = cm_ref[rows, :]
        dt_c = dtc_ref[rows, :]
        a_c = ac_ref[rows, :]
        a_r = ar_ref[c]
        if d == 0:
            cum_c = _dot_hi(lo_f, a_c)
            cum_r = _dot_hi(a_r, up_f)
            mask, last = lower, q - 1
        else:
            cum_c = _dot_hi(up_f, a_c)
            cum_r = _dot_hi(a_r, lo_f)
            mask, last = upper, 0
        b_t = b_c.T.astype(BF16)
        c_g = jnp.concatenate([jnp.where(lane128 < HD, c_c, 0.0),
                               jnp.where(lane128 >= HD, c_c, 0.0)], axis=0).astype(BF16)
        g_mat = _dot(c_g, b_t)
        w_parts = []
        for h in range(NH):
            j = d * NH + h
            g = h // 2
            diff = cum_c[:, j:j + 1] - cum_r[j:j + 1, :]
            m = jnp.where(mask, jnp.exp(jnp.minimum(diff, 0.0)), 0.0)
            w_parts.append((g_mat[g * q:(g + 1) * q, :] * m).astype(BF16))
        w_cat = jnp.concatenate(w_parts, axis=1)
        cols = [cum_c[:, d * NH + h:d * NH + h + 1] for h in range(NH)]
        lasts = [cum_c[last:last + 1, d * NH + h:d * NH + h + 1] for h in range(NH)]
        xdt = xs_c * _head_lanes([dt_c[:, d * NH + h:d * NH + h + 1] for h in range(NH)], q)
        xdt_b = xdt.astype(BF16)
        stack = jnp.concatenate(
            [jnp.where((lane256 // HD) == h, xdt_b, jnp.zeros_like(xdt_b)) for h in range(NH)], axis=0)
        y_diag = _dot(w_cat, stack)
        ss = st_ref[d]
        y_off = _dot(c_c.astype(BF16), ss.astype(BF16)) * _head_lanes([jnp.exp(cc) for cc in cols], q)
        y_ref[rows, :] += y_diag + y_off
        d_in = _head_lanes([jnp.exp(lasts[h] - cols[h]) for h in range(NH)], q)
        upd = _dot(b_t, (xdt * d_in).astype(BF16))
        dec = _head_lanes([jnp.exp(lv) for lv in lasts], 1)
        st_ref[d] = ss * dec + jnp.where(st_mask, upd, 0.0)

    def body(i, carry):
        chunk_dir(i, 0)
        chunk_dir(nc - 1 - i, 1)
        return carry

    lax.fori_loop(0, nc, body, 0)

    y = y_ref[...] * _silu(zx_ref[:, 0:GW])
    y = y * lax.rsqrt(jnp.mean(y * y, axis=-1, keepdims=True) + EPS) * gn_ref[...]
    o_ref[...] = y
    if emit_state:
        so_ref[...] = st_ref[...]


def _ssd_call(proj3, p, s0, seq_len, emit_state):
    bsz = proj3.shape[0]
    nc = seq_len // SSD_Q
    full = lambda shape: pl.BlockSpec(shape, lambda b: (0,) * len(shape))
    in_specs = [pl.BlockSpec((None, seq_len, 3 * GW), lambda b: (b, 0, COL_SSD // (3 * GW))),
                pl.BlockSpec((None, seq_len, 128), lambda b: (b, 0, COL_SMALL // 128)),
                full((4, 2 * GW)), full((1, 128)), full((8, 1)), full((1, 128)), full((8, 1)),
                full((1, GW)), full((1, GW))]
    args = [proj3, proj3, p["conv"], p["dtb_row"], p["dtb_col"], p["an_row"], p["an_col"], p["dvec"], p["gn"]]
    if s0 is not None:
        in_specs.append(pl.BlockSpec((None, 2, 128, GW), lambda b: (b, 0, 0, 0)))
        args.append(s0)
    out_specs = [pl.BlockSpec((None, seq_len, GW), lambda b: (b, 0, 0))]
    out_shape = [jax.ShapeDtypeStruct((bsz, seq_len, GW), F32)]
    if emit_state:
        out_specs.append(pl.BlockSpec((None, 2, 128, GW), lambda b: (b, 0, 0, 0)))
        out_shape.append(jax.ShapeDtypeStruct((bsz, 2, 128, GW), F32))
    scratch = [pltpu.VMEM((seq_len + 16, 2 * GW), F32),
               pltpu.VMEM((seq_len, GW), F32),
               pltpu.VMEM((seq_len, 128), F32),
               pltpu.VMEM((seq_len, 128), F32),
               pltpu.VMEM((seq_len, 128), F32),
               pltpu.VMEM((seq_len, 128), F32),
               pltpu.VMEM((nc, 8, SSD_Q), F32),
               pltpu.VMEM((seq_len, GW), F32),
               pltpu.VMEM((2, 128, GW), F32)]
    return pl.pallas_call(
        functools.partial(_ssd_kernel, seq_len=seq_len, has_s0=s0 is not None, emit_state=emit_state),
        grid=(bsz,),
        in_specs=in_specs,
        out_specs=out_specs,
        out_shape=out_shape,
        scratch_shapes=scratch,
        compiler_params=_cparams(1),
        name=f"ssd_{seq_len}",
    )(*args)


def _lru_kernel(*refs, seq_len, has_s0, emit_state):
    it = iter(refs)
    u_ref, conv_ref, w_ref, bias_ref, sp_ref = (next(it) for _ in range(5))
    s0_ref = next(it) if has_s0 else None
    o_ref = next(it)
    so_ref = next(it) if emit_state else None
    pad_ref, af_ref, bf_ref, ab_ref, bb_ref = it
    n = seq_len

    xc = _dwconv(u_ref[:, 0:GW], conv_ref, pad_ref, n, 4)
    ri = jax.nn.sigmoid(_dot(xc.astype(BF16), w_ref[...]) + bias_ref[...])
    log_a = -LRU_C * ri[:, 0:2 * GW] * sp_ref[...]
    a = jnp.exp(log_a)
    bt = jnp.sqrt(jnp.maximum(1.0 - a * a, 0.0)) * ri[:, 2 * GW:4 * GW] * jnp.concatenate([xc, xc], axis=1)

    ones8 = jnp.ones((8, GW), F32)
    zeros8 = jnp.zeros((8, GW), F32)
    for a_ref_, b_ref_, lo in ((af_ref, bf_ref, 0), (ab_ref, bb_ref, GW)):
        a_ref_[0:8, :] = ones8
        a_ref_[8 + n:16 + n, :] = ones8
        b_ref_[0:8, :] = zeros8
        b_ref_[8 + n:16 + n, :] = zeros8
        a_ref_[8:8 + n, :] = a[:, lo:lo + GW]
        b_ref_[8:8 + n, :] = bt[:, lo:lo + GW]
    if has_s0:
        s0 = s0_ref[...]
        bf_ref[8:9, :] = bf_ref[8:9, :] + af_ref[8:9, :] * s0[0:1, :]
        bb_ref[7 + n:8 + n, :] = bb_ref[7 + n:8 + n, :] + ab_ref[7 + n:8 + n, :] * s0[1:2, :]

    d = 1
    while d < n:
        if d < 8:
            a_cur = af_ref[8:8 + n, :]
            af_ref[8:8 + n, :] = a_cur * af_ref[8 - d:8 - d + n, :]
            bf_ref[8:8 + n, :] = a_cur * bf_ref[8 - d:8 - d + n, :] + bf_ref[8:8 + n, :]
            a_cur = ab_ref[8:8 + n, :]
            ab_ref[8:8 + n, :] = a_cur * ab_ref[8 + d:8 + d + n, :]
            bb_ref[8:8 + n, :] = a_cur * bb_ref[8 + d:8 + d + n, :] + bb_ref[8:8 + n, :]
        else:
            m = n - d
            a_cur = af_ref[8 + d:8 + n, :]
            a_new = a_cur * af_ref[8:8 + m, :]
            b_new = a_cur * bf_ref[8:8 + m, :] + bf_ref[8 + d:8 + n, :]
            af_ref[8 + d:8 + n, :] = a_new
            bf_ref[8 + d:8 + n, :] = b_new
            a_cur = ab_ref[8:8 + m, :]
            a_new = a_cur * ab_ref[8 + d:8 + n, :]
            b_new = a_cur * bb_ref[8 + d:8 + n, :] + bb_ref[8:8 + m, :]
            ab_ref[8:8 + m, :] = a_new
            bb_ref[8:8 + m, :] = b_new
        d *= 2

    y = (bf_ref[8:8 + n, :] + bb_ref[8:8 + n, :]) * _gelu_tanh(u_ref[:, GW:2 * GW])
    o_ref[...] = y
    if emit_state:
        so_ref[0:1, :] = bf_ref[7 + n:8 + n, :]
        so_ref[1:2, :] = bb_ref[8:9, :]


def _lru_call(proj3, p, s0, seq_len, emit_state):
    bsz = proj3.shape[0]
    full = lambda shape: pl.BlockSpec(shape, lambda b: (0,) * len(shape))
    in_specs = [pl.BlockSpec((None, seq_len, 2 * GW), lambda b: (b, 0, COL_LRU // (2 * GW))),
                full((4, GW)), full((GW, 4 * GW)), full((1, 4 * GW)), full((1, 2 * GW))]
    args = [proj3, p["conv"], p["w"], p["bias"], p["sp"]]
    if s0 is not None:
        in_specs.append(pl.BlockSpec((None, 2, GW), lambda b: (b, 0, 0)))
        args.append(s0)
    out_specs = [pl.BlockSpec((None, seq_len, GW), lambda b: (b, 0, 0))]
    out_shape = [jax.ShapeDtypeStruct((bsz, seq_len, GW), F32)]
    if emit_state:
        out_specs.append(pl.BlockSpec((None, 2, GW), lambda b: (b, 0, 0)))
        out_shape.append(jax.ShapeDtypeStruct((bsz, 2, GW), F32))
    scratch = [pltpu.VMEM((seq_len + 16, GW), F32)] * 5
    return pl.pallas_call(
        functools.partial(_lru_kernel, seq_len=seq_len, has_s0=s0 is not None, emit_state=emit_state),
        grid=(bsz,),
        in_specs=in_specs,
        out_specs=out_specs,
        out_shape=out_shape,
        scratch_shapes=scratch,
        compiler_params=_cparams(1),
        name=f"rglru_{seq_len}",
    )(*args)


TRI_BASE = 8


def _unit_tri_inverse(a_mat, eye, same_blk):
    c = a_mat.shape[0]
    d = jnp.where(same_blk[TRI_BASE], a_mat, 0.0)
    p = eye - d
    x = _dot_hi(d, d)
    r = _dot_hi(x, jnp.concatenate([p, x], axis=1))
    p = p + r[:, 0:c]
    t = p + _dot_hi(r[:, c:2 * c], p)
    size = 2 * TRI_BASE
    while size <= c:
        e = jnp.where(same_blk[size] & jnp.logical_not(same_blk[size // 2]), a_mat, 0.0)
        t = t - _dot_hi(_dot_hi(t, e), t)
        size *= 2
    return t


def _gdn_kernel(*refs, seq_len, has_s0, emit_state):
    it = iter(refs)
    u_ref, sm_ref, conv_ref, dtb_r_ref, dtb_c_ref, an_r_ref, an_c_ref, gn_ref, ones_ref = (
        next(it) for _ in range(9))
    s0_ref = next(it) if has_s0 else None
    o_ref = next(it)
    so_ref = next(it) if emit_state else None
    pad_ref, q_ref, k_ref, v_ref, be_ref, gc_ref, gr_ref, oa_ref, st_ref = it
    c = GDN_C
    nc = seq_len // c
    ones_bd = ones_ref[...]

    qkv = _silu(_dwconv(u_ref[:, 0:3 * GW], conv_ref, pad_ref, seq_len, 4))
    qq = qkv[:, 0:GW]
    kk = qkv[:, GW:2 * GW]
    q_ref[...] = qq * lax.rsqrt(_seg_sum(qq * qq, ones_bd) + EPS) * (HD ** -0.5)
    k_ref[...] = kk * lax.rsqrt(_seg_sum(kk * kk, ones_bd) + EPS)
    v_ref[...] = qkv[:, 2 * GW:3 * GW]
    sm = sm_ref[...]
    be_ref[...] = jax.nn.sigmoid(sm)
    gc_ref[...] = an_r_ref[...] * _softplus(sm + dtb_r_ref[...])
    sm_t = sm.T
    g_r = an_c_ref[...] * _softplus(sm_t[SM_A:SM_A + 8, :] + dtb_c_ref[...])
    for i in range(nc):
        gr_ref[i] = g_r[:, i * c:(i + 1) * c]
    oa_ref[...] = jnp.zeros(oa_ref.shape, F32)
    if has_s0:
        st_ref[...] = s0_ref[...]
    else:
        st_ref[...] = jnp.zeros(st_ref.shape, F32)

    lower, upper = _tri_masks(c)
    lo_f = lower.astype(F32)
    up_f = upper.astype(F32)
    row_i = lax.broadcasted_iota(jnp.int32, (c, c), 0)
    col_i = lax.broadcasted_iota(jnp.int32, (c, c), 1)
    eye = (row_i == col_i).astype(F32)
    same_blk = {}
    size = TRI_BASE
    while size <= c:
        same_blk[size] = (row_i // size) == (col_i // size)
        size *= 2
    lane256 = _lane_iota((c, GW))
    lane512 = _lane_iota((c, 2 * GW))
    bd_row = lax.broadcasted_iota(jnp.int32, (GW, GW), 0)
    bd_col = lax.broadcasted_iota(jnp.int32, (GW, GW), 1)
    bd_mask = (bd_row // HD) == (bd_col // HD)

    def chunk_dir(ci, d):
        rows = pl.ds(pl.multiple_of(ci * c, c), c)
        q_c = q_ref[rows, :]
        k_c = k_ref[rows, :]
        v_c = v_ref[rows, :]
        be = be_ref[rows, :]
        g_c = gc_ref[rows, :]
        g_row = gr_ref[ci]
        if d == 0:
            cum_c = _dot_hi(lo_f, g_c)
            cum_r = _dot_hi(g_row, up_f)
            incl, strict, last = lower, col_i < row_i, c - 1
        else:
            cum_c = _dot_hi(up_f, g_c)
            cum_r = _dot_hi(g_row, lo_f)
            incl, strict, last = upper, col_i > row_i, 0
        k_t = k_c.T.astype(BF16)
        k_b = k_c.astype(BF16)
        q_b = q_c.astype(BF16)
        zero_b = jnp.zeros_like(k_b)
        stack_kq = jnp.concatenate(
            [jnp.where((lane256 // HD) == h, k_b, zero_b) for h in range(NH)]
            + [jnp.where((lane256 // HD) == h, q_b, zero_b) for h in range(NH)], axis=0)
        kq = _dot(stack_kq, k_t)
        t_parts, attn_parts = [], []
        for h in range(NH):
            j = d * NH + h
            diff = cum_c[:, SM_A + j:SM_A + j + 1] - cum_r[j:j + 1, :]
            dec = jnp.where(incl, jnp.exp(jnp.minimum(diff, 0.0)), 0.0)
            a_mat = jnp.where(strict, kq[h * c:(h + 1) * c, :] * be[:, SM_B + j:SM_B + j + 1] * dec, 0.0)
            t_parts.append(_unit_tri_inverse(a_mat, eye, same_blk))
            attn_parts.append((kq[(NH + h) * c:(NH + h + 1) * c, :] * dec).astype(BF16))
        t_cat = jnp.concatenate(t_parts, axis=1)
        cols = [cum_c[:, SM_A + d * NH + h:SM_A + d * NH + h + 1] for h in range(NH)]
        lasts = [cum_c[last:last + 1, SM_A + d * NH + h:SM_A + d * NH + h + 1] for h in range(NH)]
        b_l = _head_lanes([be[:, SM_B + d * NH + h:SM_B + d * NH + h + 1] for h in range(NH)], c)
        e_l = _head_lanes([jnp.exp(cc) for cc in cols], c)
        rhs = jnp.concatenate([v_c * b_l, k_c * b_l * e_l], axis=1)
        stack_r = jnp.concatenate(
            [jnp.where(((lane512 % GW) // HD) == h, rhs, 0.0) for h in range(NH)], axis=0)
        uw = _dot_hi(t_cat, stack_r)
        ss = st_ref[d]
        wq = jnp.concatenate([uw[:, GW:2 * GW], q_c * e_l], axis=0).astype(BF16)
        r2 = _dot(wq, ss.astype(BF16))
        v_new = uw[:, 0:GW] - r2[0:c, :]
        v_new_b = v_new.astype(BF16)
        stack_v = jnp.concatenate(
            [jnp.where((lane256 // HD) == h, v_new_b, jnp.zeros_like(v_new_b)) for h in range(NH)], axis=0)
        o_c = r2[c:2 * c, :] + _dot(jnp.concatenate(attn_parts, axis=1), stack_v)
        oa_ref[rows, :] += o_c
        k_dec = k_c * _head_lanes([jnp.exp(lasts[h] - cols[h]) for h in range(NH)], c)
        upd = _dot(k_dec.T.astype(BF16), v_new_b)
        dec_row = _head_lanes([jnp.exp(lv) for lv in lasts], 1)
        st_ref[d] = ss * dec_row + jnp.where(bd_mask, upd, 0.0)

    def body(i, carry):
        chunk_dir(i, 0)
        chunk_dir(nc - 1 - i, 1)
        return carry

    lax.fori_loop(0, nc, body, 0)

    o = oa_ref[...]
    o = o * lax.rsqrt(_seg_sum(o * o, ones_bd) * (1.0 / HD) + EPS) * gn_ref[...]
    o_ref[...] = o * _silu(u_ref[:, 3 * GW:4 * GW])
    if emit_state:
        so_ref[...] = st_ref[...]


def _gdn_call(proj3, p, s0, seq_len, emit_state):
    bsz = proj3.shape[0]
    nc = seq_len // GDN_C
    full = lambda shape: pl.BlockSpec(shape, lambda b: (0,) * len(shape))
    in_specs = [pl.BlockSpec((None, seq_len, 4 * GW), lambda b: (b, 0, COL_GDN // (4 * GW))),
                pl.BlockSpec((None, seq_len, 128), lambda b: (b, 0, COL_SMALL // 128)),
                full((4, 3 * GW)), full((1, 128)), full((8, 1)), full((1, 128)), full((8, 1)),
                full((1, GW)), full((GW, GW))]
    args = [proj3, proj3, p["conv"], p["dtb_row"], p["dtb_col"], p["an_row"], p["an_col"], p["gn"], p["ones"]]
    if s0 is not None:
        in_specs.append(pl.BlockSpec((None, 2, GW, GW), lambda b: (b, 0, 0, 0)))
        args.append(s0)
    out_specs = [pl.BlockSpec((None, seq_len, GW), lambda b: (b, 0, 0))]
    out_shape = [jax.ShapeDtypeStruct((bsz, seq_len, GW), F32)]
    if emit_state:
        out_specs.append(pl.BlockSpec((None, 2, GW, GW), lambda b: (b, 0, 0, 0)))
        out_shape.append(jax.ShapeDtypeStruct((bsz, 2, GW, GW), F32))
    scratch = [pltpu.VMEM((seq_len + 16, 3 * GW), F32),
               pltpu.VMEM((seq_len, GW), F32),
               pltpu.VMEM((seq_len, GW), F32),
               pltpu.VMEM((seq_len, GW), F32),
               pltpu.VMEM((seq_len, 128), F32),
               pltpu.VMEM((seq_len, 128), F32),
               pltpu.VMEM((nc, 8, GDN_C), F32),
               pltpu.VMEM((seq_len, GW), F32),
               pltpu.VMEM((2, GW, GW), F32)]
    return pl.pallas_call(
        functools.partial(_gdn_kernel, seq_len=seq_len, has_s0=s0 is not None, emit_state=emit_state),
        grid=(bsz,),
        in_specs=in_specs,
        out_specs=out_specs,
        out_shape=out_shape,
        scratch_shapes=scratch,
        compiler_params=_cparams(1),
        name=f"gdn_{seq_len}",
    )(*args)


def _ffn_kernel(x_ref, ohy_ref, ossd_ref, olru_ref, ogdn_ref, mod_ref, g_ref, wo_ref, wg_ref, wu_ref,
                wd_ref, gfin_ref, o_ref, *, final_norm):
    mod = mod_ref[...]
    ga_m = mod[:, 2 * D_MODEL:3 * D_MODEL]
    sh_f = mod[:, 3 * D_MODEL:4 * D_MODEL]
    sc_f = mod[:, 4 * D_MODEL:5 * D_MODEL]
    ga_f = mod[:, 5 * D_MODEL:6 * D_MODEL]
    mo = None
    for i, r in enumerate((ohy_ref, ossd_ref, olru_ref, ogdn_ref)):
        part = _dot(r[...].astype(BF16), wo_ref[i * GW:(i + 1) * GW, :])
        mo = part if mo is None else mo + part
    x = x_ref[...] + ga_m * mo
    h = x * lax.rsqrt(jnp.mean(x * x, axis=-1, keepdims=True) + EPS) * g_ref[...]
    h = (h * (1.0 + sc_f) + sh_f).astype(BF16)
    half = D_FF // 2
    ff = None
    for i in range(2):
        gate = _dot(h, wg_ref[:, i * half:(i + 1) * half])
        up = _dot(h, wu_ref[:, i * half:(i + 1) * half])
        part = _dot((_silu(gate) * up).astype(BF16), wd_ref[i * half:(i + 1) * half, :])
        ff = part if ff is None else ff + part
    x = x + ga_f * ff
    if final_norm:
        x = x * lax.rsqrt(jnp.mean(x * x, axis=-1, keepdims=True) + EPS) * gfin_ref[...]
    o_ref[...] = x


def _ffn_call(x2d, outs, mods_l, g, wo, wg, wu, wd, g_final, seq_len, is_lat, final_norm):
    n = x2d.shape[0]
    tm = ROW_TILE
    per_seq = seq_len // tm
    mod_map = (lambda t: (1 + t // per_seq, 0, 0)) if is_lat else (lambda t: (0, 0, 0))
    tile = lambda w: pl.BlockSpec((tm, w), lambda t: (t, 0))
    res = lambda shape: pl.BlockSpec(shape, lambda t: (0, 0), pipeline_mode=pl.Buffered(1))
    return pl.pallas_call(
        functools.partial(_ffn_kernel, final_norm=final_norm),
        grid=(n // tm,),
        in_specs=[tile(D_MODEL), tile(GW), tile(GW), tile(GW), tile(GW),
                  pl.BlockSpec((None, 1, 6 * D_MODEL), mod_map),
                  res((1, D_MODEL)), res((D_MODEL, D_MODEL)), res((D_MODEL, D_FF)), res((D_MODEL, D_FF)),
                  res((D_FF, D_MODEL)), res((1, D_MODEL))],
        out_specs=tile(D_MODEL),
        out_shape=jax.ShapeDtypeStruct((n, D_MODEL), F32),
        compiler_params=_cparams(1),
        name="outproj_ffn",
    )(x2d, *outs, mods_l, g.reshape(1, D_MODEL), wo, wg, wu, wd, g_final.reshape(1, D_MODEL))


def _grid_pos_embed(n_tokens):
    rows = n_tokens // GRID_W
    rr, cc = np.meshgrid(np.arange(rows, dtype=np.float32), np.arange(GRID_W, dtype=np.float32), indexing="ij")
    quarter = D_MODEL // 4
    omega = (1.0 / (np.float32(POS_BASE) ** (np.arange(quarter, dtype=np.float32) / quarter))).astype(np.float32)

    def enc(pos):
        ang = pos.reshape(-1)[:, None] * omega[None, :]
        return np.concatenate([np.sin(ang), np.cos(ang)], axis=-1)

    return jnp.asarray(np.concatenate([enc(rr), enc(cc)], axis=-1).astype(np.float32))


def _pad_small(vec8, offset):
    row = jnp.zeros((1, 128), F32).at[0, offset:offset + 8].set(vec8)
    return row, vec8.reshape(8, 1)


def _pack_w_in(w):
    hy_in, ssd_in, lru_in = 768, 776, 512
    o_ssd = hy_in
    o_lru = o_ssd + ssd_in
    o_gdn = o_lru + lru_in
    pieces = [w[:, 0:o_ssd + 768],
              w[:, o_lru:o_gdn + 1024],
              w[:, o_ssd + 768:o_ssd + 776],
              w[:, o_gdn + 1024:o_gdn + 1040],
              jnp.zeros((D_MODEL, IN_PACK - 3096), w.dtype)]
    return jnp.concatenate(pieces, axis=1).astype(BF16)


def _ssd_state_in(s):
    st = jnp.transpose(s, (0, 1, 4, 2, 3))
    grp = (jnp.arange(NH) // 2)[None, :] == jnp.arange(2)[:, None]
    full = st[:, :, None] * grp[None, None, :, None, :, None].astype(s.dtype)
    return full.reshape(s.shape[0], 2, 128, GW)


def _ssd_state_out(ss):
    bsz = ss.shape[0]
    blocks = ss.reshape(bsz, 2, 2, HD, NH, HD)
    pick = jnp.stack([blocks[:, :, h // 2, :, h, :] for h in range(NH)], axis=2)
    return jnp.transpose(pick, (0, 1, 2, 4, 3))


def _gdn_state_in(s):
    bsz = s.shape[0]
    eye = jnp.eye(NH, dtype=s.dtype)
    full = s[:, :, :, :, None, :] * eye[None, None, :, None, :, None]
    return full.reshape(bsz, 2, GW, GW)


def _gdn_state_out(ss):
    bsz = ss.shape[0]
    blocks = ss.reshape(bsz, 2, NH, HD, NH, HD)
    return jnp.stack([blocks[:, :, h, :, h, :] for h in range(NH)], axis=2)


def kernel(x_prompt, x_sample, state_lru, state_ssd, state_gdn, c, c_ctx, w_mod, b_mod, g_mix, g_ffn, g_final, w_in, w_out, hy_conv, hy_w1, hy_b1, hy_w2, hy_b2, hy_w3, hy_freq, hy_bias, ssd_conv, ssd_dt_bias, ssd_a_log, ssd_d, ssd_norm, lru_conv, lru_w_r, lru_b_r, lru_w_i, lru_b_i, lru_lambda, gdn_conv, gdn_dt_bias, gdn_a_log, gdn_norm, w_gate, w_up, w_down):
    b_ctx, l_ctx, _ = x_prompt.shape
    b_lat, l_lat, _ = x_sample.shape

    cvec = jnp.zeros((16, D_MODEL), F32).at[0].set(c_ctx).at[1:1 + b_lat].set(c)
    mods = _mod_call(cvec, w_mod, b_mod).reshape(DEPTH, 16, 1, 6 * D_MODEL)

    hw = {"w1": jnp.zeros((DEPTH, 128, 128), F32).at[:, :HY_EMB, :HY_HIDDEN].set(hy_w1),
          "b1": jnp.zeros((DEPTH, 1, 128), F32).at[:, 0, :HY_HIDDEN].set(hy_b1),
          "w2": jnp.zeros((DEPTH, 128, 128), F32).at[:, :HY_HIDDEN, :HY_HIDDEN].set(hy_w2),
          "b2": jnp.zeros((DEPTH, 1, 128), F32).at[:, 0, :HY_HIDDEN].set(hy_b2),
          "w3": jnp.zeros((DEPTH, 128, 2 * GW), F32).at[:, :HY_HIDDEN, :].set(hy_w3),
          "freq": jnp.zeros((DEPTH, 2, 128), F32).at[:, :, :HY_HIDDEN].set(hy_freq)}
    filt = {l_ctx: _hyfilt_call(l_ctx, hw), l_lat: _hyfilt_call(l_lat, hw)}

    ones_bd = jnp.asarray(np.kron(np.eye(NH, dtype=np.float32), np.ones((HD, HD), np.float32))).astype(BF16)
    pos = _grid_pos_embed(l_lat)

    x_ctx = x_prompt.reshape(b_ctx * l_ctx, D_MODEL)
    x_lat = x_sample.reshape(b_lat * l_lat, D_MODEL)
    lru_out, ssd_out, gdn_out = [], [], []
    for l in range(DEPTH):
        w_pack = _pack_w_in(w_in[l])
        wo, wg, wu, wd = (w.astype(BF16) for w in (w_out[l], w_gate[l], w_up[l], w_down[l]))
        ssd_dtb_r, ssd_dtb_c = _pad_small(ssd_dt_bias[l].reshape(8), SM_DT)
        ssd_an_r, ssd_an_c = _pad_small(-jnp.exp(ssd_a_log[l].reshape(8)), SM_DT)
        ssd_p = {"conv": ssd_conv[l], "dtb_row": ssd_dtb_r, "dtb_col": ssd_dtb_c, "an_row": ssd_an_r,
                 "an_col": ssd_an_c, "dvec": jnp.repeat(ssd_d[l], HD).reshape(1, GW),
                 "gn": ssd_norm[l].reshape(1, GW)}
        eye_h = jnp.eye(NH, dtype=F32)
        blockdiag = lambda w: (w[:, :, :, None, :] * eye_h[None, :, None, :, None]).reshape(2, GW, GW)
        w_r, w_i = blockdiag(lru_w_r[l]), blockdiag(lru_w_i[l])
        lru_p = {"conv": lru_conv[l],
                 "w": jnp.concatenate([w_r[0], w_r[1], w_i[0], w_i[1]], axis=1).astype(BF16),
                 "bias": jnp.concatenate([lru_b_r[l].reshape(1, 2 * GW), lru_b_i[l].reshape(1, 2 * GW)], axis=1),
                 "sp": jax.nn.softplus(-lru_lambda[l]).reshape(1, 2 * GW)}
        gdn_dtb_r, gdn_dtb_c = _pad_small(gdn_dt_bias[l].reshape(8), SM_A)
        gdn_an_r, gdn_an_c = _pad_small(-jnp.exp(gdn_a_log[l].reshape(8)), SM_A)
        gdn_p = {"conv": gdn_conv[l], "dtb_row": gdn_dtb_r, "dtb_col": gdn_dtb_c, "an_row": gdn_an_r,
                 "an_col": gdn_an_c, "gn": jnp.tile(gdn_norm[l], NH).reshape(1, GW), "ones": ones_bd}
        last = l == DEPTH - 1

        proj = _inproj_call(x_ctx, None, mods[l], g_mix[l], w_pack, l_ctx, False).reshape(b_ctx, l_ctx, IN_PACK)
        o_hy = _hyena_call(proj, hy_conv[l], hy_bias[l], filt[l_ctx][0][l], filt[l_ctx][1][l], l_ctx, 4)
        o_ssd, s_ssd = _ssd_call(proj, ssd_p, None, l_ctx, True)
        o_lru, s_lru = _lru_call(proj, lru_p, None, l_ctx, True)
        o_gdn, s_gdn = _gdn_call(proj, gdn_p, None, l_ctx, True)
        outs = [o.reshape(b_ctx * l_ctx, GW) for o in (o_hy, o_ssd, o_lru, o_gdn)]
        x_ctx = _ffn_call(x_ctx, outs, mods[l], g_ffn[l], wo, wg, wu, wd, g_final, l_ctx, False, last)
        lru_out.append(s_lru)
        ssd_out.append(_ssd_state_out(s_ssd))
        gdn_out.append(_gdn_state_out(s_gdn))

        if l == 0:
            proj, x_lat = _inproj_call(x_lat, pos, mods[l], g_mix[l], w_pack, l_lat, True)
        else:
            proj = _inproj_call(x_lat, None, mods[l], g_mix[l], w_pack, l_lat, True)
        proj = proj.reshape(b_lat, l_lat, IN_PACK)
        o_hy = _hyena_call(proj, hy_conv[l], hy_bias[l], filt[l_lat][0][l], filt[l_lat][1][l], l_lat, 1)
        o_ssd, = _ssd_call(proj, ssd_p, _ssd_state_in(state_ssd[:, l]), l_lat, False)
        o_lru, = _lru_call(proj, lru_p, state_lru[:, l], l_lat, False)
        o_gdn, = _gdn_call(proj, gdn_p, _gdn_state_in(state_gdn[:, l]), l_lat, False)
        outs = [o.reshape(b_lat * l_lat, GW) for o in (o_hy, o_ssd, o_lru, o_gdn)]
        x_lat = _ffn_call(x_lat, outs, mods[l], g_ffn[l], wo, wg, wu, wd, g_final, l_lat, True, last)

    return (x_ctx.reshape(b_ctx, l_ctx, D_MODEL), x_lat.reshape(b_lat, l_lat, D_MODEL),
            jnp.stack(lru_out, axis=1), jnp.stack(ssd_out, axis=1), jnp.stack(gdn_out, axis=1))
```

```python
import functools
import math

import ml_dtypes
import numpy as np
import jax
import jax.numpy as jnp
from jax import lax
from jax.experimental import pallas as pl
from jax.experimental.pallas import tpu as pltpu

F32 = jnp.float32
BF16 = jnp.bfloat16
HI = lax.Precision.HIGHEST

D_MODEL = 1024
DEPTH = 2
EPS = 1e-6
GRID_W = 64
POS_BASE = 10000.0
GW = 256
D_FF = 2816
HD = 64
NH = 4
HY_BANDS = 16
HY_EMB = 1 + 2 * HY_BANDS
HY_HIDDEN = 64
LRU_C = 8.0

COL_HY = 0
COL_SSD = 768
COL_LRU = 1536
COL_GDN = 2048
COL_SMALL = 3072
IN_PACK = 3200
SM_DT, SM_B, SM_A = 0, 8, 16

SSD_Q = 128
GDN_C = 128
ROW_TILE = 512
V7X_VMEM_LIMIT = 56 * 1024 * 1024


def _cparams(n_axes=1):
    return pltpu.CompilerParams(dimension_semantics=("arbitrary",) * n_axes,
                                vmem_limit_bytes=V7X_VMEM_LIMIT)


def _silu(x):
    return x * jax.nn.sigmoid(x)


def _softplus(x):
    return jnp.maximum(x, 0.0) + jnp.log1p(jnp.exp(-jnp.abs(x)))


def _gelu_tanh(x):
    return 0.5 * x * (1.0 + jnp.tanh(math.sqrt(2.0 / math.pi) * (x + 0.044715 * (x * x * x))))


def _split_bf16(x):
    hi = x.astype(BF16)
    lo = (x - hi.astype(F32)).astype(BF16)
    return hi, lo


def _dot(a, b):
    return jnp.dot(a, b, preferred_element_type=F32)


def _dot_hi(a, b):
    return jnp.dot(a, b, precision=HI, preferred_element_type=F32)


def _cumsum_rows(tri_b, x):
    x1 = x.astype(BF16)
    r = x - x1.astype(F32)
    x2 = r.astype(BF16)
    x3 = (r - x2.astype(F32)).astype(BF16)
    return _dot(tri_b, x1) + _dot(tri_b, x2) + _dot(tri_b, x3)


def _cumsum_cols(x, tri_b):
    x1 = x.astype(BF16)
    r = x - x1.astype(F32)
    x2 = r.astype(BF16)
    x3 = (r - x2.astype(F32)).astype(BF16)
    return _dot(x1, tri_b) + _dot(x2, tri_b) + _dot(x3, tri_b)


def _dot3_const(ah, al, x):
    xh, xl = _split_bf16(x)
    return _dot(ah, xh) + _dot(ah, xl) + _dot(al, xh)


def _seg_sum(x, ones_bd):
    xh, xl = _split_bf16(x)
    return _dot(xh, ones_bd) + _dot(xl, ones_bd)


def _lane_iota(shape):
    return lax.broadcasted_iota(jnp.int32, shape, len(shape) - 1)


def _head_lanes(cols, rows):
    lane = _lane_iota((rows, GW))
    out = jnp.broadcast_to(cols[NH - 1], (rows, GW))
    for h in range(NH - 2, -1, -1):
        out = jnp.where(lane < (h + 1) * HD, cols[h], out)
    return out


def _dwconv(x, w_ref, pad_ref, n_rows, taps):
    ch = x.shape[1]
    left = taps // 2
    pad_ref[0:8, 0:ch] = jnp.zeros((8, ch), F32)
    pad_ref[8 + n_rows:16 + n_rows, 0:ch] = jnp.zeros((8, ch), F32)
    pad_ref[8:8 + n_rows, 0:ch] = x
    acc = None
    for j in range(taps):
        off = 8 - left + j
        term = pad_ref[off:off + n_rows, 0:ch] * w_ref[j:j + 1, :]
        acc = term if acc is None else acc + term
    return acc


def _mod_kernel(c_ref, w_ref, b_ref, o_ref):
    s = _silu(c_ref[...])
    o_ref[...] = _dot_hi(s, w_ref[...]) + b_ref[...]


def _mod_call(cvec, w_mod, b_mod):
    tn = 1536
    n = w_mod.shape[-1]
    return pl.pallas_call(
        _mod_kernel,
        grid=(DEPTH, n // tn),
        in_specs=[pl.BlockSpec((16, D_MODEL), lambda l, j: (0, 0)),
                  pl.BlockSpec((None, D_MODEL, tn), lambda l, j: (l, 0, j)),
                  pl.BlockSpec((None, 1, tn), lambda l, j: (l, 0, j))],
        out_specs=pl.BlockSpec((None, 16, tn), lambda l, j: (l, 0, j)),
        out_shape=jax.ShapeDtypeStruct((DEPTH, 16, n), F32),
        compiler_params=_cparams(2),
        name="adaln_mod",
    )(cvec, w_mod, b_mod.reshape(DEPTH, 1, n))


def _inproj_kernel(*refs, add_pos):
    if add_pos:
        x_ref, pos_ref, mod_ref, g_ref, w_ref, o_ref, x0_ref = refs
        x = x_ref[...] + pos_ref[...]
        x0_ref[...] = x
    else:
        x_ref, mod_ref, g_ref, w_ref, o_ref = refs
        x = x_ref[...]
    mod = mod_ref[...]
    sh = mod[:, 0:D_MODEL]
    sc = mod[:, D_MODEL:2 * D_MODEL]
    h = x * lax.rsqrt(jnp.mean(x * x, axis=-1, keepdims=True) + EPS) * g_ref[...]
    h = h * (1.0 + sc) + sh
    o_ref[...] = _dot(h.astype(BF16), w_ref[...])


def _inproj_call(x2d, pos, mods_l, g, w_pack, seq_len, is_lat):
    n = x2d.shape[0]
    tm = ROW_TILE
    per_seq = max(seq_len // tm, 1)
    mod_map = (lambda t: (1 + t // per_seq, 0, 0)) if is_lat else (lambda t: (0, 0, 0))
    in_specs = [pl.BlockSpec((tm, D_MODEL), lambda t: (t, 0))]
    args = [x2d]
    if pos is not None:
        in_specs.append(pl.BlockSpec((tm, D_MODEL), lambda t: (t % per_seq, 0)))
        args.append(pos)
    in_specs += [pl.BlockSpec((None, 1, 6 * D_MODEL), mod_map),
                 pl.BlockSpec((1, D_MODEL), lambda t: (0, 0)),
                 pl.BlockSpec((D_MODEL, IN_PACK), lambda t: (0, 0))]
    args += [mods_l, g.reshape(1, D_MODEL), w_pack]
    out_specs = pl.BlockSpec((tm, IN_PACK), lambda t: (t, 0))
    out_shape = jax.ShapeDtypeStruct((n, IN_PACK), F32)
    if pos is not None:
        out_specs = [out_specs, pl.BlockSpec((tm, D_MODEL), lambda t: (t, 0))]
        out_shape = [out_shape, jax.ShapeDtypeStruct((n, D_MODEL), F32)]
    return pl.pallas_call(
        functools.partial(_inproj_kernel, add_pos=pos is not None),
        grid=(n // tm,),
        in_specs=in_specs,
        out_specs=out_specs,
        out_shape=out_shape,
        compiler_params=_cparams(1),
        name="inproj_lat0" if pos is not None else "inproj",
    )(*args)


@functools.lru_cache(maxsize=None)
def _dft_tables(seq_len):
    n = 2 * seq_len - 1
    idx = np.arange(seq_len, dtype=np.int64)
    ang = (2.0 * np.pi / n) * ((idx[:, None] * idx[None, :]) % n).astype(np.float64)
    out = []
    for tab in (np.cos(ang), np.sin(ang)):
        hi = tab.astype(ml_dtypes.bfloat16)
        lo = (tab - hi.astype(np.float64)).astype(ml_dtypes.bfloat16)
        out += [hi, lo]
    return tuple(out)


@functools.lru_cache(maxsize=None)
def _hyena_consts(seq_len):
    t = np.linspace(0.0, 1.0, seq_len, dtype=np.float32)[:, None]
    w = np.float32(2.0 * math.pi / seq_len) * np.arange(seq_len, dtype=np.float32)[:, None]
    bands = np.linspace(1e-4, HY_BANDS - 1, HY_BANDS, dtype=np.float32)[None, :]
    feats = np.concatenate([t, np.cos(bands * w), -np.sin(bands * w)], axis=-1).astype(np.float32)
    feats_p = np.zeros((seq_len, 128), np.float32)
    feats_p[:, :HY_EMB] = feats
    max_decay = math.log(1e-2) / 0.3
    min_decay = math.log(1e-2) / 1.5
    deltas = np.abs(np.linspace(min_decay, max_decay, GW, dtype=np.float32))
    decay = np.exp(-t * deltas).astype(np.float32)
    return feats_p, np.concatenate([decay, decay], axis=1)


def _hyfilt_kernel(feat_ref, dec_ref, w1_ref, b1_ref, w2_ref, b2_ref, w3_ref, fr_ref,
                   ch_ref, cl_ref, sh_ref, sl_ref, hr_ref, hi_ref, *, seq_len):
    fr = fr_ref[...]
    h = jnp.sin(fr[0:1, :] * (_dot_hi(feat_ref[...], w1_ref[...]) + b1_ref[...]))
    h = jnp.sin(fr[1:2, :] * (_dot_hi(h, w2_ref[...]) + b2_ref[...]))
    h = _dot_hi(h, w3_ref[...]) * dec_ref[...]
    den = jnp.sum(jnp.abs(h[:, :GW]) + jnp.abs(h[:, GW:]), axis=0, keepdims=True)
    h = h / jnp.concatenate([den, den], axis=1)
    hc = _dot3_const(ch_ref[...], cl_ref[...], h)
    hs = _dot3_const(sh_ref[...], sl_ref[...], h)
    n = 2 * seq_len - 1
    row = lax.broadcasted_iota(jnp.int32, (seq_len, GW), 0)
    wgt = jnp.where(row == 0, 1.0 / n, 2.0 / n)
    hr_ref[...] = (hc[:, :GW] + hc[:, GW:]) * wgt
    hi_ref[...] = (hs[:, GW:] - hs[:, :GW]) * wgt


def _hyfilt_call(seq_len, hw):
    feats, decay = _hyena_consts(seq_len)
    tabs = _dft_tables(seq_len)
    full = lambda shape: pl.BlockSpec(shape, lambda l: (0,) * len(shape))
    lay = lambda shape: pl.BlockSpec((None,) + shape, lambda l: (l,) + (0,) * len(shape))
    return pl.pallas_call(
        functools.partial(_hyfilt_kernel, seq_len=seq_len),
        grid=(DEPTH,),
        in_specs=[full((seq_len, 128)), full((seq_len, 2 * GW)),
                  lay((128, 128)), lay((1, 128)), lay((128, 128)), lay((1, 128)), lay((128, 2 * GW)),
                  lay((2, 128))] + [full((seq_len, seq_len))] * 4,
        out_specs=[lay((seq_len, GW)), lay((seq_len, GW))],
        out_shape=[jax.ShapeDtypeStruct((DEPTH, seq_len, GW), F32)] * 2,
        compiler_params=_cparams(1),
        name=f"hyena_filter_{seq_len}",
    )(feats, decay, hw["w1"], hw["b1"], hw["w2"], hw["b2"], hw["w3"], hw["freq"], *tabs)


def _hyena_kernel(u_ref, cw_ref, bias_ref, hr_ref, hi_ref, ch_ref, cl_ref, sh_ref, sl_ref,
                  o_ref, pad_ref, *, seq_len, n_seq):
    for b in range(n_seq):
        uc = _dwconv(u_ref[b], cw_ref, pad_ref, seq_len, 3)
        x0 = uc[:, 0:GW]
        z = uc[:, 2 * GW:3 * GW] * uc[:, GW:2 * GW]
        za = _dot3_const(ch_ref[...], cl_ref[...], z)
        zb = _dot3_const(sh_ref[...], sl_ref[...], z)
        hr = hr_ref[...]
        hi = hi_ref[...]
        yr = za * hr + zb * hi
        yi = za * hi - zb * hr
        y = _dot3_const(ch_ref[...], cl_ref[...], yr) - _dot3_const(sh_ref[...], sl_ref[...], yi)
        o_ref[b] = x0 * (y + z * bias_ref[...])


def _hyena_call(proj3, conv_w, bias, hr, hi, seq_len, n_seq):
    bsz = proj3.shape[0]
    tabs = _dft_tables(seq_len)
    full = lambda shape: pl.BlockSpec(shape, lambda b: (0,) * len(shape))
    return pl.pallas_call(
        functools.partial(_hyena_kernel, seq_len=seq_len, n_seq=n_seq),
        grid=(bsz // n_seq,),
        in_specs=[pl.BlockSpec((n_seq, seq_len, 3 * GW), lambda b: (b, 0, COL_HY // (3 * GW))),
                  full((3, 3 * GW)), full((1, GW)), full((seq_len, GW)), full((seq_len, GW))]
                 + [full((seq_len, seq_len))] * 4,
        out_specs=pl.BlockSpec((n_seq, seq_len, GW), lambda b: (b, 0, 0)),
        out_shape=jax.ShapeDtypeStruct((bsz, seq_len, GW), F32),
        scratch_shapes=[pltpu.VMEM((seq_len + 16, 3 * GW), F32)],
        compiler_params=_cparams(1),
        name=f"hyena_{seq_len}",
    )(proj3, conv_w, bias.reshape(1, GW), hr, hi, *tabs)


def _tri_masks(q):
    row = lax.broadcasted_iota(jnp.int32, (q, q), 0)
    col = lax.broadcasted_iota(jnp.int32, (q, q), 1)
    return col <= row, col >= row


def _ssd_kernel(*refs, seq_len, has_s0, emit_state):
    it = iter(refs)
    zx_ref, sm_ref, conv_ref, dtb_r_ref, dtb_c_ref, an_r_ref, an_c_ref, dvec_ref, gn_ref = (
        next(it) for _ in range(9))
    s0_ref = next(it) if has_s0 else None
    o_ref = next(it)
    so_ref = next(it) if emit_state else None
    pad_ref, xs_ref, bm_ref, cm_ref, dtc_ref, ac_ref, ar_ref, y_ref, st_ref = it
    q = SSD_Q
    nc = seq_len // q

    xbc = _silu(_dwconv(zx_ref[:, GW:3 * GW], conv_ref, pad_ref, seq_len, 4))
    xs_ref[...] = xbc[:, 0:GW]
    bm_ref[...] = xbc[:, GW:GW + 128]
    cm_ref[...] = xbc[:, GW + 128:GW + 256]
    y_ref[...] = xbc[:, 0:GW] * dvec_ref[...]
    sm = sm_ref[...]
    dtc = _softplus(sm + dtb_r_ref[...])
    dtc_ref[...] = dtc
    ac_ref[...] = dtc * an_r_ref[...]
    sm_t = sm.T
    ar = _softplus(sm_t[SM_DT:SM_DT + 8, :] + dtb_c_ref[...]) * an_c_ref[...]
    for c in range(nc):
        ar_ref[c] = ar[:, c * q:(c + 1) * q]
    if has_s0:
        st_ref[...] = s0_ref[...]
    else:
        st_ref[...] = jnp.zeros(st_ref.shape, F32)

    lower, upper = _tri_masks(q)
    lo_f = lower.astype(F32).astype(BF16)
    up_f = upper.astype(F32).astype(BF16)
    lane128 = _lane_iota((q, 128))
    lane256 = _lane_iota((q, GW))
    st_row = lax.broadcasted_iota(jnp.int32, (128, GW), 0)
    st_col = lax.broadcasted_iota(jnp.int32, (128, GW), 1)
    st_mask = (st_row // HD) == (st_col // 128)

    def chunk_dir(c, d):
        rows = pl.ds(pl.multiple_of(c * q, q), q)
        xs_c = xs_ref[rows, :]
        b_c = bm_ref[rows, :]
        c_c = cm_ref[rows, :]
        dt_c = dtc_ref[rows, :]
        a_c = ac_ref[rows, :]
        a_r = ar_ref[c]
        if d == 0:
            cum_c = _cumsum_rows(lo_f, a_c)
            cum_r = _cumsum_cols(a_r, up_f)
            mask, last = lower, q - 1
        else:
            cum_c = _cumsum_rows(up_f, a_c)
            cum_r = _cumsum_cols(a_r, lo_f)
            mask, last = upper, 0
        b_t = b_c.T.astype(BF16)
        c_g = jnp.concatenate([jnp.where(lane128 < HD, c_c, 0.0),
                               jnp.where(lane128 >= HD, c_c, 0.0)], axis=0).astype(BF16)
        g_mat = _dot(c_g, b_t)
        w_parts = []
        for h in range(NH):
            j = d * NH + h
            g = h // 2
            diff = cum_c[:, j:j + 1] - cum_r[j:j + 1, :]
            m = jnp.where(mask, jnp.exp(jnp.minimum(diff, 0.0)), 0.0)
            w_parts.append((g_mat[g * q:(g + 1) * q, :] * m).astype(BF16))
        w_cat = jnp.concatenate(w_parts, axis=1)
        cols = [cum_c[:, d * NH + h:d * NH + h + 1] for h in range(NH)]
        lasts = [cum_c[last:last + 1, d * NH + h:d * NH + h + 1] for h in range(NH)]
        xdt = xs_c * _head_lanes([dt_c[:, d * NH + h:d * NH + h + 1] for h in range(NH)], q)
        xdt_b = xdt.astype(BF16)
        stack = jnp.concatenate(
            [jnp.where((lane256 // HD) == h, xdt_b, jnp.zeros_like(xdt_b)) for h in range(NH)], axis=0)
        y_diag = _dot(w_cat, stack)
        ss = st_ref[d]
        y_off = _dot(c_c.astype(BF16), ss.astype(BF16)) * _head_lanes([jnp.exp(cc) for cc in cols], q)
        y_ref[rows, :] += y_diag + y_off
        d_in = _head_lanes([jnp.exp(lasts[h] - cols[h]) for h in range(NH)], q)
        upd = _dot(b_t, (xdt * d_in).astype(BF16))
        dec = _head_lanes([jnp.exp(lv) for lv in lasts], 1)
        st_ref[d] = ss * dec + jnp.where(st_mask, upd, 0.0)

    def body(i, carry):
        chunk_dir(i, 0)
        chunk_dir(nc - 1 - i, 1)
        return carry

    lax.fori_loop(0, nc, body, 0)

    y = y_ref[...] * _silu(zx_ref[:, 0:GW])
    y = y * lax.rsqrt(jnp.mean(y * y, axis=-1, keepdims=True) + EPS) * gn_ref[...]
    o_ref[...] = y
    if emit_state:
        so_ref[...] = st_ref[...]


def _ssd_call(proj3, p, s0, seq_len, emit_state):
    bsz = proj3.shape[0]
    nc = seq_len // SSD_Q
    full = lambda shape: pl.BlockSpec(shape, lambda b: (0,) * len(shape))
    in_specs = [pl.BlockSpec((None, seq_len, 3 * GW), lambda b: (b, 0, COL_SSD // (3 * GW))),
                pl.BlockSpec((None, seq_len, 128), lambda b: (b, 0, COL_SMALL // 128)),
                full((4, 2 * GW)), full((1, 128)), full((8, 1)), full((1, 128)), full((8, 1)),
                full((1, GW)), full((1, GW))]
    args = [proj3, proj3, p["conv"], p["dtb_row"], p["dtb_col"], p["an_row"], p["an_col"], p["dvec"], p["gn"]]
    if s0 is not None:
        in_specs.append(pl.BlockSpec((None, 2, 128, GW), lambda b: (b, 0, 0, 0)))
        args.append(s0)
    out_specs = [pl.BlockSpec((None, seq_len, GW), lambda b: (b, 0, 0))]
    out_shape = [jax.ShapeDtypeStruct((bsz, seq_len, GW), F32)]
    if emit_state:
        out_specs.append(pl.BlockSpec((None, 2, 128, GW), lambda b: (b, 0, 0, 0)))
        out_shape.append(jax.ShapeDtypeStruct((bsz, 2, 128, GW), F32))
    scratch = [pltpu.VMEM((seq_len + 16, 2 * GW), F32),
               pltpu.VMEM((seq_len, GW), F32),
               pltpu.VMEM((seq_len, 128), F32),
               pltpu.VMEM((seq_len, 128), F32),
               pltpu.VMEM((seq_len, 128), F32),
               pltpu.VMEM((seq_len, 128), F32),
               pltpu.VMEM((nc, 8, SSD_Q), F32),
               pltpu.VMEM((seq_len, GW), F32),
               pltpu.VMEM((2, 128, GW), F32)]
    return pl.pallas_call(
        functools.partial(_ssd_kernel, seq_len=seq_len, has_s0=s0 is not None, emit_state=emit_state),
        grid=(bsz,),
        in_specs=in_specs,
        out_specs=out_specs,
        out_shape=out_shape,
        scratch_shapes=scratch,
        compiler_params=_cparams(1),
        name=f"ssd_{seq_len}",
    )(*args)


def _lru_kernel(*refs, seq_len, has_s0, emit_state):
    it = iter(refs)
    u_ref, conv_ref, w_ref, bias_ref, sp_ref = (next(it) for _ in range(5))
    s0_ref = next(it) if has_s0 else None
    o_ref = next(it)
    so_ref = next(it) if emit_state else None
    pad_ref, af_ref, bf_ref, ab_ref, bb_ref = it
    n = seq_len

    xc = _dwconv(u_ref[:, 0:GW], conv_ref, pad_ref, n, 4)
    ri = jax.nn.sigmoid(_dot(xc.astype(BF16), w_ref[...]) + bias_ref[...])
    log_a = -LRU_C * ri[:, 0:2 * GW] * sp_ref[...]
    a = jnp.exp(log_a)
    bt = jnp.sqrt(jnp.maximum(1.0 - a * a, 0.0)) * ri[:, 2 * GW:4 * GW] * jnp.concatenate([xc, xc], axis=1)

    ones8 = jnp.ones((8, GW), F32)
    zeros8 = jnp.zeros((8, GW), F32)
    for a_ref_, b_ref_, lo in ((af_ref, bf_ref, 0), (ab_ref, bb_ref, GW)):
        a_ref_[0:8, :] = ones8
        a_ref_[8 + n:16 + n, :] = ones8
        b_ref_[0:8, :] = zeros8
        b_ref_[8 + n:16 + n, :] = zeros8
        a_ref_[8:8 + n, :] = a[:, lo:lo + GW]
        b_ref_[8:8 + n, :] = bt[:, lo:lo + GW]
    if has_s0:
        s0 = s0_ref[...]
        bf_ref[8:9, :] = bf_ref[8:9, :] + af_ref[8:9, :] * s0[0:1, :]
        bb_ref[7 + n:8 + n, :] = bb_ref[7 + n:8 + n, :] + ab_ref[7 + n:8 + n, :] * s0[1:2, :]

    d = 1
    while d < n:
        if d < 8:
            a_cur = af_ref[8:8 + n, :]
            af_ref[8:8 + n, :] = a_cur * af_ref[8 - d:8 - d + n, :]
            bf_ref[8:8 + n, :] = a_cur * bf_ref[8 - d:8 - d + n, :] + bf_ref[8:8 + n, :]
            a_cur = ab_ref[8:8 + n, :]
            ab_ref[8:8 + n, :] = a_cur * ab_ref[8 + d:8 + d + n, :]
            bb_ref[8:8 + n, :] = a_cur * bb_ref[8 + d:8 + d + n, :] + bb_ref[8:8 + n, :]
        else:
            m = n - d
            a_cur = af_ref[8 + d:8 + n, :]
            a_new = a_cur * af_ref[8:8 + m, :]
            b_new = a_cur * bf_ref[8:8 + m, :] + bf_ref[8 + d:8 + n, :]
            af_ref[8 + d:8 + n, :] = a_new
            bf_ref[8 + d:8 + n, :] = b_new
            a_cur = ab_ref[8:8 + m, :]
            a_new = a_cur * ab_ref[8 + d:8 + n, :]
            b_new = a_cur * bb_ref[8 + d:8 + n, :] + bb_ref[8:8 + m, :]
            ab_ref[8:8 + m, :] = a_new
            bb_ref[8:8 + m, :] = b_new
        d *= 2

    y = (bf_ref[8:8 + n, :] + bb_ref[8:8 + n, :]) * _gelu_tanh(u_ref[:, GW:2 * GW])
    o_ref[...] = y
    if emit_state:
        so_ref[0:1, :] = bf_ref[7 + n:8 + n, :]
        so_ref[1:2, :] = bb_ref[8:9, :]


def _lru_call(proj3, p, s0, seq_len, emit_state):
    bsz = proj3.shape[0]
    full = lambda shape: pl.BlockSpec(shape, lambda b: (0,) * len(shape))
    in_specs = [pl.BlockSpec((None, seq_len, 2 * GW), lambda b: (b, 0, COL_LRU // (2 * GW))),
                full((4, GW)), full((GW, 4 * GW)), full((1, 4 * GW)), full((1, 2 * GW))]
    args = [proj3, p["conv"], p["w"], p["bias"], p["sp"]]
    if s0 is not None:
        in_specs.append(pl.BlockSpec((None, 2, GW), lambda b: (b, 0, 0)))
        args.append(s0)
    out_specs = [pl.BlockSpec((None, seq_len, GW), lambda b: (b, 0, 0))]
    out_shape = [jax.ShapeDtypeStruct((bsz, seq_len, GW), F32)]
    if emit_state:
        out_specs.append(pl.BlockSpec((None, 2, GW), lambda b: (b, 0, 0)))
        out_shape.append(jax.ShapeDtypeStruct((bsz, 2, GW), F32))
    scratch = [pltpu.VMEM((seq_len + 16, GW), F32)] * 5
    return pl.pallas_call(
        functools.partial(_lru_kernel, seq_len=seq_len, has_s0=s0 is not None, emit_state=emit_state),
        grid=(bsz,),
        in_specs=in_specs,
        out_specs=out_specs,
        out_shape=out_shape,
        scratch_shapes=scratch,
        compiler_params=_cparams(1),
        name=f"rglru_{seq_len}",
    )(*args)


TRI_BASE = 8


def _unit_tri_inverse(a_mats, eye, same_blk):
    c = a_mats[0].shape[0]
    ds = [jnp.where(same_blk[TRI_BASE], a, 0.0) for a in a_mats]
    d_bs = [d.astype(BF16) for d in ds]
    ps = [eye - d for d in ds]
    x_bs = [_dot(d_b, d_b).astype(BF16) for d_b in d_bs]
    rs = [_dot(x_b, jnp.concatenate([p.astype(BF16), x_b], axis=1)) for x_b, p in zip(x_bs, ps)]
    ps = [p + r[:, 0:c] for p, r in zip(ps, rs)]
    ts = [p + _dot(r[:, c:2 * c].astype(BF16), p.astype(BF16)) for p, r in zip(ps, rs)]
    size = 2 * TRI_BASE
    while size <= c:
        off = same_blk[size] & jnp.logical_not(same_blk[size // 2])
        e_bs = [jnp.where(off, a, 0.0).astype(BF16) for a in a_mats]
        t_bs = [t.astype(BF16) for t in ts]
        us = [_dot(t_b, e_b).astype(BF16) for t_b, e_b in zip(t_bs, e_bs)]
        ts = [t - _dot(u, t_b) for t, u, t_b in zip(ts, us, t_bs)]
        size *= 2
    return ts


def _gdn_kernel(*refs, seq_len, has_s0, emit_state):
    it = iter(refs)
    u_ref, sm_ref, conv_ref, dtb_r_ref, dtb_c_ref, an_r_ref, an_c_ref, gn_ref, ones_ref = (
        next(it) for _ in range(9))
    s0_ref = next(it) if has_s0 else None
    o_ref = next(it)
    so_ref = next(it) if emit_state else None
    pad_ref, q_ref, k_ref, v_ref, be_ref, gc_ref, gr_ref, oa_ref, st_ref = it
    c = GDN_C
    nc = seq_len // c
    ones_bd = ones_ref[...]

    qkv = _silu(_dwconv(u_ref[:, 0:3 * GW], conv_ref, pad_ref, seq_len, 4))
    qq = qkv[:, 0:GW]
    kk = qkv[:, GW:2 * GW]
    q_ref[...] = qq * lax.rsqrt(_seg_sum(qq * qq, ones_bd) + EPS) * (HD ** -0.5)
    k_ref[...] = kk * lax.rsqrt(_seg_sum(kk * kk, ones_bd) + EPS)
    v_ref[...] = qkv[:, 2 * GW:3 * GW]
    sm = sm_ref[...]
    be_ref[...] = jax.nn.sigmoid(sm)
    gc_ref[...] = an_r_ref[...] * _softplus(sm + dtb_r_ref[...])
    sm_t = sm.T
    g_r = an_c_ref[...] * _softplus(sm_t[SM_A:SM_A + 8, :] + dtb_c_ref[...])
    for i in range(nc):
        gr_ref[i] = g_r[:, i * c:(i + 1) * c]
    oa_ref[...] = jnp.zeros(oa_ref.shape, F32)
    if has_s0:
        st_ref[...] = s0_ref[...]
    else:
        st_ref[...] = jnp.zeros(st_ref.shape, F32)

    lower, upper = _tri_masks(c)
    lo_f = lower.astype(F32).astype(BF16)
    up_f = upper.astype(F32).astype(BF16)
    row_i =lax.broadcasted_iota(jnp.int32, (c, c), 0)
    col_i = lax.broadcasted_iota(jnp.int32, (c, c), 1)
    eye = (row_i == col_i).astype(F32)
    same_blk = {}
    size = TRI_BASE
    while size <= c:
        same_blk[size] = (row_i // size) == (col_i // size)
        size *= 2
    lane256 = _lane_iota((c, GW))
    lane512 = _lane_iota((c, 2 * GW))
    bd_row = lax.broadcasted_iota(jnp.int32, (GW, GW), 0)
    bd_col = lax.broadcasted_iota(jnp.int32, (GW, GW), 1)
    bd_mask = (bd_row // HD) == (bd_col // HD)

    def stack_heads(x_b, lane):
        zero = jnp.zeros_like(x_b)
        return jnp.concatenate([jnp.where(((lane % GW) // HD) == h, x_b, zero) for h in range(NH)], axis=0)

    def body(i, carry):
        dirs = (0, 1)
        cis = (i, nc - 1 - i)
        rows = [pl.ds(pl.multiple_of(ci * c, c), c) for ci in cis]
        q_c = [q_ref[r, :] for r in rows]
        k_c = [k_ref[r, :] for r in rows]
        v_c = [v_ref[r, :] for r in rows]
        be = [be_ref[r, :] for r in rows]
        tri_c = (lo_f, up_f)
        tri_r = (up_f, lo_f)
        incl = (lower, upper)
        strict = (col_i < row_i, col_i > row_i)
        last = (c - 1, 0)
        cum_c = [_cumsum_rows(tri_c[d], gc_ref[rows[d], :]) for d in dirs]
        cum_r = [_cumsum_cols(gr_ref[cis[d]], tri_r[d]) for d in dirs]
        kq = []
        for d in dirs:
            k_b = k_c[d].astype(BF16)
            q_b = q_c[d].astype(BF16)
            kq.append(_dot(jnp.concatenate([stack_heads(k_b, lane256), stack_heads(q_b, lane256)], axis=0),
                           k_c[d].T.astype(BF16)))
        a_mats, attn = [], []
        for d in dirs:
            for h in range(NH):
                j = d * NH + h
                diff = cum_c[d][:, SM_A + j:SM_A + j + 1] - cum_r[d][j:j + 1, :]
                dec = jnp.where(incl[d], jnp.exp(jnp.minimum(diff, 0.0)), 0.0)
                a_mats.append(jnp.where(
                    strict[d], kq[d][h * c:(h + 1) * c, :] * be[d][:, SM_B + j:SM_B + j + 1] * dec, 0.0))
                attn.append((kq[d][(NH + h) * c:(NH + h + 1) * c, :] * dec).astype(BF16))
        t_mats = _unit_tri_inverse(a_mats, eye, same_blk)
        t_cat = [jnp.concatenate([t.astype(BF16) for t in t_mats[d * NH:(d + 1) * NH]], axis=1) for d in dirs]
        a_split = [_split_bf16(a) for a in a_mats]
        ah_cat = [jnp.concatenate([a_split[d * NH + h][0] for h in range(NH)], axis=1) for d in dirs]
        al_cat = [jnp.concatenate([a_split[d * NH + h][1] for h in range(NH)], axis=1) for d in dirs]
        cols = [[cum_c[d][:, SM_A + d * NH + h:SM_A + d * NH + h + 1] for h in range(NH)] for d in dirs]
        lasts = [[cum_c[d][last[d]:last[d] + 1, SM_A + d * NH + h:SM_A + d * NH + h + 1] for h in range(NH)]
                 for d in dirs]
        b_l = [_head_lanes([be[d][:, SM_B + d * NH + h:SM_B + d * NH + h + 1] for h in range(NH)], c) for d in dirs]
        e_l = [_head_lanes([jnp.exp(cc) for cc in cols[d]], c) for d in dirs]
        rhs = [jnp.concatenate([v_c[d] * b_l[d], k_c[d] * b_l[d] * e_l[d]], axis=1) for d in dirs]
        sol = [_dot(t_cat[d], stack_heads(rhs[d].astype(BF16), lane512)) for d in dirs]
        s_split = [_split_bf16(s) for s in sol]
        st_hi = [stack_heads(s_split[d][0], lane512) for d in dirs]
        st_lo = [stack_heads(s_split[d][1], lane512) for d in dirs]
        a_sol = [_dot(ah_cat[d], st_hi[d]) + _dot(ah_cat[d], st_lo[d]) + _dot(al_cat[d], st_hi[d]) for d in dirs]
        uw = [sol[d] + _dot(t_cat[d], stack_heads((rhs[d] - sol[d] - a_sol[d]).astype(BF16), lane512))
              for d in dirs]
        ss = [st_ref[d] for d in dirs]
        r2 = [_dot(jnp.concatenate([uw[d][:, GW:2 * GW], q_c[d] * e_l[d]], axis=0).astype(BF16),
                   ss[d].astype(BF16)) for d in dirs]
        v_new_b = [(uw[d][:, 0:GW] - r2[d][0:c, :]).astype(BF16) for d in dirs]
        o_c = [r2[d][c:2 * c, :] + _dot(jnp.concatenate(attn[d * NH:(d + 1) * NH], axis=1),
                                        stack_heads(v_new_b[d], lane256)) for d in dirs]
        for d in dirs:
            oa_ref[rows[d], :] += o_c[d]
        for d in dirs:
            k_dec = k_c[d] * _head_lanes([jnp.exp(lasts[d][h] - cols[d][h]) for h in range(NH)], c)
            upd = _dot(k_dec.T.astype(BF16), v_new_b[d])
            dec_row = _head_lanes([jnp.exp(lv) for lv in lasts[d]], 1)
            st_ref[d] = ss[d] * dec_row + jnp.where(bd_mask, upd, 0.0)
        return carry

    lax.fori_loop(0, nc, body, 0)

    o = oa_ref[...]
    o = o * lax.rsqrt(_seg_sum(o * o, ones_bd) * (1.0 / HD) + EPS) * gn_ref[...]
    o_ref[...] = o * _silu(u_ref[:, 3 * GW:4 * GW])
    if emit_state:
        so_ref[...] = st_ref[...]


def _gdn_call(proj3, p, s0, seq_len, emit_state):
    bsz = proj3.shape[0]
    nc = seq_len // GDN_C
    full = lambda shape: pl.BlockSpec(shape, lambda b: (0,) * len(shape))
    in_specs = [pl.BlockSpec((None, seq_len, 4 * GW), lambda b: (b, 0, COL_GDN // (4 * GW))),
                pl.BlockSpec((None, seq_len, 128), lambda b: (b, 0, COL_SMALL // 128)),
                full((4, 3 * GW)), full((1, 128)), full((8, 1)), full((1, 128)), full((8, 1)),
                full((1, GW)), full((GW, GW))]
    args = [proj3, proj3, p["conv"], p["dtb_row"], p["dtb_col"], p["an_row"], p["an_col"], p["gn"], p["ones"]]
    if s0 is not None:
        in_specs.append(pl.BlockSpec((None, 2, GW, GW), lambda b: (b, 0, 0, 0)))
        args.append(s0)
    out_specs = [pl.BlockSpec((None, seq_len, GW), lambda b: (b, 0, 0))]
    out_shape = [jax.ShapeDtypeStruct((bsz, seq_len, GW), F32)]
    if emit_state:
        out_specs.append(pl.BlockSpec((None, 2, GW, GW), lambda b: (b, 0, 0, 0)))
        out_shape.append(jax.ShapeDtypeStruct((bsz, 2, GW, GW), F32))
    scratch = [pltpu.VMEM((seq_len + 16, 3 * GW), F32),
               pltpu.VMEM((seq_len, GW), F32),
               pltpu.VMEM((seq_len, GW), F32),
               pltpu.VMEM((seq_len, GW), F32),
               pltpu.VMEM((seq_len, 128), F32),
               pltpu.VMEM((seq_len, 128), F32),
               pltpu.VMEM((nc, 8, GDN_C), F32),
               pltpu.VMEM((seq_len, GW), F32),
               pltpu.VMEM((2, GW, GW), F32)]
    return pl.pallas_call(
        functools.partial(_gdn_kernel, seq_len=seq_len, has_s0=s0 is not None, emit_state=emit_state),
        grid=(bsz,),
        in_specs=in_specs,
        out_specs=out_specs,
        out_shape=out_shape,
        scratch_shapes=scratch,
        compiler_params=_cparams(1),
        name=f"gdn_{seq_len}",
    )(*args)


def _ffn_kernel(x_ref, ohy_ref, ossd_ref, olru_ref, ogdn_ref, mod_ref, g_ref, wo_ref, wg_ref, wu_ref,
                wd_ref, gfin_ref, o_ref, *, final_norm):
    mod = mod_ref[...]
    ga_m = mod[:, 2 * D_MODEL:3 * D_MODEL]
    sh_f = mod[:, 3 * D_MODEL:4 * D_MODEL]
    sc_f = mod[:, 4 * D_MODEL:5 * D_MODEL]
    ga_f = mod[:, 5 * D_MODEL:6 * D_MODEL]
    mo = None
    for i, r in enumerate((ohy_ref, ossd_ref, olru_ref, ogdn_ref)):
        part = _dot(r[...].astype(BF16), wo_ref[i * GW:(i + 1) * GW, :])
        mo = part if mo is None else mo + part
    x = x_ref[...] + ga_m * mo
    h = x * lax.rsqrt(jnp.mean(x * x, axis=-1, keepdims=True) + EPS) * g_ref[...]
    h = (h * (1.0 + sc_f) + sh_f).astype(BF16)
    half = D_FF // 2
    ff = None
    for i in range(2):
        gate = _dot(h, wg_ref[:, i * half:(i + 1) * half])
        up = _dot(h, wu_ref[:, i * half:(i + 1) * half])
        part = _dot((_silu(gate) * up).astype(BF16), wd_ref[i * half:(i + 1) * half, :])
        ff = part if ff is None else ff + part
    x = x + ga_f * ff
    if final_norm:
        x = x * lax.rsqrt(jnp.mean(x * x, axis=-1, keepdims=True) + EPS) * gfin_ref[...]
    o_ref[...] = x


def _ffn_call(x2d, outs, mods_l, g, wo, wg, wu, wd, g_final, seq_len, is_lat, final_norm):
    n = x2d.shape[0]
    tm = ROW_TILE
    per_seq = seq_len // tm
    mod_map = (lambda t: (1 + t // per_seq, 0, 0)) if is_lat else (lambda t: (0, 0, 0))
    tile = lambda w: pl.BlockSpec((tm, w), lambda t: (t, 0))
    res = lambda shape: pl.BlockSpec(shape, lambda t: (0, 0), pipeline_mode=pl.Buffered(1))
    return pl.pallas_call(
        functools.partial(_ffn_kernel, final_norm=final_norm),
        grid=(n // tm,),
        in_specs=[tile(D_MODEL), tile(GW), tile(GW), tile(GW), tile(GW),
                  pl.BlockSpec((None, 1, 6 * D_MODEL), mod_map),
                  res((1, D_MODEL)), res((D_MODEL, D_MODEL)), res((D_MODEL, D_FF)), res((D_MODEL, D_FF)),
                  res((D_FF, D_MODEL)), res((1, D_MODEL))],
        out_specs=tile(D_MODEL),
        out_shape=jax.ShapeDtypeStruct((n, D_MODEL), F32),
        compiler_params=_cparams(1),
        name="outproj_ffn",
    )(x2d, *outs, mods_l, g.reshape(1, D_MODEL), wo, wg, wu, wd, g_final.reshape(1, D_MODEL))


def _grid_pos_embed(n_tokens):
    rows = n_tokens // GRID_W
    rr, cc = np.meshgrid(np.arange(rows, dtype=np.float32), np.arange(GRID_W, dtype=np.float32), indexing="ij")
    quarter = D_MODEL // 4
    omega = (1.0 / (np.float32(POS_BASE) ** (np.arange(quarter, dtype=np.float32) / quarter))).astype(np.float32)

    def enc(pos):
        ang = pos.reshape(-1)[:, None] * omega[None, :]
        return np.concatenate([np.sin(ang), np.cos(ang)], axis=-1)

    return jnp.asarray(np.concatenate([enc(rr), enc(cc)], axis=-1).astype(np.float32))


def _pad_small(vec8, offset):
    row = jnp.zeros((1, 128), F32).at[0, offset:offset + 8].set(vec8)
    return row, vec8.reshape(8, 1)


def _pack_w_in(w):
    hy_in, ssd_in, lru_in = 768, 776, 512
    o_ssd = hy_in
    o_lru = o_ssd + ssd_in
    o_gdn = o_lru + lru_in
    pieces = [w[:, 0:o_ssd + 768],
              w[:, o_lru:o_gdn + 1024],
              w[:, o_ssd + 768:o_ssd + 776],
              w[:, o_gdn + 1024:o_gdn + 1040],
              jnp.zeros((D_MODEL, IN_PACK - 3096), w.dtype)]
    return jnp.concatenate(pieces, axis=1).astype(BF16)


def _ssd_state_in(s):
    st = jnp.transpose(s, (0, 1, 4, 2, 3))
    grp = (jnp.arange(NH) // 2)[None, :] == jnp.arange(2)[:, None]
    full = st[:, :, None] * grp[None, None, :, None, :, None].astype(s.dtype)
    return full.reshape(s.shape[0], 2, 128, GW)


def _ssd_state_out(ss):
    bsz = ss.shape[0]
    blocks = ss.reshape(bsz, 2, 2, HD, NH, HD)
    pick = jnp.stack([blocks[:, :, h // 2, :, h, :] for h in range(NH)], axis=2)
    return jnp.transpose(pick, (0, 1, 2, 4, 3))


def _gdn_state_in(s):
    bsz = s.shape[0]
    eye = jnp.eye(NH, dtype=s.dtype)
    full = s[:, :, :, :, None, :] * eye[None, None, :, None, :, None]
    return full.reshape(bsz, 2, GW, GW)


def _gdn_state_out(ss):
    bsz = ss.shape[0]
    blocks = ss.reshape(bsz, 2, NH, HD, NH, HD)
    return jnp.stack([blocks[:, :, h, :, h, :] for h in range(NH)], axis=2)


def kernel(x_prompt, x_sample, state_lru, state_ssd, state_gdn, c, c_ctx, w_mod, b_mod, g_mix, g_ffn, g_final, w_in, w_out, hy_conv, hy_w1, hy_b1, hy_w2, hy_b2, hy_w3, hy_freq, hy_bias, ssd_conv, ssd_dt_bias, ssd_a_log, ssd_d, ssd_norm, lru_conv, lru_w_r, lru_b_r, lru_w_i, lru_b_i, lru_lambda, gdn_conv, gdn_dt_bias, gdn_a_log, gdn_norm, w_gate, w_up, w_down):
    b_ctx, l_ctx, _ = x_prompt.shape
    b_lat, l_lat, _ = x_sample.shape

    cvec = jnp.zeros((16, D_MODEL), F32).at[0].set(c_ctx).at[1:1 + b_lat].set(c)
    mods = _mod_call(cvec, w_mod, b_mod).reshape(DEPTH, 16, 1, 6 * D_MODEL)

    hw = {"w1": jnp.zeros((DEPTH, 128, 128), F32).at[:, :HY_EMB, :HY_HIDDEN].set(hy_w1),
          "b1": jnp.zeros((DEPTH, 1, 128), F32).at[:, 0, :HY_HIDDEN].set(hy_b1),
          "w2": jnp.zeros((DEPTH, 128, 128), F32).at[:, :HY_HIDDEN, :HY_HIDDEN].set(hy_w2),
          "b2": jnp.zeros((DEPTH, 1, 128), F32).at[:, 0, :HY_HIDDEN].set(hy_b2),
          "w3": jnp.zeros((DEPTH, 128, 2 * GW), F32).at[:, :HY_HIDDEN, :].set(hy_w3),
          "freq": jnp.zeros((DEPTH, 2, 128), F32).at[:, :, :HY_HIDDEN].set(hy_freq)}
    filt = {l_ctx: _hyfilt_call(l_ctx, hw), l_lat: _hyfilt_call(l_lat, hw)}

    ones_bd = jnp.asarray(np.kron(np.eye(NH, dtype=np.float32), np.ones((HD, HD), np.float32))).astype(BF16)
    pos = _grid_pos_embed(l_lat)

    x_ctx = x_prompt.reshape(b_ctx * l_ctx, D_MODEL)
    x_lat = x_sample.reshape(b_lat * l_lat, D_MODEL)
    lru_out, ssd_out, gdn_out = [], [], []
    for l in range(DEPTH):
        w_pack = _pack_w_in(w_in[l])
        wo, wg, wu, wd = (w.astype(BF16) for w in (w_out[l], w_gate[l], w_up[l], w_down[l]))
        ssd_dtb_r, ssd_dtb_c = _pad_small(ssd_dt_bias[l].reshape(8), SM_DT)
        ssd_an_r, ssd_an_c = _pad_small(-jnp.exp(ssd_a_log[l].reshape(8)), SM_DT)
        ssd_p = {"conv": ssd_conv[l], "dtb_row": ssd_dtb_r, "dtb_col": ssd_dtb_c, "an_row": ssd_an_r,
                 "an_col": ssd_an_c, "dvec": jnp.repeat(ssd_d[l], HD).reshape(1, GW),
                 "gn": ssd_norm[l].reshape(1, GW)}
        eye_h = jnp.eye(NH, dtype=F32)
        blockdiag = lambda w: (w[:, :, :, None, :] * eye_h[None, :, None, :, None]).reshape(2, GW, GW)
        w_r, w_i = blockdiag(lru_w_r[l]), blockdiag(lru_w_i[l])
        lru_p = {"conv": lru_conv[l],
                 "w": jnp.concatenate([w_r[0], w_r[1], w_i[0], w_i[1]], axis=1).astype(BF16),
                 "bias": jnp.concatenate([lru_b_r[l].reshape(1, 2 * GW), lru_b_i[l].reshape(1, 2 * GW)], axis=1),
                 "sp": jax.nn.softplus(-lru_lambda[l]).reshape(1, 2 * GW)}
        gdn_dtb_r, gdn_dtb_c = _pad_small(gdn_dt_bias[l].reshape(8), SM_A)
        gdn_an_r, gdn_an_c = _pad_small(-jnp.exp(gdn_a_log[l].reshape(8)), SM_A)
        gdn_p = {"conv": gdn_conv[l], "dtb_row": gdn_dtb_r, "dtb_col": gdn_dtb_c, "an_row": gdn_an_r,
                 "an_col": gdn_an_c, "gn": jnp.tile(gdn_norm[l], NH).reshape(1, GW), "ones": ones_bd}
        last = l == DEPTH - 1

        proj = _inproj_call(x_ctx, None, mods[l], g_mix[l], w_pack, l_ctx, False).reshape(b_ctx, l_ctx, IN_PACK)
        o_hy = _hyena_call(proj, hy_conv[l], hy_bias[l], filt[l_ctx][0][l], filt[l_ctx][1][l], l_ctx, 4)
        o_ssd, s_ssd = _ssd_call(proj, ssd_p, None, l_ctx, True)
        o_lru, s_lru = _lru_call(proj, lru_p, None, l_ctx, True)
        o_gdn, s_gdn = _gdn_call(proj, gdn_p, None, l_ctx, True)
        outs = [o.reshape(b_ctx * l_ctx, GW) for o in (o_hy, o_ssd, o_lru, o_gdn)]
        x_ctx = _ffn_call(x_ctx, outs, mods[l], g_ffn[l], wo, wg, wu, wd, g_final, l_ctx, False, last)
        lru_out.append(s_lru)
        ssd_out.append(_ssd_state_out(s_ssd))
        gdn_out.append(_gdn_state_out(s_gdn))

        if l == 0:
            proj, x_lat = _inproj_call(x_lat, pos, mods[l], g_mix[l], w_pack, l_lat, True)
        else:
            proj = _inproj_call(x_lat, None, mods[l], g_mix[l], w_pack, l_lat, True)
        proj = proj.reshape(b_lat, l_lat, IN_PACK)
        o_hy = _hyena_call(proj, hy_conv[l], hy_bias[l], filt[l_lat][0][l], filt[l_lat][1][l], l_lat, 1)
        o_ssd, = _ssd_call(proj, ssd_p, _ssd_state_in(state_ssd[:, l]), l_lat, False)
        o_lru, = _lru_call(proj, lru_p, state_lru[:, l], l_lat, False)
        o_gdn, = _gdn_call(proj, gdn_p, _gdn_state_in(state_gdn[:, l]), l_lat, False)
        outs = [o.reshape(b_lat * l_lat, GW) for o in (o_hy, o_ssd, o_lru, o_gdn)]
        x_lat = _ffn_call(x_lat, outs, mods[l], g_ffn[l], wo, wg, wu, wd, g_final, l_lat, True, last)

    return (x_ctx.reshape(b_ctx, l_ctx, D_MODEL), x_lat.reshape(b_lat, l_lat, D_MODEL),
            jnp.stack(lru_out, axis=1), jnp.stack(ssd_out, axis=1), jnp.stack(gdn_out, axis=1))
```

```python
import functools
import math

import ml_dtypes
import numpy as np
import jax
import jax.numpy as jnp
from jax import lax
from jax.experimental import pallas as pl
from jax.experimental.pallas import tpu as pltpu

F32 = jnp.float32
BF16 = jnp.bfloat16
HI = lax.Precision.HIGHEST

D_MODEL = 1024
DEPTH = 2
EPS = 1e-6
GRID_W = 64
POS_BASE = 10000.0
GW = 256
D_FF = 2816
HD = 64
NH = 4
HY_BANDS = 16
HY_EMB = 1 + 2 * HY_BANDS
HY_HIDDEN = 64
LRU_C = 8.0

COL_HY = 0
COL_SSD = 768
COL_LRU = 1536
COL_GDN = 2048
COL_SMALL = 3072
IN_PACK = 3200
SM_DT, SM_B, SM_A = 0, 8, 16

SSD_Q = 128
GDN_C = 128
ROW_TILE = 512
V7X_VMEM_LIMIT = 56 * 1024 * 1024


def _cparams(n_axes=1):
    return pltpu.CompilerParams(dimension_semantics=("arbitrary",) * n_axes,
                                vmem_limit_bytes=V7X_VMEM_LIMIT)


def _silu(x):
    return x * jax.nn.sigmoid(x)


def _softplus(x):
    return jnp.maximum(x, 0.0) + jnp.log1p(jnp.exp(-jnp.abs(x)))


def _gelu_tanh(x):
    return 0.5 * x * (1.0 + jnp.tanh(math.sqrt(2.0 / math.pi) * (x + 0.044715 * (x * x * x))))


def _split_bf16(x):
    hi = x.astype(BF16)
    lo = (x - hi.astype(F32)).astype(BF16)
    return hi, lo


def _dot(a, b):
    return jnp.dot(a, b, preferred_element_type=F32)


def _dot_hi(a, b):
    return jnp.dot(a, b, precision=HI, preferred_element_type=F32)


def _cumsum_rows(tri_b, x):
    x1 = x.astype(BF16)
    r = x - x1.astype(F32)
    x2 = r.astype(BF16)
    x3 = (r - x2.astype(F32)).astype(BF16)
    return _dot(tri_b, x1) + _dot(tri_b, x2) + _dot(tri_b, x3)


def _cumsum_cols(x, tri_b):
    x1 = x.astype(BF16)
    r = x - x1.astype(F32)
    x2 = r.astype(BF16)
    x3 = (r - x2.astype(F32)).astype(BF16)
    return _dot(x1, tri_b) + _dot(x2, tri_b) + _dot(x3, tri_b)


def _dot3_const(ah, al, x):
    xh, xl = _split_bf16(x)
    return _dot(ah, xh) + _dot(ah, xl) + _dot(al, xh)


def _seg_sum(x, ones_bd):
    xh, xl = _split_bf16(x)
    return _dot(xh, ones_bd) + _dot(xl, ones_bd)


def _lane_iota(shape):
    return lax.broadcasted_iota(jnp.int32, shape, len(shape) - 1)


def _head_lanes(cols, rows):
    lane = _lane_iota((rows, GW))
    out = jnp.broadcast_to(cols[NH - 1], (rows, GW))
    for h in range(NH - 2, -1, -1):
        out = jnp.where(lane < (h + 1) * HD, cols[h], out)
    return out


def _dwconv(x, w_ref, pad_ref, n_rows, taps):
    ch = x.shape[1]
    left = taps // 2
    pad_ref[0:8, 0:ch] = jnp.zeros((8, ch), F32)
    pad_ref[8 + n_rows:16 + n_rows, 0:ch] = jnp.zeros((8, ch), F32)
    pad_ref[8:8 + n_rows, 0:ch] = x
    acc = None
    for j in range(taps):
        off = 8 - left + j
        term = pad_ref[off:off + n_rows, 0:ch] * w_ref[j:j + 1, :]
        acc = term if acc is None else acc + term
    return acc


def _mod_kernel(c_ref, w_ref, b_ref, o_ref):
    s = _silu(c_ref[...])
    o_ref[...] = _dot_hi(s, w_ref[...]) + b_ref[...]


def _mod_call(cvec, w_mod, b_mod):
    tn = 1536
    n = w_mod.shape[-1]
    return pl.pallas_call(
        _mod_kernel,
        grid=(DEPTH, n // tn),
        in_specs=[pl.BlockSpec((16, D_MODEL), lambda l, j: (0, 0)),
                  pl.BlockSpec((None, D_MODEL, tn), lambda l, j: (l, 0, j)),
                  pl.BlockSpec((None, 1, tn), lambda l, j: (l, 0, j))],
        out_specs=pl.BlockSpec((None, 16, tn), lambda l, j: (l, 0, j)),
        out_shape=jax.ShapeDtypeStruct((DEPTH, 16, n), F32),
        compiler_params=_cparams(2),
        name="adaln_mod",
    )(cvec, w_mod, b_mod.reshape(DEPTH, 1, n))


def _inproj_kernel(*refs, add_pos):
    if add_pos:
        x_ref, pos_ref, mod_ref, g_ref, w_ref, o_ref, x0_ref = refs
        x = x_ref[...] + pos_ref[...]
        x0_ref[...] = x
    else:
        x_ref, mod_ref, g_ref, w_ref, o_ref = refs
        x = x_ref[...]
    mod = mod_ref[...]
    sh = mod[:, 0:D_MODEL]
    sc = mod[:, D_MODEL:2 * D_MODEL]
    h = x * lax.rsqrt(jnp.mean(x * x, axis=-1, keepdims=True) + EPS) * g_ref[...]
    h = h * (1.0 + sc) + sh
    o_ref[...] = _dot(h.astype(BF16), w_ref[...])


def _inproj_call(x2d, pos, mods_l, g, w_pack, layer, seq_len, is_lat):
    n = x2d.shape[0]
    tm = ROW_TILE
    per_seq = max(seq_len // tm, 1)
    mod_map = (lambda t: (1 + t // per_seq, 0, 0)) if is_lat else (lambda t: (0, 0, 0))
    in_specs = [pl.BlockSpec((tm, D_MODEL), lambda t: (t, 0))]
    args = [x2d]
    if pos is not None:
        in_specs.append(pl.BlockSpec((tm, D_MODEL), lambda t: (t % per_seq, 0)))
        args.append(pos)
    in_specs += [pl.BlockSpec((None, 1, 6 * D_MODEL), mod_map),
                 pl.BlockSpec((1, D_MODEL), lambda t: (0, 0)),
                 pl.BlockSpec((None, D_MODEL, IN_PACK), lambda t: (layer, 0, 0))]
    args += [mods_l, g.reshape(1, D_MODEL), w_pack]
    out_specs = pl.BlockSpec((tm, IN_PACK), lambda t: (t, 0))
    out_shape = jax.ShapeDtypeStruct((n, IN_PACK), F32)
    if pos is not None:
        out_specs = [out_specs, pl.BlockSpec((tm, D_MODEL), lambda t: (t, 0))]
        out_shape = [out_shape, jax.ShapeDtypeStruct((n, D_MODEL), F32)]
    return pl.pallas_call(
        functools.partial(_inproj_kernel, add_pos=pos is not None),
        grid=(n // tm,),
        in_specs=in_specs,
        out_specs=out_specs,
        out_shape=out_shape,
        compiler_params=_cparams(1),
        name="inproj_lat0" if pos is not None else "inproj",
    )(*args)


@functools.lru_cache(maxsize=None)
def _dft_tables(seq_len):
    n = 2 * seq_len - 1
    idx = np.arange(seq_len, dtype=np.int64)
    ang = (2.0 * np.pi / n) * ((idx[:, None] * idx[None, :]) % n).astype(np.float64)
    out = []
    for tab in (np.cos(ang), np.sin(ang)):
        hi = tab.astype(ml_dtypes.bfloat16)
        lo = (tab - hi.astype(np.float64)).astype(ml_dtypes.bfloat16)
        out += [hi, lo]
    return tuple(out)


@functools.lru_cache(maxsize=None)
def _hyena_consts(seq_len):
    t = np.linspace(0.0, 1.0, seq_len, dtype=np.float32)[:, None]
    w = np.float32(2.0 * math.pi / seq_len) * np.arange(seq_len, dtype=np.float32)[:, None]
    bands = np.linspace(1e-4, HY_BANDS - 1, HY_BANDS, dtype=np.float32)[None, :]
    feats = np.concatenate([t, np.cos(bands * w), -np.sin(bands * w)], axis=-1).astype(np.float32)
    feats_p = np.zeros((seq_len, 128), np.float32)
    feats_p[:, :HY_EMB] = feats
    max_decay = math.log(1e-2) / 0.3
    min_decay = math.log(1e-2) / 1.5
    deltas = np.abs(np.linspace(min_decay, max_decay, GW, dtype=np.float32))
    decay = np.exp(-t * deltas).astype(np.float32)
    return feats_p, np.concatenate([decay, decay], axis=1)


def _hyfilt_kernel(feat_ref, dec_ref, w1_ref, b1_ref, w2_ref, b2_ref, w3_ref, fr_ref,
                   ch_ref, cl_ref, sh_ref, sl_ref, hr_ref, hi_ref, *, seq_len):
    fr = fr_ref[...]
    h = jnp.sin(fr[0:1, :] * (_dot_hi(feat_ref[...], w1_ref[...]) + b1_ref[...]))
    h = jnp.sin(fr[1:2, :] * (_dot_hi(h, w2_ref[...]) + b2_ref[...]))
    h = _dot_hi(h, w3_ref[...]) * dec_ref[...]
    den = jnp.sum(jnp.abs(h[:, :GW]) + jnp.abs(h[:, GW:]), axis=0, keepdims=True)
    h = h / jnp.concatenate([den, den], axis=1)
    hc = _dot3_const(ch_ref[...], cl_ref[...], h)
    hs = _dot3_const(sh_ref[...], sl_ref[...], h)
    n = 2 * seq_len - 1
    row = lax.broadcasted_iota(jnp.int32, (seq_len, GW), 0)
    wgt = jnp.where(row == 0, 1.0 / n, 2.0 / n)
    hr_ref[...] = (hc[:, :GW] + hc[:, GW:]) * wgt
    hi_ref[...] = (hs[:, GW:] - hs[:, :GW]) * wgt


def _hyfilt_call(seq_len, hw):
    feats, decay = _hyena_consts(seq_len)
    tabs = _dft_tables(seq_len)
    full = lambda shape: pl.BlockSpec(shape, lambda l: (0,) * len(shape))
    lay = lambda shape: pl.BlockSpec((None,) + shape, lambda l: (l,) + (0,) * len(shape))
    return pl.pallas_call(
        functools.partial(_hyfilt_kernel, seq_len=seq_len),
        grid=(DEPTH,),
        in_specs=[full((seq_len, 128)), full((seq_len, 2 * GW)),
                  lay((128, 128)), lay((1, 128)), lay((128, 128)), lay((1, 128)), lay((128, 2 * GW)),
                  lay((2, 128))] + [full((seq_len, seq_len))] * 4,
        out_specs=[lay((seq_len, GW)), lay((seq_len, GW))],
        out_shape=[jax.ShapeDtypeStruct((DEPTH, seq_len, GW), F32)] * 2,
        compiler_params=_cparams(1),
        name=f"hyena_filter_{seq_len}",
    )(feats, decay, hw["w1"], hw["b1"], hw["w2"], hw["b2"], hw["w3"], hw["freq"], *tabs)


def _hyena_kernel(u_ref, cw_ref, bias_ref, hr_ref, hi_ref, ch_ref, cl_ref, sh_ref, sl_ref,
                  o_ref, pad_ref, *, seq_len, n_seq):
    for b in range(n_seq):
        uc = _dwconv(u_ref[b], cw_ref, pad_ref, seq_len, 3)
        x0 = uc[:, 0:GW]
        z = uc[:, 2 * GW:3 * GW] * uc[:, GW:2 * GW]
        za = _dot3_const(ch_ref[...], cl_ref[...], z)
        zb = _dot3_const(sh_ref[...], sl_ref[...], z)
        hr = hr_ref[...]
        hi = hi_ref[...]
        yr = za * hr + zb * hi
        yi = za * hi - zb * hr
        y = _dot3_const(ch_ref[...], cl_ref[...], yr) - _dot3_const(sh_ref[...], sl_ref[...], yi)
        o_ref[b] = x0 * (y + z * bias_ref[...])


def _hyena_call(proj3, conv_w, bias, hr, hi, seq_len, n_seq):
    bsz = proj3.shape[0]
    tabs = _dft_tables(seq_len)
    full = lambda shape: pl.BlockSpec(shape, lambda b: (0,) * len(shape))
    return pl.pallas_call(
        functools.partial(_hyena_kernel, seq_len=seq_len, n_seq=n_seq),
        grid=(bsz // n_seq,),
        in_specs=[pl.BlockSpec((n_seq, seq_len, 3 * GW), lambda b: (b, 0, COL_HY // (3 * GW))),
                  full((3, 3 * GW)), full((1, GW)), full((seq_len, GW)), full((seq_len, GW))]
                 + [full((seq_len, seq_len))] * 4,
        out_specs=pl.BlockSpec((n_seq, seq_len, GW), lambda b: (b, 0, 0)),
        out_shape=jax.ShapeDtypeStruct((bsz, seq_len, GW), F32),
        scratch_shapes=[pltpu.VMEM((seq_len + 16, 3 * GW), F32)],
        compiler_params=_cparams(1),
        name=f"hyena_{seq_len}",
    )(proj3, conv_w, bias.reshape(1, GW), hr, hi, *tabs)


def _tri_masks(q):
    row = lax.broadcasted_iota(jnp.int32, (q, q), 0)
    col = lax.broadcasted_iota(jnp.int32, (q, q), 1)
    return col <= row, col >= row


def _ssd_kernel(*refs, seq_len, has_s0, emit_state):
    it = iter(refs)
    zx_ref, sm_ref, conv_ref, dtb_r_ref, dtb_c_ref, an_r_ref, an_c_ref, dvec_ref, gn_ref = (
        next(it) for _ in range(9))
    s0_ref = next(it) if has_s0 else None
    o_ref = next(it)
    so_ref = next(it) if emit_state else None
    pad_ref, xs_ref, bm_ref, cm_ref, dtc_ref, ac_ref, ar_ref, y_ref, st_ref = it
    q = SSD_Q
    nc = seq_len // q

    xbc = _silu(_dwconv(zx_ref[:, GW:3 * GW], conv_ref, pad_ref, seq_len, 4))
    xs_ref[...] = xbc[:, 0:GW]
    bm_ref[...] = xbc[:, GW:GW + 128]
    cm_ref[...] = xbc[:, GW + 128:GW + 256]
    y_ref[...] = xbc[:, 0:GW] * dvec_ref[...]
    sm = sm_ref[...]
    dtc = _softplus(sm + dtb_r_ref[...])
    dtc_ref[...] = dtc
    ac_ref[...] = dtc * an_r_ref[...]
    sm_t = sm.T
    ar = _softplus(sm_t[SM_DT:SM_DT + 8, :] + dtb_c_ref[...]) * an_c_ref[...]
    for c in range(nc):
        ar_ref[c] = ar[:, c * q:(c + 1) * q]
    if has_s0:
        st_ref[...] = s0_ref[...]
    else:
        st_ref[...] = jnp.zeros(st_ref.shape, F32)

    lower, upper = _tri_masks(q)
    lo_f = lower.astype(F32).astype(BF16)
    up_f = upper.astype(F32).astype(BF16)
    lane128 = _lane_iota((q, 128))
    lane256 = _lane_iota((q, GW))
    st_row = lax.broadcasted_iota(jnp.int32, (128, GW), 0)
    st_col = lax.broadcasted_iota(jnp.int32, (128, GW), 1)
    st_mask = (st_row // HD) == (st_col // 128)

    def chunk_dir(c, d):
        rows = pl.ds(pl.multiple_of(c * q, q), q)
        xs_c = xs_ref[rows, :]
        b_c = bm_ref[rows, :]
        c_c = cm_ref[rows, :]
        dt_c = dtc_ref[rows, :]
        a_c = ac_ref[rows, :]
        a_r = ar_ref[c]
        if d == 0:
            cum_c = _cumsum_rows(lo_f, a_c)
            cum_r = _cumsum_cols(a_r, up_f)
            mask, last = lower, q - 1
        else:
            cum_c = _cumsum_rows(up_f, a_c)
            cum_r = _cumsum_cols(a_r, lo_f)
            mask, last = upper, 0
        b_t = b_c.T.astype(BF16)
        c_g = jnp.concatenate([jnp.where(lane128 < HD, c_c, 0.0),
                               jnp.where(lane128 >= HD, c_c, 0.0)], axis=0).astype(BF16)
        g_mat = _dot(c_g, b_t)
        w_parts = []
        for h in range(NH):
            j = d * NH + h
            g = h // 2
            diff = cum_c[:, j:j + 1] - cum_r[j:j + 1, :]
            m = jnp.where(mask, jnp.exp(jnp.minimum(diff, 0.0)), 0.0)
            w_parts.append((g_mat[g * q:(g + 1) * q, :] * m).astype(BF16))
        w_cat = jnp.concatenate(w_parts, axis=1)
        cols = [cum_c[:, d * NH + h:d * NH + h + 1] for h in range(NH)]
        lasts = [cum_c[last:last + 1, d * NH + h:d * NH + h + 1] for h in range(NH)]
        xdt = xs_c * _head_lanes([dt_c[:, d * NH + h:d * NH + h + 1] for h in range(NH)], q)
        xdt_b = xdt.astype(BF16)
        stack = jnp.concatenate(
            [jnp.where((lane256 // HD) == h, xdt_b, jnp.zeros_like(xdt_b)) for h in range(NH)], axis=0)
        y_diag = _dot(w_cat, stack)
        ss = st_ref[d]
        y_off = _dot(c_c.astype(BF16), ss.astype(BF16)) * _head_lanes([jnp.exp(cc) for cc in cols], q)
        y_ref[rows, :] += y_diag + y_off
        d_in = _head_lanes([jnp.exp(lasts[h] - cols[h]) for h in range(NH)], q)
        upd = _dot(b_t, (xdt * d_in).astype(BF16))
        dec = _head_lanes([jnp.exp(lv) for lv in lasts], 1)
        st_ref[d] = ss * dec + jnp.where(st_mask, upd, 0.0)

    def body(i, carry):
        chunk_dir(i, 0)
        chunk_dir(nc - 1 - i, 1)
        return carry

    lax.fori_loop(0, nc, body, 0)

    y = y_ref[...] * _silu(zx_ref[:, 0:GW])
    y = y * lax.rsqrt(jnp.mean(y * y, axis=-1, keepdims=True) + EPS) * gn_ref[...]
    o_ref[...] = y
    if emit_state:
        for d in range(2):
            st_t = st_ref[d].T
            for h in range(NH):
                blk = st_t[h * HD:(h + 1) * HD, :]
                if h // 2 == 1:
                    blk = pltpu.roll(blk, HD, axis=1)
                so_ref[d, h] = blk[:, 0:HD]


def _ssd_call(proj3, p, s0, seq_len, emit_state):
    bsz = proj3.shape[0]
    nc = seq_len // SSD_Q
    full = lambda shape: pl.BlockSpec(shape, lambda b: (0,) * len(shape))
    in_specs = [pl.BlockSpec((None, seq_len, 3 * GW), lambda b: (b, 0, COL_SSD // (3 * GW))),
                pl.BlockSpec((None, seq_len, 128), lambda b: (b, 0, COL_SMALL // 128)),
                full((4, 2 * GW)), full((1, 128)), full((8, 1)), full((1, 128)), full((8, 1)),
                full((1, GW)), full((1, GW))]
    args = [proj3, proj3, p["conv"], p["dtb_row"], p["dtb_col"], p["an_row"], p["an_col"], p["dvec"], p["gn"]]
    if s0 is not None:
        in_specs.append(pl.BlockSpec((None, 2, 128, GW), lambda b: (b, 0, 0, 0)))
        args.append(s0)
    out_specs = [pl.BlockSpec((None, seq_len, GW), lambda b: (b, 0, 0))]
    out_shape = [jax.ShapeDtypeStruct((bsz, seq_len, GW), F32)]
    if emit_state:
        out_specs.append(pl.BlockSpec((None, 2, NH, HD, HD), lambda b: (b, 0, 0, 0, 0)))
        out_shape.append(jax.ShapeDtypeStruct((bsz, 2, NH, HD, HD), F32))
    scratch = [pltpu.VMEM((seq_len + 16, 2 * GW), F32),
               pltpu.VMEM((seq_len, GW), F32),
               pltpu.VMEM((seq_len, 128), F32),
               pltpu.VMEM((seq_len, 128), F32),
               pltpu.VMEM((seq_len, 128), F32),
               pltpu.VMEM((seq_len, 128), F32),
               pltpu.VMEM((nc, 8, SSD_Q), F32),
               pltpu.VMEM((seq_len, GW), F32),
               pltpu.VMEM((2, 128, GW), F32)]
    return pl.pallas_call(
        functools.partial(_ssd_kernel, seq_len=seq_len, has_s0=s0 is not None, emit_state=emit_state),
        grid=(bsz,),
        in_specs=in_specs,
        out_specs=out_specs,
        out_shape=out_shape,
        scratch_shapes=scratch,
        compiler_params=_cparams(1),
        name=f"ssd_{seq_len}",
    )(*args)


def _lru_kernel(*refs, seq_len, has_s0, emit_state):
    it = iter(refs)
    u_ref, conv_ref, w_ref, bias_ref, sp_ref = (next(it) for _ in range(5))
    s0_ref = next(it) if has_s0 else None
    o_ref = next(it)
    so_ref = next(it) if emit_state else None
    pad_ref, af_ref, bf_ref, ab_ref, bb_ref = it
    n = seq_len

    xc = _dwconv(u_ref[:, 0:GW], conv_ref, pad_ref, n, 4)
    ri = jax.nn.sigmoid(_dot(xc.astype(BF16), w_ref[...]) + bias_ref[...])
    log_a = -LRU_C * ri[:, 0:2 * GW] * sp_ref[...]
    a = jnp.exp(log_a)
    bt = jnp.sqrt(jnp.maximum(1.0 - a * a, 0.0)) * ri[:, 2 * GW:4 * GW] * jnp.concatenate([xc, xc], axis=1)

    ones8 = jnp.ones((8, GW), F32)
    zeros8 = jnp.zeros((8, GW), F32)
    for a_ref_, b_ref_, lo in ((af_ref, bf_ref, 0), (ab_ref, bb_ref, GW)):
        a_ref_[0:8, :] = ones8
        a_ref_[8 + n:16 + n, :] = ones8
        b_ref_[0:8, :] = zeros8
        b_ref_[8 + n:16 + n, :] = zeros8
        a_ref_[8:8 + n, :] = a[:, lo:lo + GW]
        b_ref_[8:8 + n, :] = bt[:, lo:lo + GW]
    if has_s0:
        s0 = s0_ref[...]
        bf_ref[8:9, :] = bf_ref[8:9, :] + af_ref[8:9, :] * s0[0:1, :]
        bb_ref[7 + n:8 + n, :] = bb_ref[7 + n:8 + n, :] + ab_ref[7 + n:8 + n, :] * s0[1:2, :]

    d = 1
    while d < n:
        if d < 8:
            a_cur = af_ref[8:8 + n, :]
            af_ref[8:8 + n, :] = a_cur * af_ref[8 - d:8 - d + n, :]
            bf_ref[8:8 + n, :] = a_cur * bf_ref[8 - d:8 - d + n, :] + bf_ref[8:8 + n, :]
            a_cur = ab_ref[8:8 + n, :]
            ab_ref[8:8 + n, :] = a_cur * ab_ref[8 + d:8 + d + n, :]
            bb_ref[8:8 + n, :] = a_cur * bb_ref[8 + d:8 + d + n, :] + bb_ref[8:8 + n, :]
        else:
            m = n - d
            a_cur = af_ref[8 + d:8 + n, :]
            a_new = a_cur * af_ref[8:8 + m, :]
            b_new = a_cur * bf_ref[8:8 + m, :] + bf_ref[8 + d:8 + n, :]
            af_ref[8 + d:8 + n, :] = a_new
            bf_ref[8 + d:8 + n, :] = b_new
            a_cur = ab_ref[8:8 + m, :]
            a_new = a_cur * ab_ref[8 + d:8 + n, :]
            b_new = a_cur * bb_ref[8 + d:8 + n, :] + bb_ref[8:8 + m, :]
            ab_ref[8:8 + m, :] = a_new
            bb_ref[8:8 + m, :] = b_new
        d *= 2

    y = (bf_ref[8:8 + n, :] + bb_ref[8:8 + n, :]) * _gelu_tanh(u_ref[:, GW:2 * GW])
    o_ref[...] = y
    if emit_state:
        so_ref[0:1, :] = bf_ref[7 + n:8 + n, :]
        so_ref[1:2, :] = bb_ref[8:9, :]


def _lru_call(proj3, p, s0, seq_len, emit_state):
    bsz = proj3.shape[0]
    full = lambda shape: pl.BlockSpec(shape, lambda b: (0,) * len(shape))
    in_specs = [pl.BlockSpec((None, seq_len, 2 * GW), lambda b: (b, 0, COL_LRU // (2 * GW))),
                full((4, GW)), full((GW, 4 * GW)), full((1, 4 * GW)), full((1, 2 * GW))]
    args = [proj3, p["conv"], p["w"], p["bias"], p["sp"]]
    if s0 is not None:
        in_specs.append(pl.BlockSpec((None, 2, GW), lambda b: (b, 0, 0)))
        args.append(s0)
    out_specs = [pl.BlockSpec((None, seq_len, GW), lambda b: (b, 0, 0))]
    out_shape = [jax.ShapeDtypeStruct((bsz, seq_len, GW), F32)]
    if emit_state:
        out_specs.append(pl.BlockSpec((None, 2, GW), lambda b: (b, 0, 0)))
        out_shape.append(jax.ShapeDtypeStruct((bsz, 2, GW), F32))
    scratch = [pltpu.VMEM((seq_len + 16, GW), F32)] * 5
    return pl.pallas_call(
        functools.partial(_lru_kernel, seq_len=seq_len, has_s0=s0 is not None, emit_state=emit_state),
        grid=(bsz,),
        in_specs=in_specs,
        out_specs=out_specs,
        out_shape=out_shape,
        scratch_shapes=scratch,
        compiler_params=_cparams(1),
        name=f"rglru_{seq_len}",
    )(*args)


TRI_BASE = 8


def _unit_tri_inverse(a_mats, eye, same_blk):
    c = a_mats[0].shape[0]
    ds = [jnp.where(same_blk[TRI_BASE], a, 0.0) for a in a_mats]
    d_bs = [d.astype(BF16) for d in ds]
    ps = [eye - d for d in ds]
    x_bs = [_dot(d_b, d_b).astype(BF16) for d_b in d_bs]
    rs = [_dot(x_b, jnp.concatenate([p.astype(BF16), x_b], axis=1)) for x_b, p in zip(x_bs, ps)]
    ps = [p + r[:, 0:c] for p, r in zip(ps, rs)]
    ts = [p + _dot(r[:, c:2 * c].astype(BF16), p.astype(BF16)) for p, r in zip(ps, rs)]
    size = 2 * TRI_BASE
    while size <= c:
        off = same_blk[size] & jnp.logical_not(same_blk[size // 2])
        e_bs = [jnp.where(off, a, 0.0).astype(BF16) for a in a_mats]
        t_bs = [t.astype(BF16) for t in ts]
        us = [_dot(t_b, e_b).astype(BF16) for t_b, e_b in zip(t_bs, e_bs)]
        ts = [t - _dot(u, t_b) for t, u, t_b in zip(ts, us, t_bs)]
        size *= 2
    return ts


def _gdn_kernel(*refs, seq_len, has_s0, emit_state):
    it = iter(refs)
    u_ref, sm_ref, conv_ref, dtb_r_ref, dtb_c_ref, an_r_ref, an_c_ref, gn_ref, ones_ref = (
        next(it) for _ in range(9))
    s0_ref = next(it) if has_s0 else None
    o_ref = next(it)
    so_ref = next(it) if emit_state else None
    pad_ref, q_ref, k_ref, v_ref, be_ref, gc_ref, gr_ref, oa_ref, st_ref = it
    c = GDN_C
    nc = seq_len // c
    ones_bd = ones_ref[...]

    qkv = _silu(_dwconv(u_ref[:, 0:3 * GW], conv_ref, pad_ref, seq_len, 4))
    qq = qkv[:, 0:GW]
    kk = qkv[:, GW:2 * GW]
    q_ref[...] = qq * lax.rsqrt(_seg_sum(qq * qq, ones_bd) + EPS) * (HD ** -0.5)
    k_ref[...] = kk * lax.rsqrt(_seg_sum(kk * kk, ones_bd) + EPS)
    v_ref[...] = qkv[:, 2 * GW:3 * GW]
    sm = sm_ref[...]
    be_ref[...] = jax.nn.sigmoid(sm)
    gc_ref[...] = an_r_ref[...] * _softplus(sm + dtb_r_ref[...])
    sm_t = sm.T
    g_r = an_c_ref[...] * _softplus(sm_t[SM_A:SM_A + 8, :] + dtb_c_ref[...])
    for i in range(nc):
        gr_ref[i] = g_r[:, i * c:(i + 1) * c]
    oa_ref[...] = jnp.zeros(oa_ref.shape, F32)
    if has_s0:
        st_ref[...] = s0_ref[...]
    else:
        st_ref[...] = jnp.zeros(st_ref.shape, F32)

    lower, upper = _tri_masks(c)
    lo_f = lower.astype(F32).astype(BF16)
    up_f = upper.astype(F32).astype(BF16)
    row_i =lax.broadcasted_iota(jnp.int32, (c, c), 0)
    col_i = lax.broadcasted_iota(jnp.int32, (c, c), 1)
    eye = (row_i == col_i).astype(F32)
    same_blk = {}
    size = TRI_BASE
    while size <= c:
        same_blk[size] = (row_i // size) == (col_i // size)
        size *= 2
    lane256 = _lane_iota((c, GW))
    lane512 = _lane_iota((c, 2 * GW))
    bd_row = lax.broadcasted_iota(jnp.int32, (GW, GW), 0)
    bd_col = lax.broadcasted_iota(jnp.int32, (GW, GW), 1)
    bd_mask = (bd_row // HD) == (bd_col // HD)
    kt_row = lax.broadcasted_iota(jnp.int32, (GW, c), 0)

    def stack_heads(x_b, lane):
        zero = jnp.zeros_like(x_b)
        return jnp.concatenate([jnp.where(((lane % GW) // HD) == h, x_b, zero) for h in range(NH)], axis=0)

    def body(i, carry):
        dirs = (0, 1)
        cis = (i, nc - 1 - i)
        rows = [pl.ds(pl.multiple_of(ci * c, c), c) for ci in cis]
        q_c = [q_ref[r, :] for r in rows]
        k_c = [k_ref[r, :] for r in rows]
        v_c = [v_ref[r, :] for r in rows]
        be = [be_ref[r, :] for r in rows]
        tri_c = (lo_f, up_f)
        tri_r = (up_f, lo_f)
        incl = (lower, upper)
        strict = (col_i < row_i, col_i > row_i)
        last = (c - 1, 0)
        cum_c = [_cumsum_rows(tri_c[d], gc_ref[rows[d], :]) for d in dirs]
        cum_r = [_cumsum_cols(gr_ref[cis[d]], tri_r[d]) for d in dirs]
        kq = []
        for d in dirs:
            k_t = k_c[d].T.astype(BF16)
            k_t_heads = jnp.concatenate(
                [jnp.where((kt_row // HD) == h, k_t, jnp.zeros_like(k_t)) for h in range(NH)], axis=1)
            kq.append(_dot(jnp.concatenate([k_c[d], q_c[d]], axis=0).astype(BF16), k_t_heads))
        a_mats, attn = [], []
        for d in dirs:
            for h in range(NH):
                j = d * NH + h
                diff = cum_c[d][:, SM_A + j:SM_A + j + 1] - cum_r[d][j:j + 1, :]
                dec = jnp.where(incl[d], jnp.exp(jnp.minimum(diff, 0.0)), 0.0)
                a_mats.append(jnp.where(
                    strict[d], kq[d][0:c, h * c:(h + 1) * c] * be[d][:, SM_B + j:SM_B + j + 1] * dec, 0.0))
                attn.append((kq[d][c:2 * c, h * c:(h + 1) * c] * dec).astype(BF16))
        t_mats = _unit_tri_inverse(a_mats, eye, same_blk)
        t_cat = [jnp.concatenate([t.astype(BF16) for t in t_mats[d * NH:(d + 1) * NH]], axis=1) for d in dirs]
        a_split = [_split_bf16(a) for a in a_mats]
        ah_cat = [jnp.concatenate([a_split[d * NH + h][0] for h in range(NH)], axis=1) for d in dirs]
        al_cat = [jnp.concatenate([a_split[d * NH + h][1] for h in range(NH)], axis=1) for d in dirs]
        cols = [[cum_c[d][:, SM_A + d * NH + h:SM_A + d * NH + h + 1] for h in range(NH)] for d in dirs]
        lasts = [[cum_c[d][last[d]:last[d] + 1, SM_A + d * NH + h:SM_A + d * NH + h + 1] for h in range(NH)]
                 for d in dirs]
        b_l = [_head_lanes([be[d][:, SM_B + d * NH + h:SM_B + d * NH + h + 1] for h in range(NH)], c) for d in dirs]
        e_l = [_head_lanes([jnp.exp(cc) for cc in cols[d]], c) for d in dirs]
        rhs = [jnp.concatenate([v_c[d] * b_l[d], k_c[d] * b_l[d] * e_l[d]], axis=1) for d in dirs]
        sol = [_dot(t_cat[d], stack_heads(rhs[d].astype(BF16), lane512)) for d in dirs]
        s_split = [_split_bf16(s) for s in sol]
        st_hi = [stack_heads(s_split[d][0], lane512) for d in dirs]
        st_lo = [stack_heads(s_split[d][1], lane512) for d in dirs]
        a_sol = [_dot(ah_cat[d], st_hi[d]) + _dot(ah_cat[d], st_lo[d]) + _dot(al_cat[d], st_hi[d]) for d in dirs]
        uw = [sol[d] + _dot(t_cat[d], stack_heads((rhs[d] - sol[d] - a_sol[d]).astype(BF16), lane512))
              for d in dirs]
        ss = [st_ref[d] for d in dirs]
        r2 = [_dot(jnp.concatenate([uw[d][:, GW:2 * GW], q_c[d] * e_l[d]], axis=0).astype(BF16),
                   ss[d].astype(BF16)) for d in dirs]
        v_new_b = [(uw[d][:, 0:GW] - r2[d][0:c, :]).astype(BF16) for d in dirs]
        o_c = [r2[d][c:2 * c, :] + _dot(jnp.concatenate(attn[d * NH:(d + 1) * NH], axis=1),
                                        stack_heads(v_new_b[d], lane256)) for d in dirs]
        for d in dirs:
            oa_ref[rows[d], :] += o_c[d]
        for d in dirs:
            k_dec = k_c[d] * _head_lanes([jnp.exp(lasts[d][h] - cols[d][h]) for h in range(NH)], c)
            upd = _dot(k_dec.T.astype(BF16), v_new_b[d])
            dec_row = _head_lanes([jnp.exp(lv) for lv in lasts[d]], 1)
            st_ref[d] = ss[d] * dec_row + jnp.where(bd_mask, upd, 0.0)
        return carry

    lax.fori_loop(0, nc, body, 0)

    o = oa_ref[...]
    o = o * lax.rsqrt(_seg_sum(o * o, ones_bd) * (1.0 / HD) + EPS) * gn_ref[...]
    o_ref[...] = o * _silu(u_ref[:, 3 * GW:4 * GW])
    if emit_state:
        for d in range(2):
            for h in range(NH):
                blk = st_ref[d, h * HD:(h + 1) * HD, (h // 2) * 128:(h // 2 + 1) * 128]
                if h % 2 == 1:
                    blk = pltpu.roll(blk, HD, axis=1)
                so_ref[d, h] = blk[:, 0:HD]


def _gdn_call(proj3, p, s0, seq_len, emit_state):
    bsz = proj3.shape[0]
    nc = seq_len // GDN_C
    full = lambda shape: pl.BlockSpec(shape, lambda b: (0,) * len(shape))
    in_specs = [pl.BlockSpec((None, seq_len, 4 * GW), lambda b: (b, 0, COL_GDN // (4 * GW))),
                pl.BlockSpec((None, seq_len, 128), lambda b: (b, 0, COL_SMALL // 128)),
                full((4, 3 * GW)), full((1, 128)), full((8, 1)), full((1, 128)), full((8, 1)),
                full((1, GW)), full((GW, GW))]
    args = [proj3, proj3, p["conv"], p["dtb_row"], p["dtb_col"], p["an_row"], p["an_col"], p["gn"], p["ones"]]
    if s0 is not None:
        in_specs.append(pl.BlockSpec((None, 2, GW, GW), lambda b: (b, 0, 0, 0)))
        args.append(s0)
    out_specs = [pl.BlockSpec((None, seq_len, GW), lambda b: (b, 0, 0))]
    out_shape = [jax.ShapeDtypeStruct((bsz, seq_len, GW), F32)]
    if emit_state:
        out_specs.append(pl.BlockSpec((None, 2, NH, HD, HD), lambda b: (b, 0, 0, 0, 0)))
        out_shape.append(jax.ShapeDtypeStruct((bsz, 2, NH, HD, HD), F32))
    scratch = [pltpu.VMEM((seq_len + 16, 3 * GW), F32),
               pltpu.VMEM((seq_len, GW), F32),
               pltpu.VMEM((seq_len, GW), F32),
               pltpu.VMEM((seq_len, GW), F32),
               pltpu.VMEM((seq_len, 128), F32),
               pltpu.VMEM((seq_len, 128), F32),
               pltpu.VMEM((nc, 8, GDN_C), F32),
               pltpu.VMEM((seq_len, GW), F32),
               pltpu.VMEM((2, GW, GW), F32)]
    return pl.pallas_call(
        functools.partial(_gdn_kernel, seq_len=seq_len, has_s0=s0 is not None, emit_state=emit_state),
        grid=(bsz,),
        in_specs=in_specs,
        out_specs=out_specs,
        out_shape=out_shape,
        scratch_shapes=scratch,
        compiler_params=_cparams(1),
        name=f"gdn_{seq_len}",
    )(*args)


def _ffn_kernel(x_ref, ohy_ref, ossd_ref, olru_ref, ogdn_ref, mod_ref, g_ref, wo_ref, wg_ref, wu_ref,
                wd_ref, gfin_ref, o_ref, *, final_norm):
    mod = mod_ref[...]
    ga_m = mod[:, 2 * D_MODEL:3 * D_MODEL]
    sh_f = mod[:, 3 * D_MODEL:4 * D_MODEL]
    sc_f = mod[:, 4 * D_MODEL:5 * D_MODEL]
    ga_f = mod[:, 5 * D_MODEL:6 * D_MODEL]
    mo = None
    for i, r in enumerate((ohy_ref, ossd_ref, olru_ref, ogdn_ref)):
        part = _dot(r[...].astype(BF16), wo_ref[i * GW:(i + 1) * GW, :])
        mo = part if mo is None else mo + part
    x = x_ref[...] + ga_m * mo
    h = x * lax.rsqrt(jnp.mean(x * x, axis=-1, keepdims=True) + EPS) * g_ref[...]
    h = (h * (1.0 + sc_f) + sh_f).astype(BF16)
    half = D_FF // 2
    ff = None
    for i in range(2):
        gate = _dot(h, wg_ref[:, i * half:(i + 1) * half])
        up = _dot(h, wu_ref[:, i * half:(i + 1) * half])
        part = _dot((_silu(gate) * up).astype(BF16), wd_ref[i * half:(i + 1) * half, :])
        ff = part if ff is None else ff + part
    x = x + ga_f * ff
    if final_norm:
        x = x * lax.rsqrt(jnp.mean(x * x, axis=-1, keepdims=True) + EPS) * gfin_ref[...]
    o_ref[...] = x


def _ffn_call(x2d, outs, mods_l, g, weights, layer, g_final, seq_len, is_lat, final_norm):
    n = x2d.shape[0]
    tm = ROW_TILE
    per_seq = seq_len // tm
    mod_map = (lambda t: (1 + t // per_seq, 0, 0)) if is_lat else (lambda t: (0, 0, 0))
    tile = lambda w: pl.BlockSpec((tm, w), lambda t: (t, 0))
    res = lambda shape: pl.BlockSpec(shape, lambda t: (0, 0), pipeline_mode=pl.Buffered(1))
    wres = lambda shape: pl.BlockSpec((None,) + shape, lambda t: (layer, 0, 0), pipeline_mode=pl.Buffered(1))
    return pl.pallas_call(
        functools.partial(_ffn_kernel, final_norm=final_norm),
        grid=(n // tm,),
        in_specs=[tile(D_MODEL), tile(GW), tile(GW), tile(GW), tile(GW),
                  pl.BlockSpec((None, 1, 6 * D_MODEL), mod_map),
                  res((1, D_MODEL)), wres((D_MODEL, D_MODEL)), wres((D_MODEL, D_FF)), wres((D_MODEL, D_FF)),
                  wres((D_FF, D_MODEL)), res((1, D_MODEL))],
        out_specs=tile(D_MODEL),
        out_shape=jax.ShapeDtypeStruct((n, D_MODEL), F32),
        compiler_params=_cparams(1),
        name="outproj_ffn",
    )(x2d, *outs, mods_l, g.reshape(1, D_MODEL), *weights, g_final.reshape(1, D_MODEL))


def _grid_pos_embed(n_tokens):
    rows = n_tokens // GRID_W
    rr, cc = np.meshgrid(np.arange(rows, dtype=np.float32), np.arange(GRID_W, dtype=np.float32), indexing="ij")
    quarter = D_MODEL // 4
    omega = (1.0 / (np.float32(POS_BASE) ** (np.arange(quarter, dtype=np.float32) / quarter))).astype(np.float32)

    def enc(pos):
        ang = pos.reshape(-1)[:, None] * omega[None, :]
        return np.concatenate([np.sin(ang), np.cos(ang)], axis=-1)

    return jnp.asarray(np.concatenate([enc(rr), enc(cc)], axis=-1).astype(np.float32))


def _pad_small(vec8, offset):
    row = jnp.zeros((1, 128), F32).at[0, offset:offset + 8].set(vec8)
    return row, vec8.reshape(8, 1)


def _pack_w_in(w):
    hy_in, ssd_in, lru_in = 768, 776, 512
    o_ssd = hy_in
    o_lru = o_ssd + ssd_in
    o_gdn = o_lru + lru_in
    pieces = [w[..., 0:o_ssd + 768],
              w[..., o_lru:o_gdn + 1024],
              w[..., o_ssd + 768:o_ssd + 776],
              w[..., o_gdn + 1024:o_gdn + 1040],
              jnp.zeros(w.shape[:-1] + (IN_PACK - 3096,), w.dtype)]
    return jnp.concatenate(pieces, axis=-1).astype(BF16)


def _ssd_state_in(s):
    st = jnp.transpose(s, (0, 1, 4, 2, 3))
    grp = (jnp.arange(NH) // 2)[None, :] == jnp.arange(2)[:, None]
    full = st[:, :, None] * grp[None, None, :, None, :, None].astype(s.dtype)
    return full.reshape(s.shape[0], 2, 128, GW)


def _gdn_state_in(s):
    bsz = s.shape[0]
    eye = jnp.eye(NH, dtype=s.dtype)
    full = s[:, :, :, :, None, :] * eye[None, None, :, None, :, None]
    return full.reshape(bsz, 2, GW, GW)


def kernel(x_prompt, x_sample, state_lru, state_ssd, state_gdn, c, c_ctx, w_mod, b_mod, g_mix, g_ffn, g_final, w_in, w_out, hy_conv, hy_w1, hy_b1, hy_w2, hy_b2, hy_w3, hy_freq, hy_bias, ssd_conv, ssd_dt_bias, ssd_a_log, ssd_d, ssd_norm, lru_conv, lru_w_r, lru_b_r, lru_w_i, lru_b_i, lru_lambda, gdn_conv, gdn_dt_bias, gdn_a_log, gdn_norm, w_gate, w_up, w_down):
    b_ctx, l_ctx, _ = x_prompt.shape
    b_lat, l_lat, _ = x_sample.shape

    cvec = jnp.zeros((16, D_MODEL), F32).at[0].set(c_ctx).at[1:1 + b_lat].set(c)
    mods = _mod_call(cvec, w_mod, b_mod).reshape(DEPTH, 16, 1, 6 * D_MODEL)

    hw = {"w1": jnp.zeros((DEPTH, 128, 128), F32).at[:, :HY_EMB, :HY_HIDDEN].set(hy_w1),
          "b1": jnp.zeros((DEPTH, 1, 128), F32).at[:, 0, :HY_HIDDEN].set(hy_b1),
          "w2": jnp.zeros((DEPTH, 128, 128), F32).at[:, :HY_HIDDEN, :HY_HIDDEN].set(hy_w2),
          "b2": jnp.zeros((DEPTH, 1, 128), F32).at[:, 0, :HY_HIDDEN].set(hy_b2),
          "w3": jnp.zeros((DEPTH, 128, 2 * GW), F32).at[:, :HY_HIDDEN, :].set(hy_w3),
          "freq": jnp.zeros((DEPTH, 2, 128), F32).at[:, :, :HY_HIDDEN].set(hy_freq)}
    filt = {l_ctx: _hyfilt_call(l_ctx, hw), l_lat: _hyfilt_call(l_lat, hw)}

    ones_bd = jnp.asarray(np.kron(np.eye(NH, dtype=np.float32), np.ones((HD, HD), np.float32))).astype(BF16)
    pos = _grid_pos_embed(l_lat)

    x_ctx = x_prompt.reshape(b_ctx * l_ctx, D_MODEL)
    x_lat = x_sample.reshape(b_lat * l_lat, D_MODEL)
    w_pack = _pack_w_in(w_in)
    ffn_w = tuple(w.astype(BF16) for w in (w_out, w_gate, w_up, w_down))
    lru_out, ssd_out, gdn_out = [], [], []
    for l in range(DEPTH):
        ssd_dtb_r, ssd_dtb_c = _pad_small(ssd_dt_bias[l].reshape(8), SM_DT)
        ssd_an_r, ssd_an_c = _pad_small(-jnp.exp(ssd_a_log[l].reshape(8)), SM_DT)
        ssd_p = {"conv": ssd_conv[l], "dtb_row": ssd_dtb_r, "dtb_col": ssd_dtb_c, "an_row": ssd_an_r,
                 "an_col": ssd_an_c, "dvec": jnp.repeat(ssd_d[l], HD).reshape(1, GW),
                 "gn": ssd_norm[l].reshape(1, GW)}
        eye_h = jnp.eye(NH, dtype=F32)
        blockdiag = lambda w: (w[:, :, :, None, :] * eye_h[None, :, None, :, None]).reshape(2, GW, GW)
        w_r, w_i = blockdiag(lru_w_r[l]), blockdiag(lru_w_i[l])
        lru_p = {"conv": lru_conv[l],
                 "w": jnp.concatenate([w_r[0], w_r[1], w_i[0], w_i[1]], axis=1).astype(BF16),
                 "bias": jnp.concatenate([lru_b_r[l].reshape(1, 2 * GW), lru_b_i[l].reshape(1, 2 * GW)], axis=1),
                 "sp": jax.nn.softplus(-lru_lambda[l]).reshape(1, 2 * GW)}
        gdn_dtb_r, gdn_dtb_c = _pad_small(gdn_dt_bias[l].reshape(8), SM_A)
        gdn_an_r, gdn_an_c = _pad_small(-jnp.exp(gdn_a_log[l].reshape(8)), SM_A)
        gdn_p = {"conv": gdn_conv[l], "dtb_row": gdn_dtb_r, "dtb_col": gdn_dtb_c, "an_row": gdn_an_r,
                 "an_col": gdn_an_c, "gn": jnp.tile(gdn_norm[l], NH).reshape(1, GW), "ones": ones_bd}
        last = l == DEPTH - 1

        proj = _inproj_call(x_ctx, None, mods[l], g_mix[l], w_pack, l, l_ctx, False).reshape(b_ctx, l_ctx, IN_PACK)
        o_hy = _hyena_call(proj, hy_conv[l], hy_bias[l], filt[l_ctx][0][l], filt[l_ctx][1][l], l_ctx, 4)
        o_ssd, s_ssd = _ssd_call(proj, ssd_p, None, l_ctx, True)
        o_lru, s_lru = _lru_call(proj, lru_p, None, l_ctx, True)
        o_gdn, s_gdn = _gdn_call(proj, gdn_p, None, l_ctx, True)
        outs = [o.reshape(b_ctx * l_ctx, GW) for o in (o_hy, o_ssd, o_lru, o_gdn)]
        x_ctx = _ffn_call(x_ctx, outs, mods[l], g_ffn[l], ffn_w, l, g_final, l_ctx, False, last)
        lru_out.append(s_lru)
        ssd_out.append(s_ssd)
        gdn_out.append(s_gdn)

        if l == 0:
            proj, x_lat = _inproj_call(x_lat, pos, mods[l], g_mix[l], w_pack, l, l_lat, True)
        else:
            proj = _inproj_call(x_lat, None, mods[l], g_mix[l], w_pack, l, l_lat, True)
        proj = proj.reshape(b_lat, l_lat, IN_PACK)
        o_hy = _hyena_call(proj, hy_conv[l], hy_bias[l], filt[l_lat][0][l], filt[l_lat][1][l], l_lat, 1)
        o_ssd, = _ssd_call(proj, ssd_p, _ssd_state_in(state_ssd[:, l]), l_lat, False)
        o_lru, = _lru_call(proj, lru_p, state_lru[:, l], l_lat, False)
        o_gdn, = _gdn_call(proj, gdn_p, _gdn_state_in(state_gdn[:, l]), l_lat, False)
        outs = [o.reshape(b_lat * l_lat, GW) for o in (o_hy, o_ssd, o_lru, o_gdn)]
        x_lat = _ffn_call(x_lat, outs, mods[l], g_ffn[l], ffn_w, l, g_final, l_lat, True, last)

    return (x_ctx.reshape(b_ctx, l_ctx, D_MODEL), x_lat.reshape(b_lat, l_lat, D_MODEL),
            jnp.stack(lru_out, axis=1), jnp.stack(ssd_out, axis=1), jnp.stack(gdn_out, axis=1))
```

```python
import functools
import math

import ml_dtypes
import numpy as np
import jax
import jax.numpy as jnp
from jax import lax
from jax.experimental import pallas as pl
from jax.experimental.pallas import tpu as pltpu

F32 = jnp.float32
BF16 = jnp.bfloat16
HI = lax.Precision.HIGHEST

D_MODEL = 1024
DEPTH = 2
EPS = 1e-6
GRID_W = 64
POS_BASE = 10000.0
GW = 256
D_FF = 2816
HD = 64
NH = 4
HY_BANDS = 16
HY_EMB = 1 + 2 * HY_BANDS
HY_HIDDEN = 64
LRU_C = 8.0

COL_HY = 0
COL_SSD = 768
COL_LRU = 1536
COL_GDN = 2048
COL_SMALL = 3072
IN_PACK = 3200
SM_DT, SM_B, SM_A = 0, 8, 16

SSD_Q = 128
GDN_C = 128
ROW_TILE = 512
V7X_VMEM_LIMIT = 56 * 1024 * 1024


def _cparams(n_axes=1):
    return pltpu.CompilerParams(dimension_semantics=("arbitrary",) * n_axes,
                                vmem_limit_bytes=V7X_VMEM_LIMIT)


def _silu(x):
    return x * jax.nn.sigmoid(x)


def _softplus(x):
    return jnp.maximum(x, 0.0) + jnp.log1p(jnp.exp(-jnp.abs(x)))


def _gelu_tanh(x):
    return 0.5 * x * (1.0 + jnp.tanh(math.sqrt(2.0 / math.pi) * (x + 0.044715 * (x * x * x))))


def _split_bf16(x):
    hi = x.astype(BF16)
    lo = (x - hi.astype(F32)).astype(BF16)
    return hi, lo


def _dot(a, b):
    return jnp.dot(a, b, preferred_element_type=F32)


def _dot_hi(a, b):
    return jnp.dot(a, b, precision=HI, preferred_element_type=F32)


def _cumsum_rows(tri_b, x):
    x1 = x.astype(BF16)
    r = x - x1.astype(F32)
    x2 = r.astype(BF16)
    x3 = (r - x2.astype(F32)).astype(BF16)
    return _dot(tri_b, x1) + _dot(tri_b, x2) + _dot(tri_b, x3)


def _cumsum_cols(x, tri_b):
    x1 = x.astype(BF16)
    r = x - x1.astype(F32)
    x2 = r.astype(BF16)
    x3 = (r - x2.astype(F32)).astype(BF16)
    return _dot(x1, tri_b) + _dot(x2, tri_b) + _dot(x3, tri_b)


def _dot3_const(ah, al, x):
    xh, xl = _split_bf16(x)
    return _dot(ah, xh) + _dot(ah, xl) + _dot(al, xh)


def _seg_sum(x, ones_bd):
    xh, xl = _split_bf16(x)
    return _dot(xh, ones_bd) + _dot(xl, ones_bd)


def _lane_iota(shape):
    return lax.broadcasted_iota(jnp.int32, shape, len(shape) - 1)


def _head_lanes(cols, rows):
    lane = _lane_iota((rows, GW))
    out = jnp.broadcast_to(cols[NH - 1], (rows, GW))
    for h in range(NH - 2, -1, -1):
        out = jnp.where(lane < (h + 1) * HD, cols[h], out)
    return out


def _dwconv(x, w_ref, pad_ref, n_rows, taps):
    ch = x.shape[1]
    left = taps // 2
    pad_ref[0:8, 0:ch] = jnp.zeros((8, ch), F32)
    pad_ref[8 + n_rows:16 + n_rows, 0:ch] = jnp.zeros((8, ch), F32)
    pad_ref[8:8 + n_rows, 0:ch] = x
    acc = None
    for j in range(taps):
        off = 8 - left + j
        term = pad_ref[off:off + n_rows, 0:ch] * w_ref[j:j + 1, :]
        acc = term if acc is None else acc + term
    return acc


def _mod_kernel(c_ref, w_ref, b_ref, o_ref):
    s = _silu(c_ref[...])
    o_ref[...] = _dot_hi(s, w_ref[...]) + b_ref[...]


def _mod_call(cvec, w_mod, b_mod):
    tn = 1536
    n = w_mod.shape[-1]
    return pl.pallas_call(
        _mod_kernel,
        grid=(DEPTH, n // tn),
        in_specs=[pl.BlockSpec((16, D_MODEL), lambda l, j: (0, 0)),
                  pl.BlockSpec((None, D_MODEL, tn), lambda l, j: (l, 0, j)),
                  pl.BlockSpec((None, 1, tn), lambda l, j: (l, 0, j))],
        out_specs=pl.BlockSpec((None, 16, tn), lambda l, j: (l, 0, j)),
        out_shape=jax.ShapeDtypeStruct((DEPTH, 16, n), F32),
        compiler_params=_cparams(2),
        name="adaln_mod",
    )(cvec, w_mod, b_mod.reshape(DEPTH, 1, n))


def _inproj_kernel(*refs, add_pos):
    if add_pos:
        x_ref, pos_ref, mod_ref, g_ref, w_ref, o_ref, x0_ref = refs
        x = x_ref[...] + pos_ref[...]
        x0_ref[...] = x
    else:
        x_ref, mod_ref, g_ref, w_ref, o_ref = refs
        x = x_ref[...]
    mod = mod_ref[...]
    sh = mod[:, 0:D_MODEL]
    sc = mod[:, D_MODEL:2 * D_MODEL]
    h = x * lax.rsqrt(jnp.mean(x * x, axis=-1, keepdims=True) + EPS) * g_ref[...]
    h = h * (1.0 + sc) + sh
    o_ref[...] = _dot(h.astype(BF16), w_ref[...])


def _inproj_call(x2d, pos, mods_l, g, w_pack, layer, seq_len, is_lat):
    n = x2d.shape[0]
    tm = ROW_TILE
    per_seq = max(seq_len // tm, 1)
    mod_map = (lambda t: (1 + t // per_seq, 0, 0)) if is_lat else (lambda t: (0, 0, 0))
    in_specs = [pl.BlockSpec((tm, D_MODEL), lambda t: (t, 0))]
    args = [x2d]
    if pos is not None:
        in_specs.append(pl.BlockSpec((tm, D_MODEL), lambda t: (t % per_seq, 0)))
        args.append(pos)
    in_specs += [pl.BlockSpec((None, 1, 6 * D_MODEL), mod_map),
                 pl.BlockSpec((1, D_MODEL), lambda t: (0, 0)),
                 pl.BlockSpec((None, D_MODEL, IN_PACK), lambda t: (layer, 0, 0))]
    args += [mods_l, g.reshape(1, D_MODEL), w_pack]
    out_specs = pl.BlockSpec((tm, IN_PACK), lambda t: (t, 0))
    out_shape = jax.ShapeDtypeStruct((n, IN_PACK), F32)
    if pos is not None:
        out_specs = [out_specs, pl.BlockSpec((tm, D_MODEL), lambda t: (t, 0))]
        out_shape = [out_shape, jax.ShapeDtypeStruct((n, D_MODEL), F32)]
    return pl.pallas_call(
        functools.partial(_inproj_kernel, add_pos=pos is not None),
        grid=(n // tm,),
        in_specs=in_specs,
        out_specs=out_specs,
        out_shape=out_shape,
        compiler_params=_cparams(1),
        name="inproj_lat0" if pos is not None else "inproj",
    )(*args)


@functools.lru_cache(maxsize=None)
def _dft_tables(seq_len):
    n = 2 * seq_len - 1
    idx = np.arange(seq_len, dtype=np.int64)
    ang = (2.0 * np.pi / n) * ((idx[:, None] * idx[None, :]) % n).astype(np.float64)
    out = []
    for tab in (np.cos(ang), np.sin(ang)):
        hi = tab.astype(ml_dtypes.bfloat16)
        lo = (tab - hi.astype(np.float64)).astype(ml_dtypes.bfloat16)
        out += [hi, lo]
    return tuple(out)


@functools.lru_cache(maxsize=None)
def _hyena_consts(seq_len):
    t = np.linspace(0.0, 1.0, seq_len, dtype=np.float32)[:, None]
    w = np.float32(2.0 * math.pi / seq_len) * np.arange(seq_len, dtype=np.float32)[:, None]
    bands = np.linspace(1e-4, HY_BANDS - 1, HY_BANDS, dtype=np.float32)[None, :]
    feats = np.concatenate([t, np.cos(bands * w), -np.sin(bands * w)], axis=-1).astype(np.float32)
    feats_p = np.zeros((seq_len, 128), np.float32)
    feats_p[:, :HY_EMB] = feats
    max_decay = math.log(1e-2) / 0.3
    min_decay = math.log(1e-2) / 1.5
    deltas = np.abs(np.linspace(min_decay, max_decay, GW, dtype=np.float32))
    decay = np.exp(-t * deltas).astype(np.float32)
    return feats_p, np.concatenate([decay, decay], axis=1)


def _hyfilt_kernel(feat_ref, dec_ref, w1_ref, b1_ref, w2_ref, b2_ref, w3_ref, fr_ref,
                   ch_ref, cl_ref, sh_ref, sl_ref, hr_ref, hi_ref, *, seq_len):
    fr = fr_ref[...]
    h = jnp.sin(fr[0:1, :] * (_dot_hi(feat_ref[...], w1_ref[...]) + b1_ref[...]))
    h = jnp.sin(fr[1:2, :] * (_dot_hi(h, w2_ref[...]) + b2_ref[...]))
    h = _dot_hi(h, w3_ref[...]) * dec_ref[...]
    den = jnp.sum(jnp.abs(h[:, :GW]) + jnp.abs(h[:, GW:]), axis=0, keepdims=True)
    h = h / jnp.concatenate([den, den], axis=1)
    hc = _dot3_const(ch_ref[...], cl_ref[...], h)
    hs = _dot3_const(sh_ref[...], sl_ref[...], h)
    n = 2 * seq_len - 1
    row = lax.broadcasted_iota(jnp.int32, (seq_len, GW), 0)
    wgt = jnp.where(row == 0, 1.0 / n, 2.0 / n)
    hr_ref[...] = (hc[:, :GW] + hc[:, GW:]) * wgt
    hi_ref[...] = (hs[:, GW:] - hs[:, :GW]) * wgt


def _hyfilt_call(seq_len, hw):
    feats, decay = _hyena_consts(seq_len)
    tabs = _dft_tables(seq_len)
    full = lambda shape: pl.BlockSpec(shape, lambda l: (0,) * len(shape))
    lay = lambda shape: pl.BlockSpec((None,) + shape, lambda l: (l,) + (0,) * len(shape))
    return pl.pallas_call(
        functools.partial(_hyfilt_kernel, seq_len=seq_len),
        grid=(DEPTH,),
        in_specs=[full((seq_len, 128)), full((seq_len, 2 * GW)),
                  lay((128, 128)), lay((1, 128)), lay((128, 128)), lay((1, 128)), lay((128, 2 * GW)),
                  lay((2, 128))] + [full((seq_len, seq_len))] * 4,
        out_specs=[lay((seq_len, GW)), lay((seq_len, GW))],
        out_shape=[jax.ShapeDtypeStruct((DEPTH, seq_len, GW), F32)] * 2,
        compiler_params=_cparams(1),
        name=f"hyena_filter_{seq_len}",
    )(feats, decay, hw["w1"], hw["b1"], hw["w2"], hw["b2"], hw["w3"], hw["freq"], *tabs)


def _hyena_kernel(u_ref, cw_ref, bias_ref, hr_ref, hi_ref, ch_ref, cl_ref, sh_ref, sl_ref,
                  o_ref, pad_ref, *, seq_len, n_seq):
    for b in range(n_seq):
        uc = _dwconv(u_ref[b], cw_ref, pad_ref, seq_len, 3)
        x0 = uc[:, 0:GW]
        z = uc[:, 2 * GW:3 * GW] * uc[:, GW:2 * GW]
        za = _dot3_const(ch_ref[...], cl_ref[...], z)
        zb = _dot3_const(sh_ref[...], sl_ref[...], z)
        hr = hr_ref[...]
        hi = hi_ref[...]
        yr = za * hr + zb * hi
        yi = za * hi - zb * hr
        y = _dot3_const(ch_ref[...], cl_ref[...], yr) - _dot3_const(sh_ref[...], sl_ref[...], yi)
        o_ref[b] = x0 * (y + z * bias_ref[...])


def _hyena_call(proj3, conv_w, bias, hr, hi, seq_len, n_seq):
    bsz = proj3.shape[0]
    tabs = _dft_tables(seq_len)
    full = lambda shape: pl.BlockSpec(shape, lambda b: (0,) * len(shape))
    return pl.pallas_call(
        functools.partial(_hyena_kernel, seq_len=seq_len, n_seq=n_seq),
        grid=(bsz // n_seq,),
        in_specs=[pl.BlockSpec((n_seq, seq_len, 3 * GW), lambda b: (b, 0, COL_HY // (3 * GW))),
                  full((3, 3 * GW)), full((1, GW)), full((seq_len, GW)), full((seq_len, GW))]
                 + [full((seq_len, seq_len))] * 4,
        out_specs=pl.BlockSpec((n_seq, seq_len, GW), lambda b: (b, 0, 0)),
        out_shape=jax.ShapeDtypeStruct((bsz, seq_len, GW), F32),
        scratch_shapes=[pltpu.VMEM((seq_len + 16, 3 * GW), F32)],
        compiler_params=_cparams(1),
        name=f"hyena_{seq_len}",
    )(proj3, conv_w, bias.reshape(1, GW), hr, hi, *tabs)


def _tri_masks(q):
    row = lax.broadcasted_iota(jnp.int32, (q, q), 0)
    col = lax.broadcasted_iota(jnp.int32, (q, q), 1)
    return col <= row, col >= row


def _ssd_kernel(*refs, seq_len, has_s0, emit_state):
    it = iter(refs)
    zx_ref, sm_ref, conv_ref, dtb_r_ref, dtb_c_ref, an_r_ref, an_c_ref, dvec_ref, gn_ref = (
        next(it) for _ in range(9))
    s0_ref = next(it) if has_s0 else None
    o_ref = next(it)
    so_ref = next(it) if emit_state else None
    pad_ref, xs_ref, cm_ref, ac_ref, bt_ref, ar_ref, dtr_ref, y_ref, st_ref, upd_ref, ecs_ref = it
    q = SSD_Q
    nc = seq_len // q

    xbc = _silu(_dwconv(zx_ref[:, GW:3 * GW], conv_ref, pad_ref, seq_len, 4))
    xs_ref[...] = xbc[:, 0:GW]
    cm_ref[...] = xbc[:, GW + 128:GW + 256].astype(BF16)
    y_ref[...] = xbc[:, 0:GW] * dvec_ref[...]
    bm_t = xbc[:, GW:GW + 128].T.astype(BF16)
    sm = sm_ref[...]
    ac_ref[...] = _softplus(sm + dtb_r_ref[...]) * an_r_ref[...]
    sm_t = sm.T
    dtr = _softplus(sm_t[SM_DT:SM_DT + 8, :] + dtb_c_ref[...])
    ar = dtr * an_c_ref[...]
    for c in range(nc):
        bt_ref[c] = bm_t[:, c * q:(c + 1) * q]
        ar_ref[c] = ar[:, c * q:(c + 1) * q]
        dtr_ref[c] = dtr[:, c * q:(c + 1) * q]
    if has_s0:
        st_ref[...] = s0_ref[...]
    else:
        st_ref[...] = jnp.zeros(st_ref.shape, F32)

    lower, upper = _tri_masks(q)
    lo_f = lower.astype(F32).astype(BF16)
    up_f = upper.astype(F32).astype(BF16)
    lane128 = _lane_iota((q, 128))
    lane256 = _lane_iota((q, GW))
    st_row = lax.broadcasted_iota(jnp.int32, (128, GW), 0)
    st_col = lax.broadcasted_iota(jnp.int32, (128, GW), 1)
    st_mask = (st_row // HD) == (st_col // 128)
    dirs = (0, 1)
    tri_c = (lo_f, up_f)
    tri_r = (up_f, lo_f)
    masks = (lower, upper)
    last = (q - 1, 0)
    spread = [((st_row == d * NH + st_col // HD)).astype(F32).astype(BF16) for d in dirs]

    def chunk_body(ci, carry):
        rows = pl.ds(pl.multiple_of(ci * q, q), q)
        xs_b = xs_ref[rows, :].astype(BF16)
        c_c = cm_ref[rows, :]
        b_t = bt_ref[ci]
        a_c = ac_ref[rows, :]
        a_r = ar_ref[ci]
        dt_r = dtr_ref[ci]
        stack = jnp.concatenate(
            [jnp.where((lane256 // HD) == h, xs_b, jnp.zeros_like(xs_b)) for h in range(NH)], axis=0)
        c_g = jnp.concatenate([jnp.where(lane128 < HD, c_c, jnp.zeros_like(c_c)),
                               jnp.where(lane128 >= HD, c_c, jnp.zeros_like(c_c))], axis=0)
        g_mat = _dot(c_g, b_t)
        cum_c = [_cumsum_rows(tri_c[d], a_c) for d in dirs]
        cum_r = [_cumsum_cols(a_r, tri_r[d]) for d in dirs]
        w_cat, bw_cat = [], []
        for d in dirs:
            w_parts, bw_parts = [], []
            for h in range(NH):
                j = d * NH + h
                g = h // 2
                diff = cum_c[d][:, j:j + 1] - cum_r[d][j:j + 1, :]
                m = jnp.where(masks[d], jnp.exp(jnp.minimum(diff, 0.0)), 0.0)
                w_parts.append((g_mat[g * q:(g + 1) * q, :] * m * dt_r[j:j + 1, :]).astype(BF16))
                to_end = jnp.exp(cum_r[d][j:j + 1, last[d]:last[d] + 1] - cum_r[d][j:j + 1, :])
                bw_parts.append((b_t.astype(F32) * (dt_r[j:j + 1, :] * to_end)).astype(BF16))
            w_cat.append(jnp.concatenate(w_parts, axis=1))
            bw_cat.append(jnp.concatenate(bw_parts, axis=1))
        y_diag = [_dot(w_cat[d], stack) for d in dirs]
        upd = [_dot(bw_cat[d], stack) for d in dirs]
        ecs = []
        for d in dirs:
            e_hi, e_lo = _split_bf16(jnp.exp(cum_c[d]))
            ecs.append(_dot(e_hi, spread[d]) + _dot(e_lo, spread[d]))
        y_ref[rows, :] += y_diag[0] + y_diag[1]
        for d in dirs:
            upd_ref[d, ci] = jnp.where(st_mask, upd[d], 0.0)
            ecs_ref[d, ci] = ecs[d]
        return carry

    lax.fori_loop(0, nc, chunk_body, 0)

    def state_body(i, carry):
        cis = (i, nc - 1 - i)
        rows = [pl.ds(pl.multiple_of(ci * q, q), q) for ci in cis]
        ss = [st_ref[d] for d in dirs]
        y_off = [_dot(cm_ref[rows[d], :], ss[d].astype(BF16)) for d in dirs]
        for d in dirs:
            ecs = ecs_ref[d, cis[d]]
            y_ref[rows[d], :] += y_off[d] * ecs
            st_ref[d] = ss[d] * ecs[last[d]:last[d] + 1, :] + upd_ref[d, cis[d]]
        return carry

    lax.fori_loop(0, nc, state_body, 0)

    y = y_ref[...] * _silu(zx_ref[:, 0:GW])
    y = y * lax.rsqrt(jnp.mean(y * y, axis=-1, keepdims=True) + EPS) * gn_ref[...]
    o_ref[...] = y
    if emit_state:
        for d in range(2):
            st_t = st_ref[d].T
            for h in range(NH):
                blk = st_t[h * HD:(h + 1) * HD, :]
                if h // 2 == 1:
                    blk = pltpu.roll(blk, HD, axis=1)
                so_ref[d, h] = blk[:, 0:HD]


def _ssd_call(proj3, p, s0, seq_len, emit_state):
    bsz = proj3.shape[0]
    nc = seq_len // SSD_Q
    full = lambda shape: pl.BlockSpec(shape, lambda b: (0,) * len(shape))
    in_specs = [pl.BlockSpec((None, seq_len, 3 * GW), lambda b: (b, 0, COL_SSD // (3 * GW))),
                pl.BlockSpec((None, seq_len, 128), lambda b: (b, 0, COL_SMALL // 128)),
                full((4, 2 * GW)), full((1, 128)), full((8, 1)), full((1, 128)), full((8, 1)),
                full((1, GW)), full((1, GW))]
    args = [proj3, proj3, p["conv"], p["dtb_row"], p["dtb_col"], p["an_row"], p["an_col"], p["dvec"], p["gn"]]
    if s0 is not None:
        in_specs.append(pl.BlockSpec((None, 2, 128, GW), lambda b: (b, 0, 0, 0)))
        args.append(s0)
    out_specs = [pl.BlockSpec((None, seq_len, GW), lambda b: (b, 0, 0))]
    out_shape = [jax.ShapeDtypeStruct((bsz, seq_len, GW), F32)]
    if emit_state:
        out_specs.append(pl.BlockSpec((None, 2, NH, HD, HD), lambda b: (b, 0, 0, 0, 0)))
        out_shape.append(jax.ShapeDtypeStruct((bsz, 2, NH, HD, HD), F32))
    scratch = [pltpu.VMEM((seq_len + 16, 2 * GW), F32),
               pltpu.VMEM((seq_len, GW), F32),
               pltpu.VMEM((seq_len, 128), BF16),
               pltpu.VMEM((seq_len, 128), F32),
               pltpu.VMEM((nc, 128, SSD_Q), BF16),
               pltpu.VMEM((nc, 8, SSD_Q), F32),
               pltpu.VMEM((nc, 8, SSD_Q), F32),
               pltpu.VMEM((seq_len, GW), F32),
               pltpu.VMEM((2, 128, GW), F32),
               pltpu.VMEM((2, nc, 128, GW), F32),
               pltpu.VMEM((2, nc, SSD_Q, GW), F32)]
    return pl.pallas_call(
        functools.partial(_ssd_kernel, seq_len=seq_len, has_s0=s0 is not None, emit_state=emit_state),
        grid=(bsz,),
        in_specs=in_specs,
        out_specs=out_specs,
        out_shape=out_shape,
        scratch_shapes=scratch,
        compiler_params=_cparams(1),
        name=f"ssd_{seq_len}",
    )(*args)


def _lru_kernel(*refs, seq_len, has_s0, emit_state):
    it = iter(refs)
    u_ref, conv_ref, w_ref, bias_ref, sp_ref = (next(it) for _ in range(5))
    s0_ref = next(it) if has_s0 else None
    o_ref = next(it)
    so_ref = next(it) if emit_state else None
    pad_ref, af_ref, bf_ref, ab_ref, bb_ref = it
    n = seq_len

    xc = _dwconv(u_ref[:, 0:GW], conv_ref, pad_ref, n, 4)
    ri = jax.nn.sigmoid(_dot(xc.astype(BF16), w_ref[...]) + bias_ref[...])
    log_a = -LRU_C * ri[:, 0:2 * GW] * sp_ref[...]
    a = jnp.exp(log_a)
    bt = jnp.sqrt(jnp.maximum(1.0 - a * a, 0.0)) * ri[:, 2 * GW:4 * GW] * jnp.concatenate([xc, xc], axis=1)

    ones8 = jnp.ones((8, GW), F32)
    zeros8 = jnp.zeros((8, GW), F32)
    for a_ref_, b_ref_, lo in ((af_ref, bf_ref, 0), (ab_ref, bb_ref, GW)):
        a_ref_[0:8, :] = ones8
        a_ref_[8 + n:16 + n, :] = ones8
        b_ref_[0:8, :] = zeros8
        b_ref_[8 + n:16 + n, :] = zeros8
        a_ref_[8:8 + n, :] = a[:, lo:lo + GW]
        b_ref_[8:8 + n, :] = bt[:, lo:lo + GW]
    if has_s0:
        s0 = s0_ref[...]
        bf_ref[8:9, :] = bf_ref[8:9, :] + af_ref[8:9, :] * s0[0:1, :]
        bb_ref[7 + n:8 + n, :] = bb_ref[7 + n:8 + n, :] + ab_ref[7 + n:8 + n, :] * s0[1:2, :]

    d = 1
    while d < n:
        if d < 8:
            a_cur = af_ref[8:8 + n, :]
            af_ref[8:8 + n, :] = a_cur * af_ref[8 - d:8 - d + n, :]
            bf_ref[8:8 + n, :] = a_cur * bf_ref[8 - d:8 - d + n, :] + bf_ref[8:8 + n, :]
            a_cur = ab_ref[8:8 + n, :]
            ab_ref[8:8 + n, :] = a_cur * ab_ref[8 + d:8 + d + n, :]
            bb_ref[8:8 + n, :] = a_cur * bb_ref[8 + d:8 + d + n, :] + bb_ref[8:8 + n, :]
        else:
            m = n - d
            a_cur = af_ref[8 + d:8 + n, :]
            a_new = a_cur * af_ref[8:8 + m, :]
            b_new = a_cur * bf_ref[8:8 + m, :] + bf_ref[8 + d:8 + n, :]
            af_ref[8 + d:8 + n, :] = a_new
            bf_ref[8 + d:8 + n, :] = b_new
            a_cur = ab_ref[8:8 + m, :]
            a_new = a_cur * ab_ref[8 + d:8 + n, :]
            b_new = a_cur * bb_ref[8 + d:8 + n, :] + bb_ref[8:8 + m, :]
            ab_ref[8:8 + m, :] = a_new
            bb_ref[8:8 + m, :] = b_new
        d *= 2

    y = (bf_ref[8:8 + n, :] + bb_ref[8:8 + n, :]) * _gelu_tanh(u_ref[:, GW:2 * GW])
    o_ref[...] = y
    if emit_state:
        so_ref[0:1, :] = bf_ref[7 + n:8 + n, :]
        so_ref[1:2, :] = bb_ref[8:9, :]


def _lru_call(proj3, p, s0, seq_len, emit_state):
    bsz = proj3.shape[0]
    full = lambda shape: pl.BlockSpec(shape, lambda b: (0,) * len(shape))
    in_specs = [pl.BlockSpec((None, seq_len, 2 * GW), lambda b: (b, 0, COL_LRU // (2 * GW))),
                full((4, GW)), full((GW, 4 * GW)), full((1, 4 * GW)), full((1, 2 * GW))]
    args = [proj3, p["conv"], p["w"], p["bias"], p["sp"]]
    if s0 is not None:
        in_specs.append(pl.BlockSpec((None, 2, GW), lambda b: (b, 0, 0)))
        args.append(s0)
    out_specs = [pl.BlockSpec((None, seq_len, GW), lambda b: (b, 0, 0))]
    out_shape = [jax.ShapeDtypeStruct((bsz, seq_len, GW), F32)]
    if emit_state:
        out_specs.append(pl.BlockSpec((None, 2, GW), lambda b: (b, 0, 0)))
        out_shape.append(jax.ShapeDtypeStruct((bsz, 2, GW), F32))
    scratch = [pltpu.VMEM((seq_len + 16, GW), F32)] * 5
    return pl.pallas_call(
        functools.partial(_lru_kernel, seq_len=seq_len, has_s0=s0 is not None, emit_state=emit_state),
        grid=(bsz,),
        in_specs=in_specs,
        out_specs=out_specs,
        out_shape=out_shape,
        scratch_shapes=scratch,
        compiler_params=_cparams(1),
        name=f"rglru_{seq_len}",
    )(*args)


TRI_BASE = 8


def _unit_tri_inverse(a_mats, lower_flags, eye, same_blk):
    c = a_mats[0].shape[0]
    ds = [jnp.where(same_blk[TRI_BASE], a, 0.0) for a in a_mats]
    d_bs = [d.astype(BF16) for d in ds]
    ps = [eye - d for d in ds]
    x_bs = [_dot(d_b, d_b).astype(BF16) for d_b in d_bs]
    rs = [_dot(x_b, jnp.concatenate([p.astype(BF16), x_b], axis=1)) for x_b, p in zip(x_bs, ps)]
    ps = [p + r[:, 0:c] for p, r in zip(ps, rs)]
    ts = [p + _dot(r[:, c:2 * c].astype(BF16), p.astype(BF16)) for p, r in zip(ps, rs)]
    size = 2 * TRI_BASE
    while size <= c:
        half = size // 2
        off = same_blk[size] & jnp.logical_not(same_blk[half])
        e_bs = [jnp.where(off, a, 0.0).astype(BF16) for a in a_mats]
        t_bs = [t.astype(BF16) for t in ts]
        starts = [[blk * size + (half if low else 0) for blk in range(c // size)] for low in lower_flags]
        rows = [jnp.concatenate([t[r0:r0 + half, :] for r0 in st], axis=0).astype(BF16)
                for t, st in zip(ts, starts)]
        us = [_dot(r, e_b).astype(BF16) for r, e_b in zip(rows, e_bs)]
        upds = [_dot(u, t_b) for u, t_b in zip(us, t_bs)]
        new_ts = []
        for t, upd, st, low in zip(ts, upds, starts, lower_flags):
            pieces = []
            for n, r0 in enumerate(st):
                changed = t[r0:r0 + half, :] - upd[n * half:(n + 1) * half, :]
                kept = t[r0 - half:r0, :] if low else t[r0 + half:r0 + size, :]
                pieces += [kept, changed] if low else [changed, kept]
            new_ts.append(jnp.concatenate(pieces, axis=0))
        ts = new_ts
        size *= 2
    return ts


def _gdn_kernel(*refs, seq_len, has_s0, emit_state):
    it = iter(refs)
    u_ref, sm_ref, conv_ref, dtb_r_ref, dtb_c_ref, an_r_ref, an_c_ref, gn_ref, ones_ref = (
        next(it) for _ in range(9))
    s0_ref = next(it) if has_s0 else None
    o_ref = next(it)
    so_ref = next(it) if emit_state else None
    (pad_ref, q_ref, k_ref, v_ref, be_ref, gc_ref, gr_ref, oa_ref, st_ref,
     us_ref, wq_ref, at_ref, kd_ref, dr_ref) = it
    c = GDN_C
    nc = seq_len // c
    ones_bd = ones_ref[...]

    qkv = _silu(_dwconv(u_ref[:, 0:3 * GW], conv_ref, pad_ref, seq_len, 4))
    qq = qkv[:, 0:GW]
    kk = qkv[:, GW:2 * GW]
    q_ref[...] = qq * lax.rsqrt(_seg_sum(qq * qq, ones_bd) + EPS) * (HD ** -0.5)
    k_ref[...] = kk * lax.rsqrt(_seg_sum(kk * kk, ones_bd) + EPS)
    v_ref[...] = qkv[:, 2 * GW:3 * GW]
    sm = sm_ref[...]
    be_ref[...] = jax.nn.sigmoid(sm)
    gc_ref[...] = an_r_ref[...] * _softplus(sm + dtb_r_ref[...])
    sm_t = sm.T
    g_r = an_c_ref[...] * _softplus(sm_t[SM_A:SM_A + 8, :] + dtb_c_ref[...])
    for i in range(nc):
        gr_ref[i] = g_r[:, i * c:(i + 1) * c]
    oa_ref[...] = jnp.zeros(oa_ref.shape, F32)
    if has_s0:
        st_ref[...] = s0_ref[...]
    else:
        st_ref[...] = jnp.zeros(st_ref.shape, F32)

    lower, upper = _tri_masks(c)
    lo_f = lower.astype(F32).astype(BF16)
    up_f = upper.astype(F32).astype(BF16)
    row_i =lax.broadcasted_iota(jnp.int32, (c, c), 0)
    col_i = lax.broadcasted_iota(jnp.int32, (c, c), 1)
    eye = (row_i == col_i).astype(F32)
    same_blk = {}
    size = TRI_BASE
    while size <= c:
        same_blk[size] = (row_i // size) == (col_i // size)
        size *= 2
    lane256 = _lane_iota((c, GW))
    lane512 = _lane_iota((c, 2 * GW))
    bd_row = lax.broadcasted_iota(jnp.int32, (GW, GW), 0)
    bd_col = lax.broadcasted_iota(jnp.int32, (GW, GW), 1)
    bd_mask = (bd_row // HD) == (bd_col // HD)
    kt_row = lax.broadcasted_iota(jnp.int32, (GW, c), 0)

    def stack_heads(x_b, lane):
        zero = jnp.zeros_like(x_b)
        return jnp.concatenate([jnp.where(((lane % GW) // HD) == h, x_b, zero) for h in range(NH)], axis=0)

    dirs = (0, 1)
    tri_c = (lo_f, up_f)
    tri_r = (up_f, lo_f)
    incl = (lower, upper)
    strict = (col_i < row_i, col_i > row_i)
    last = (c - 1, 0)

    chunks_per_step = 2

    def solve_body(step, carry):
        cks = [step * chunks_per_step + n for n in range(chunks_per_step)]
        units = [(n, d) for n in range(chunks_per_step) for d in dirs]
        rows = [pl.ds(pl.multiple_of(ci * c, c), c) for ci in cks]
        q_c = [q_ref[r, :] for r in rows]
        k_c = [k_ref[r, :] for r in rows]
        v_c = [v_ref[r, :] for r in rows]
        be = [be_ref[r, :] for r in rows]
        g_c = [gc_ref[r, :] for r in rows]
        g_r = [gr_ref[ci] for ci in cks]
        cum_c = {(n, d): _cumsum_rows(tri_c[d], g_c[n]) for n, d in units}
        cum_r = {(n, d): _cumsum_cols(g_r[n], tri_r[d]) for n, d in units}
        kq = []
        for n in range(chunks_per_step):
            k_t = k_c[n].T.astype(BF16)
            k_t_heads = jnp.concatenate(
                [jnp.where((kt_row // HD) == h, k_t, jnp.zeros_like(k_t)) for h in range(NH)], axis=1)
            kq.append(_dot(jnp.concatenate([k_c[n], q_c[n]], axis=0).astype(BF16), k_t_heads))
        a_mats, attn, flags = [], {}, []
        for n, d in units:
            for h in range(NH):
                j = d * NH + h
                diff = cum_c[n, d][:, SM_A + j:SM_A + j + 1] - cum_r[n, d][j:j + 1, :]
                dec = jnp.where(incl[d], jnp.exp(jnp.minimum(diff, 0.0)), 0.0)
                a_mats.append(jnp.where(
                    strict[d], kq[n][0:c, h * c:(h + 1) * c] * be[n][:, SM_B + j:SM_B + j + 1] * dec, 0.0))
                attn[n, d, h] = (kq[n][c:2 * c, h * c:(h + 1) * c] * dec).astype(BF16)
                flags.append(d == 0)
        t_mats = _unit_tri_inverse(a_mats, flags, eye, same_blk)
        a_split = [_split_bf16(a) for a in a_mats]
        t_cat, ah_cat, al_cat = {}, {}, {}
        for idx, (n, d) in enumerate(units):
            sl = slice(idx * NH, (idx + 1) * NH)
            t_cat[n, d] = jnp.concatenate([t.astype(BF16) for t in t_mats[sl]], axis=1)
            ah_cat[n, d] = jnp.concatenate([hi for hi, _ in a_split[sl]], axis=1)
            al_cat[n, d] = jnp.concatenate([lo for _, lo in a_split[sl]], axis=1)
        cols = {(n, d): [cum_c[n, d][:, SM_A + d * NH + h:SM_A + d * NH + h + 1] for h in range(NH)]
                for n, d in units}
        lasts = {(n, d): [cum_c[n, d][last[d]:last[d] + 1, SM_A + d * NH + h:SM_A + d * NH + h + 1]
                          for h in range(NH)] for n, d in units}
        b_l = {(n, d): _head_lanes([be[n][:, SM_B + d * NH + h:SM_B + d * NH + h + 1] for h in range(NH)], c)
               for n, d in units}
        e_l = {(n, d): _head_lanes([jnp.exp(cc) for cc in cols[n, d]], c) for n, d in units}
        rhs = {(n, d): jnp.concatenate([v_c[n] * b_l[n, d], k_c[n] * b_l[n, d] * e_l[n, d]], axis=1)
               for n, d in units}
        sol_b = {u: _dot(t_cat[u], stack_heads(rhs[u].astype(BF16), lane512)).astype(BF16)
                 for u in units}
        st_sol = {u: stack_heads(sol_b[u], lane512) for u in units}
        a_sol = {u: _dot(ah_cat[u], st_sol[u]) + _dot(al_cat[u], st_sol[u]) for u in units}
        uw = {}
        for u in units:
            sol = sol_b[u].astype(F32)
            uw[u] = sol + _dot(t_cat[u], stack_heads((rhs[u] - sol - a_sol[u]).astype(BF16), lane512))
        for n, d in units:
            ci = cks[n]
            us_ref[d, ci] = uw[n, d][:, 0:GW]
            wq_ref[d, ci, 0:c, :] = uw[n, d][:, GW:2 * GW].astype(BF16)
            wq_ref[d, ci, c:2 * c, :] = (q_c[n] * e_l[n, d]).astype(BF16)
            at_ref[d, ci] = jnp.concatenate([attn[n, d, h] for h in range(NH)], axis=1)
            k_dec = k_c[n] * _head_lanes([jnp.exp(lasts[n, d][h] - cols[n, d][h]) for h in range(NH)], c)
            kd_ref[d, ci] = k_dec.T.astype(BF16)
            dr_ref[d, ci] = _head_lanes([jnp.exp(lv) for lv in lasts[n, d]], 1)
        return carry

    lax.fori_loop(0, nc // chunks_per_step, solve_body, 0)

    def state_body(i, carry):
        cis = (i, nc - 1 - i)
        ss = [st_ref[d] for d in dirs]
        r2 = [_dot(wq_ref[d, cis[d]], ss[d].astype(BF16)) for d in dirs]
        v_new_b = [(us_ref[d, cis[d]] - r2[d][0:c, :]).astype(BF16) for d in dirs]
        upd = [_dot(kd_ref[d, cis[d]], v_new_b[d]) for d in dirs]
        o_c = [r2[d][c:2 * c, :] + _dot(at_ref[d, cis[d]], stack_heads(v_new_b[d], lane256)) for d in dirs]
        for d in dirs:
            st_ref[d] = ss[d] * dr_ref[d, cis[d]] + jnp.where(bd_mask, upd[d], 0.0)
            rows = pl.ds(pl.multiple_of(cis[d] * c, c), c)
            oa_ref[rows, :] += o_c[d]
        return carry

    lax.fori_loop(0, nc, state_body, 0)

    o = oa_ref[...]
    o = o * lax.rsqrt(_seg_sum(o * o, ones_bd) * (1.0 / HD) + EPS) * gn_ref[...]
    o_ref[...] = o * _silu(u_ref[:, 3 * GW:4 * GW])
    if emit_state:
        for d in range(2):
            for h in range(NH):
                blk = st_ref[d, h * HD:(h + 1) * HD, (h // 2) * 128:(h // 2 + 1) * 128]
                if h % 2 == 1:
                    blk = pltpu.roll(blk, HD, axis=1)
                so_ref[d, h] = blk[:, 0:HD]


def _gdn_call(proj3, p, s0, seq_len, emit_state):
    bsz = proj3.shape[0]
    nc = seq_len // GDN_C
    full = lambda shape: pl.BlockSpec(shape, lambda b: (0,) * len(shape))
    in_specs = [pl.BlockSpec((None, seq_len, 4 * GW), lambda b: (b, 0, COL_GDN // (4 * GW))),
                pl.BlockSpec((None, seq_len, 128), lambda b: (b, 0, COL_SMALL // 128)),
                full((4, 3 * GW)), full((1, 128)), full((8, 1)), full((1, 128)), full((8, 1)),
                full((1, GW)), full((GW, GW))]
    args = [proj3, proj3, p["conv"], p["dtb_row"], p["dtb_col"], p["an_row"], p["an_col"], p["gn"], p["ones"]]
    if s0 is not None:
        in_specs.append(pl.BlockSpec((None, 2, GW, GW), lambda b: (b, 0, 0, 0)))
        args.append(s0)
    out_specs = [pl.BlockSpec((None, seq_len, GW), lambda b: (b, 0, 0))]
    out_shape = [jax.ShapeDtypeStruct((bsz, seq_len, GW), F32)]
    if emit_state:
        out_specs.append(pl.BlockSpec((None, 2, NH, HD, HD), lambda b: (b, 0, 0, 0, 0)))
        out_shape.append(jax.ShapeDtypeStruct((bsz, 2, NH, HD, HD), F32))
    scratch = [pltpu.VMEM((seq_len + 16, 3 * GW), F32),
               pltpu.VMEM((seq_len, GW), F32),
               pltpu.VMEM((seq_len, GW), F32),
               pltpu.VMEM((seq_len, GW), F32),
               pltpu.VMEM((seq_len, 128), F32),
               pltpu.VMEM((seq_len, 128), F32),
               pltpu.VMEM((nc, 8, GDN_C), F32),
               pltpu.VMEM((seq_len, GW), F32),
               pltpu.VMEM((2, GW, GW), F32),
               pltpu.VMEM((2, nc, GDN_C, GW), F32),
               pltpu.VMEM((2, nc, 2 * GDN_C, GW), BF16),
               pltpu.VMEM((2, nc, GDN_C, NH * GDN_C), BF16),
               pltpu.VMEM((2, nc, GW, GDN_C), BF16),
               pltpu.VMEM((2, nc, 1, GW), F32)]
    return pl.pallas_call(
        functools.partial(_gdn_kernel, seq_len=seq_len, has_s0=s0 is not None, emit_state=emit_state),
        grid=(bsz,),
        in_specs=in_specs,
        out_specs=out_specs,
        out_shape=out_shape,
        scratch_shapes=scratch,
        compiler_params=_cparams(1),
        name=f"gdn_{seq_len}",
    )(*args)


def _ffn_kernel(x_ref, ohy_ref, ossd_ref, olru_ref, ogdn_ref, mod_ref, g_ref, wo_ref, wg_ref, wu_ref,
                wd_ref, gfin_ref, o_ref, *, final_norm):
    mod = mod_ref[...]
    ga_m = mod[:, 2 * D_MODEL:3 * D_MODEL]
    sh_f = mod[:, 3 * D_MODEL:4 * D_MODEL]
    sc_f = mod[:, 4 * D_MODEL:5 * D_MODEL]
    ga_f = mod[:, 5 * D_MODEL:6 * D_MODEL]
    mo = None
    for i, r in enumerate((ohy_ref, ossd_ref, olru_ref, ogdn_ref)):
        part = _dot(r[...].astype(BF16), wo_ref[i * GW:(i + 1) * GW, :])
        mo = part if mo is None else mo + part
    x = x_ref[...] + ga_m * mo
    h = x * lax.rsqrt(jnp.mean(x * x, axis=-1, keepdims=True) + EPS) * g_ref[...]
    h = (h * (1.0 + sc_f) + sh_f).astype(BF16)
    half = D_FF // 2
    ff = None
    for i in range(2):
        gate = _dot(h, wg_ref[:, i * half:(i + 1) * half])
        up = _dot(h, wu_ref[:, i * half:(i + 1) * half])
        part = _dot((_silu(gate) * up).astype(BF16), wd_ref[i * half:(i + 1) * half, :])
        ff = part if ff is None else ff + part
    x = x + ga_f * ff
    if final_norm:
        x = x * lax.rsqrt(jnp.mean(x * x, axis=-1, keepdims=True) + EPS) * gfin_ref[...]
    o_ref[...] = x


def _ffn_call(x2d, outs, mods_l, g, weights, layer, g_final, seq_len, is_lat, final_norm):
    n = x2d.shape[0]
    tm = ROW_TILE
    per_seq = seq_len // tm
    mod_map = (lambda t: (1 + t // per_seq, 0, 0)) if is_lat else (lambda t: (0, 0, 0))
    tile = lambda w: pl.BlockSpec((tm, w), lambda t: (t, 0))
    res = lambda shape: pl.BlockSpec(shape, lambda t: (0, 0), pipeline_mode=pl.Buffered(1))
    wres = lambda shape: pl.BlockSpec((None,) + shape, lambda t: (layer, 0, 0), pipeline_mode=pl.Buffered(1))
    return pl.pallas_call(
        functools.partial(_ffn_kernel, final_norm=final_norm),
        grid=(n // tm,),
        in_specs=[tile(D_MODEL), tile(GW), tile(GW), tile(GW), tile(GW),
                  pl.BlockSpec((None, 1, 6 * D_MODEL), mod_map),
                  res((1, D_MODEL)), wres((D_MODEL, D_MODEL)), wres((D_MODEL, D_FF)), wres((D_MODEL, D_FF)),
                  wres((D_FF, D_MODEL)), res((1, D_MODEL))],
        out_specs=tile(D_MODEL),
        out_shape=jax.ShapeDtypeStruct((n, D_MODEL), F32),
        compiler_params=_cparams(1),
        name="outproj_ffn",
    )(x2d, *outs, mods_l, g.reshape(1, D_MODEL), *weights, g_final.reshape(1, D_MODEL))


def _grid_pos_embed(n_tokens):
    rows = n_tokens // GRID_W
    rr, cc = np.meshgrid(np.arange(rows, dtype=np.float32), np.arange(GRID_W, dtype=np.float32), indexing="ij")
    quarter = D_MODEL // 4
    omega = (1.0 / (np.float32(POS_BASE) ** (np.arange(quarter, dtype=np.float32) / quarter))).astype(np.float32)

    def enc(pos):
        ang = pos.reshape(-1)[:, None] * omega[None, :]
        return np.concatenate([np.sin(ang), np.cos(ang)], axis=-1)

    return jnp.asarray(np.concatenate([enc(rr), enc(cc)], axis=-1).astype(np.float32))


def _pad_small(vec8, offset):
    row = jnp.zeros((1, 128), F32).at[0, offset:offset + 8].set(vec8)
    return row, vec8.reshape(8, 1)


def _pack_w_in(w):
    hy_in, ssd_in, lru_in = 768, 776, 512
    o_ssd = hy_in
    o_lru = o_ssd + ssd_in
    o_gdn = o_lru + lru_in
    pieces = [w[..., 0:o_ssd + 768],
              w[..., o_lru:o_gdn + 1024],
              w[..., o_ssd + 768:o_ssd + 776],
              w[..., o_gdn + 1024:o_gdn + 1040],
              jnp.zeros(w.shape[:-1] + (IN_PACK - 3096,), w.dtype)]
    return jnp.concatenate(pieces, axis=-1).astype(BF16)


def _ssd_state_in(s):
    st = jnp.transpose(s, (0, 1, 4, 2, 3))
    grp = (jnp.arange(NH) // 2)[None, :] == jnp.arange(2)[:, None]
    full = st[:, :, None] * grp[None, None, :, None, :, None].astype(s.dtype)
    return full.reshape(s.shape[0], 2, 128, GW)


def _gdn_state_in(s):
    bsz = s.shape[0]
    eye = jnp.eye(NH, dtype=s.dtype)
    full = s[:, :, :, :, None, :] * eye[None, None, :, None, :, None]
    return full.reshape(bsz, 2, GW, GW)


def kernel(x_prompt, x_sample, state_lru, state_ssd, state_gdn, c, c_ctx, w_mod, b_mod, g_mix, g_ffn, g_final, w_in, w_out, hy_conv, hy_w1, hy_b1, hy_w2, hy_b2, hy_w3, hy_freq, hy_bias, ssd_conv, ssd_dt_bias, ssd_a_log, ssd_d, ssd_norm, lru_conv, lru_w_r, lru_b_r, lru_w_i, lru_b_i, lru_lambda, gdn_conv, gdn_dt_bias, gdn_a_log, gdn_norm, w_gate, w_up, w_down):
    b_ctx, l_ctx, _ = x_prompt.shape
    b_lat, l_lat, _ = x_sample.shape

    cvec = jnp.zeros((16, D_MODEL), F32).at[0].set(c_ctx).at[1:1 + b_lat].set(c)
    mods = _mod_call(cvec, w_mod, b_mod).reshape(DEPTH, 16, 1, 6 * D_MODEL)

    hw = {"w1": jnp.zeros((DEPTH, 128, 128), F32).at[:, :HY_EMB, :HY_HIDDEN].set(hy_w1),
          "b1": jnp.zeros((DEPTH, 1, 128), F32).at[:, 0, :HY_HIDDEN].set(hy_b1),
          "w2": jnp.zeros((DEPTH, 128, 128), F32).at[:, :HY_HIDDEN, :HY_HIDDEN].set(hy_w2),
          "b2": jnp.zeros((DEPTH, 1, 128), F32).at[:, 0, :HY_HIDDEN].set(hy_b2),
          "w3": jnp.zeros((DEPTH, 128, 2 * GW), F32).at[:, :HY_HIDDEN, :].set(hy_w3),
          "freq": jnp.zeros((DEPTH, 2, 128), F32).at[:, :, :HY_HIDDEN].set(hy_freq)}
    filt = {l_ctx: _hyfilt_call(l_ctx, hw), l_lat: _hyfilt_call(l_lat, hw)}

    ones_bd = jnp.asarray(np.kron(np.eye(NH, dtype=np.float32), np.ones((HD, HD), np.float32))).astype(BF16)
    pos = _grid_pos_embed(l_lat)

    x_ctx = x_prompt.reshape(b_ctx * l_ctx, D_MODEL)
    x_lat = x_sample.reshape(b_lat * l_lat, D_MODEL)
    w_pack = _pack_w_in(w_in)
    ffn_w = tuple(w.astype(BF16) for w in (w_out, w_gate, w_up, w_down))
    lru_out, ssd_out, gdn_out = [], [], []
    for l in range(DEPTH):
        ssd_dtb_r, ssd_dtb_c = _pad_small(ssd_dt_bias[l].reshape(8), SM_DT)
        ssd_an_r, ssd_an_c = _pad_small(-jnp.exp(ssd_a_log[l].reshape(8)), SM_DT)
        ssd_p = {"conv": ssd_conv[l], "dtb_row": ssd_dtb_r, "dtb_col": ssd_dtb_c, "an_row": ssd_an_r,
                 "an_col": ssd_an_c, "dvec": jnp.repeat(ssd_d[l], HD).reshape(1, GW),
                 "gn": ssd_norm[l].reshape(1, GW)}
        eye_h = jnp.eye(NH, dtype=F32)
        blockdiag = lambda w: (w[:, :, :, None, :] * eye_h[None, :, None, :, None]).reshape(2, GW, GW)
        w_r, w_i = blockdiag(lru_w_r[l]), blockdiag(lru_w_i[l])
        lru_p = {"conv": lru_conv[l],
                 "w": jnp.concatenate([w_r[0], w_r[1], w_i[0], w_i[1]], axis=1).astype(BF16),
                 "bias": jnp.concatenate([lru_b_r[l].reshape(1, 2 * GW), lru_b_i[l].reshape(1, 2 * GW)], axis=1),
                 "sp": jax.nn.softplus(-lru_lambda[l]).reshape(1, 2 * GW)}
        gdn_dtb_r, gdn_dtb_c = _pad_small(gdn_dt_bias[l].reshape(8), SM_A)
        gdn_an_r, gdn_an_c = _pad_small(-jnp.exp(gdn_a_log[l].reshape(8)), SM_A)
        gdn_p = {"conv": gdn_conv[l], "dtb_row": gdn_dtb_r, "dtb_col": gdn_dtb_c, "an_row": gdn_an_r,
                 "an_col": gdn_an_c, "gn": jnp.tile(gdn_norm[l], NH).reshape(1, GW), "ones": ones_bd}
        last = l == DEPTH - 1

        proj = _inproj_call(x_ctx, None, mods[l], g_mix[l], w_pack, l, l_ctx, False).reshape(b_ctx, l_ctx, IN_PACK)
        o_hy = _hyena_call(proj, hy_conv[l], hy_bias[l], filt[l_ctx][0][l], filt[l_ctx][1][l], l_ctx, 4)
        o_ssd, s_ssd = _ssd_call(proj, ssd_p, None, l_ctx, True)
        o_lru, s_lru = _lru_call(proj, lru_p, None, l_ctx, True)
        o_gdn, s_gdn = _gdn_call(proj, gdn_p, None, l_ctx, True)
        outs = [o.reshape(b_ctx * l_ctx, GW) for o in (o_hy, o_ssd, o_lru, o_gdn)]
        x_ctx = _ffn_call(x_ctx, outs, mods[l], g_ffn[l], ffn_w, l, g_final, l_ctx, False, last)
        lru_out.append(s_lru)
        ssd_out.append(s_ssd)
        gdn_out.append(s_gdn)

        if l == 0:
            proj, x_lat = _inproj_call(x_lat, pos, mods[l], g_mix[l], w_pack, l, l_lat, True)
        else:
            proj = _inproj_call(x_lat, None, mods[l], g_mix[l], w_pack, l, l_lat, True)
        proj = proj.reshape(b_lat, l_lat, IN_PACK)
        o_hy = _hyena_call(proj, hy_conv[l], hy_bias[l], filt[l_lat][0][l], filt[l_lat][1][l], l_lat, 1)
        o_ssd, = _ssd_call(proj, ssd_p, _ssd_state_in(state_ssd[:, l]), l_lat, False)
        o_lru, = _lru_call(proj, lru_p, state_lru[:, l], l_lat, False)
        o_gdn, = _gdn_call(proj, gdn_p, _gdn_state_in(state_gdn[:, l]), l_lat, False)
        outs = [o.reshape(b_lat * l_lat, GW) for o in (o_hy, o_ssd, o_lru, o_gdn)]
        x_lat = _ffn_call(x_lat, outs, mods[l], g_ffn[l], ffn_w, l, g_final, l_lat, True, last)

    return (x_ctx.reshape(b_ctx, l_ctx, D_MODEL), x_lat.reshape(b_lat, l_lat, D_MODEL),
            jnp.stack(lru_out, axis=1), jnp.stack(ssd_out, axis=1), jnp.stack(gdn_out, axis=1))
```

```python
import functools
import math

import ml_dtypes
import numpy as np
import jax
import jax.numpy as jnp
from jax import lax
from jax.experimental import pallas as pl
from jax.experimental.pallas import tpu as pltpu

F32 = jnp.float32
BF16 = jnp.bfloat16
HI = lax.Precision.HIGHEST

D_MODEL = 1024
DEPTH = 2
EPS = 1e-6
GRID_W = 64
POS_BASE = 10000.0
GW = 256
D_FF = 2816
HD = 64
NH = 4
HY_BANDS = 16
HY_EMB = 1 + 2 * HY_BANDS
HY_HIDDEN = 64
LRU_C = 8.0

COL_HY = 0
COL_SSD = 768
COL_LRU = 1536
COL_GDN = 2048
COL_SMALL = 3072
IN_PACK = 3200
SM_DT, SM_B, SM_A = 0, 8, 16

SSD_Q = 128
GDN_C = 128
ROW_TILE = 512
V7X_MXU_DIM = 256
FF_SPLITS = ((0, 6 * V7X_MXU_DIM), (6 * V7X_MXU_DIM, D_FF))
V7X_VMEM_LIMIT = 56 * 1024 * 1024


def _cparams(n_axes=1):
    return pltpu.CompilerParams(dimension_semantics=("arbitrary",) * n_axes,
                                vmem_limit_bytes=V7X_VMEM_LIMIT)


def _silu(x):
    return x * jax.nn.sigmoid(x)


def _softplus(x):
    return jnp.maximum(x, 0.0) + jnp.log1p(jnp.exp(-jnp.abs(x)))


def _gelu_tanh(x):
    return 0.5 * x * (1.0 + jnp.tanh(math.sqrt(2.0 / math.pi) * (x + 0.044715 * (x * x * x))))


def _split_bf16(x):
    hi = x.astype(BF16)
    lo = (x - hi.astype(F32)).astype(BF16)
    return hi, lo


def _dot(a, b):
    return jnp.dot(a, b, preferred_element_type=F32)


def _dot_hi(a, b):
    return jnp.dot(a, b, precision=HI, preferred_element_type=F32)


def _cumsum_rows(tri_b, x):
    x1 = x.astype(BF16)
    r = x - x1.astype(F32)
    x2 = r.astype(BF16)
    x3 = (r - x2.astype(F32)).astype(BF16)
    return _dot(tri_b, x1) + _dot(tri_b, x2) + _dot(tri_b, x3)


def _cumsum_cols(x, tri_b):
    x1 = x.astype(BF16)
    r = x - x1.astype(F32)
    x2 = r.astype(BF16)
    x3 = (r - x2.astype(F32)).astype(BF16)
    return _dot(x1, tri_b) + _dot(x2, tri_b) + _dot(x3, tri_b)


def _dot3_const(ah, al, x):
    xh, xl = _split_bf16(x)
    return _dot(ah, xh) + _dot(ah, xl) + _dot(al, xh)


def _seg_sum(x, ones_bd):
    xh, xl = _split_bf16(x)
    return _dot(xh, ones_bd) + _dot(xl, ones_bd)


def _lane_iota(shape):
    return lax.broadcasted_iota(jnp.int32, shape, len(shape) - 1)


def _head_lanes(cols, rows):
    lane = _lane_iota((rows, GW))
    out = jnp.broadcast_to(cols[NH - 1], (rows, GW))
    for h in range(NH - 2, -1, -1):
        out = jnp.where(lane < (h + 1) * HD, cols[h], out)
    return out


def _dwconv(x, w_ref, pad_ref, n_rows, taps):
    ch = x.shape[1]
    left = taps // 2
    pad_ref[0:8, 0:ch] = jnp.zeros((8, ch), F32)
    pad_ref[8 + n_rows:16 + n_rows, 0:ch] = jnp.zeros((8, ch), F32)
    pad_ref[8:8 + n_rows, 0:ch] = x
    acc = None
    for j in range(taps):
        off = 8 - left + j
        term = pad_ref[off:off + n_rows, 0:ch] * w_ref[j:j + 1, :]
        acc = term if acc is None else acc + term
    return acc


def _mod_kernel(c_ref, w_ref, b_ref, o_ref):
    s_hi, s_lo = _split_bf16(_silu(c_ref[...]))
    w_hi, w_lo = _split_bf16(w_ref[...])
    o_ref[...] = _dot(s_hi, w_hi) + _dot(s_hi, w_lo) + _dot(s_lo, w_hi) + b_ref[...]


def _mod_call(cvec, w_mod, b_mod):
    tn = 1536
    n = w_mod.shape[-1]
    return pl.pallas_call(
        _mod_kernel,
        grid=(DEPTH, n // tn),
        in_specs=[pl.BlockSpec((16, D_MODEL), lambda l, j: (0, 0)),
                  pl.BlockSpec((None, D_MODEL, tn), lambda l, j: (l, 0, j)),
                  pl.BlockSpec((None, 1, tn), lambda l, j: (l, 0, j))],
        out_specs=pl.BlockSpec((None, 16, tn), lambda l, j: (l, 0, j)),
        out_shape=jax.ShapeDtypeStruct((DEPTH, 16, n), F32),
        compiler_params=_cparams(2),
        name="adaln_mod",
    )(cvec, w_mod, b_mod.reshape(DEPTH, 1, n))


def _inproj_kernel(*refs, add_pos):
    if add_pos:
        x_ref, pos_ref, mod_ref, g_ref, wa_ref, wb_ref, wc_ref, o_ref, x0_ref = refs
        x = x_ref[...] + pos_ref[...]
        x0_ref[...] = x
    else:
        x_ref, mod_ref, g_ref, wa_ref, wb_ref, wc_ref, o_ref = refs
        x = x_ref[...]
    mod = mod_ref[...]
    sh = mod[:, 0:D_MODEL]
    sc = mod[:, D_MODEL:2 * D_MODEL]
    h = x * lax.rsqrt(jnp.mean(x * x, axis=-1, keepdims=True) + EPS) * g_ref[...]
    h = (h * (1.0 + sc) + sh).astype(BF16)
    o_ref[:, 0:COL_LRU] = _dot(h, wa_ref[...])
    o_ref[:, COL_LRU:COL_SMALL] = _dot(h, wb_ref[...])
    o_ref[:, COL_SMALL:IN_PACK] = _dot(h, wc_ref[...])


def _inproj_call(x2d, pos, mods_l, g, w_pack, layer, seq_len, is_lat):
    n = x2d.shape[0]
    tm = ROW_TILE
    per_seq = max(seq_len // tm, 1)
    mod_map = (lambda t: (1 + t // per_seq, 0, 0)) if is_lat else (lambda t: (0, 0, 0))
    in_specs = [pl.BlockSpec((tm, D_MODEL), lambda t: (t, 0))]
    args = [x2d]
    if pos is not None:
        in_specs.append(pl.BlockSpec((tm, D_MODEL), lambda t: (t % per_seq, 0)))
        args.append(pos)
    in_specs += [pl.BlockSpec((None, 1, 6 * D_MODEL), mod_map),
                 pl.BlockSpec((1, D_MODEL), lambda t: (0, 0)),
                 pl.BlockSpec((None, D_MODEL, COL_LRU), lambda t: (layer, 0, 0)),
                 pl.BlockSpec((None, D_MODEL, COL_SMALL - COL_LRU), lambda t: (layer, 0, 0)),
                 pl.BlockSpec((None, D_MODEL, IN_PACK - COL_SMALL), lambda t: (layer, 0, 0))]
    args += [mods_l, g.reshape(1, D_MODEL), *w_pack]
    out_specs = pl.BlockSpec((tm, IN_PACK), lambda t: (t, 0))
    out_shape = jax.ShapeDtypeStruct((n, IN_PACK), F32)
    if pos is not None:
        out_specs = [out_specs, pl.BlockSpec((tm, D_MODEL), lambda t: (t, 0))]
        out_shape = [out_shape, jax.ShapeDtypeStruct((n, D_MODEL), F32)]
    return pl.pallas_call(
        functools.partial(_inproj_kernel, add_pos=pos is not None),
        grid=(n // tm,),
        in_specs=in_specs,
        out_specs=out_specs,
        out_shape=out_shape,
        compiler_params=_cparams(1),
        name="inproj_lat0" if pos is not None else "inproj",
    )(*args)


@functools.lru_cache(maxsize=None)
def _dft_tables(seq_len):
    n = 2 * seq_len - 1
    idx = np.arange(seq_len, dtype=np.int64)
    ang = (2.0 * np.pi / n) * ((idx[:, None] * idx[None, :]) % n).astype(np.float64)
    out = []
    for tab in (np.cos(ang), np.sin(ang)):
        hi = tab.astype(ml_dtypes.bfloat16)
        lo = (tab - hi.astype(np.float64)).astype(ml_dtypes.bfloat16)
        out += [hi, lo]
    return tuple(out)


@functools.lru_cache(maxsize=None)
def _hyena_consts(seq_len):
    t = np.linspace(0.0, 1.0, seq_len, dtype=np.float32)[:, None]
    w = np.float32(2.0 * math.pi / seq_len) * np.arange(seq_len, dtype=np.float32)[:, None]
    bands = np.linspace(1e-4, HY_BANDS - 1, HY_BANDS, dtype=np.float32)[None, :]
    feats = np.concatenate([t, np.cos(bands * w), -np.sin(bands * w)], axis=-1).astype(np.float32)
    feats_p = np.zeros((seq_len, 128), np.float32)
    feats_p[:, :HY_EMB] = feats
    max_decay = math.log(1e-2) / 0.3
    min_decay = math.log(1e-2) / 1.5
    deltas = np.abs(np.linspace(min_decay, max_decay, GW, dtype=np.float32))
    decay = np.exp(-t * deltas).astype(np.float32)
    return feats_p, np.concatenate([decay, decay], axis=1)


def _hyfilt_kernel(feat_ref, dec_ref, w1_ref, b1_ref, w2_ref, b2_ref, w3_ref, fr_ref,
                   ch_ref, cl_ref, sh_ref, sl_ref, hr_ref, hi_ref, *, seq_len):
    fr = fr_ref[...]
    h = jnp.sin(fr[0:1, :] * (_dot_hi(feat_ref[...], w1_ref[...]) + b1_ref[...]))
    h = jnp.sin(fr[1:2, :] * (_dot_hi(h, w2_ref[...]) + b2_ref[...]))
    h = _dot_hi(h, w3_ref[...]) * dec_ref[...]
    den = jnp.sum(jnp.abs(h[:, :GW]) + jnp.abs(h[:, GW:]), axis=0, keepdims=True)
    h = h / jnp.concatenate([den, den], axis=1)
    hc = _dot3_const(ch_ref[...], cl_ref[...], h)
    hs = _dot3_const(sh_ref[...], sl_ref[...], h)
    n = 2 * seq_len - 1
    row = lax.broadcasted_iota(jnp.int32, (seq_len, GW), 0)
    wgt = jnp.where(row == 0, 1.0 / n, 2.0 / n)
    hr_ref[...] = (hc[:, :GW] + hc[:, GW:]) * wgt
    hi_ref[...] = (hs[:, GW:] - hs[:, :GW]) * wgt


def _hyfilt_call(seq_len, hw):
    feats, decay = _hyena_consts(seq_len)
    tabs = _dft_tables(seq_len)
    full = lambda shape: pl.BlockSpec(shape, lambda l: (0,) * len(shape))
    lay = lambda shape: pl.BlockSpec((None,) + shape, lambda l: (l,) + (0,) * len(shape))
    return pl.pallas_call(
        functools.partial(_hyfilt_kernel, seq_len=seq_len),
        grid=(DEPTH,),
        in_specs=[full((seq_len, 128)), full((seq_len, 2 * GW)),
                  lay((128, 128)), lay((1, 128)), lay((128, 128)), lay((1, 128)), lay((128, 2 * GW)),
                  lay((2, 128))] + [full((seq_len, seq_len))] * 4,
        out_specs=[lay((seq_len, GW)), lay((seq_len, GW))],
        out_shape=[jax.ShapeDtypeStruct((DEPTH, seq_len, GW), F32)] * 2,
        compiler_params=_cparams(1),
        name=f"hyena_filter_{seq_len}",
    )(feats, decay, hw["w1"], hw["b1"], hw["w2"], hw["b2"], hw["w3"], hw["freq"], *tabs)


def _hyena_kernel(u_ref, cw_ref, bias_ref, hr_ref, hi_ref, ch_ref, cl_ref, sh_ref, sl_ref,
                  o_ref, pad_ref, *, seq_len, n_seq):
    for b in range(n_seq):
        uc = _dwconv(u_ref[b], cw_ref, pad_ref, seq_len, 3)
        x0 = uc[:, 0:GW]
        z = uc[:, 2 * GW:3 * GW] * uc[:, GW:2 * GW]
        za = _dot3_const(ch_ref[...], cl_ref[...], z)
        zb = _dot3_const(sh_ref[...], sl_ref[...], z)
        hr = hr_ref[...]
        hi = hi_ref[...]
        yr = za * hr + zb * hi
        yi = za * hi - zb * hr
        y = _dot3_const(ch_ref[...], cl_ref[...], yr) - _dot3_const(sh_ref[...], sl_ref[...], yi)
        o_ref[b] = x0 * (y + z * bias_ref[...])


def _hyena_call(proj3, conv_w, bias, hr, hi, seq_len, n_seq):
    bsz = proj3.shape[0]
    tabs = _dft_tables(seq_len)
    full = lambda shape: pl.BlockSpec(shape, lambda b: (0,) * len(shape))
    return pl.pallas_call(
        functools.partial(_hyena_kernel, seq_len=seq_len, n_seq=n_seq),
        grid=(bsz // n_seq,),
        in_specs=[pl.BlockSpec((n_seq, seq_len, 3 * GW), lambda b: (b, 0, COL_HY // (3 * GW))),
                  full((3, 3 * GW)), full((1, GW)), full((seq_len, GW)), full((seq_len, GW))]
                 + [full((seq_len, seq_len))] * 4,
        out_specs=pl.BlockSpec((n_seq, seq_len, GW), lambda b: (b, 0, 0)),
        out_shape=jax.ShapeDtypeStruct((bsz, seq_len, GW), F32),
        scratch_shapes=[pltpu.VMEM((seq_len + 16, 3 * GW), F32)],
        compiler_params=_cparams(1),
        name=f"hyena_{seq_len}",
    )(proj3, conv_w, bias.reshape(1, GW), hr, hi, *tabs)


def _tri_masks(q):
    row = lax.broadcasted_iota(jnp.int32, (q, q), 0)
    col = lax.broadcasted_iota(jnp.int32, (q, q), 1)
    return col <= row, col >= row


def _ssd_kernel(*refs, seq_len, has_s0, emit_state):
    it = iter(refs)
    zx_ref, sm_ref, conv_ref, dtb_r_ref, dtb_c_ref, an_r_ref, an_c_ref, dvec_ref, gn_ref = (
        next(it) for _ in range(9))
    s0_ref = next(it) if has_s0 else None
    o_ref = next(it)
    so_ref = next(it) if emit_state else None
    pad_ref, xs_ref, cm_ref, ac_ref, bt_ref, ar_ref, dtr_ref, y_ref, st_ref, upd_ref, ecs_ref = it
    q = SSD_Q
    nc = seq_len // q

    xbc = _silu(_dwconv(zx_ref[:, GW:3 * GW], conv_ref, pad_ref, seq_len, 4))
    xs_ref[...] = xbc[:, 0:GW]
    cm_ref[...] = xbc[:, GW + 128:GW + 256].astype(BF16)
    y_ref[...] = xbc[:, 0:GW] * dvec_ref[...]
    bm_t = xbc[:, GW:GW + 128].T.astype(BF16)
    sm = sm_ref[...]
    ac_ref[...] = _softplus(sm + dtb_r_ref[...]) * an_r_ref[...]
    sm_t = sm.T
    dtr = _softplus(sm_t[SM_DT:SM_DT + 8, :] + dtb_c_ref[...])
    ar = dtr * an_c_ref[...]
    for c in range(nc):
        bt_ref[c] = bm_t[:, c * q:(c + 1) * q]
        ar_ref[c] = ar[:, c * q:(c + 1) * q]
        dtr_ref[c] = dtr[:, c * q:(c + 1) * q]
    if has_s0:
        st_ref[...] = s0_ref[...]
    else:
        st_ref[...] = jnp.zeros(st_ref.shape, F32)

    lower, upper = _tri_masks(q)
    lo_f = lower.astype(F32).astype(BF16)
    up_f = upper.astype(F32).astype(BF16)
    lane128 = _lane_iota((q, 128))
    lane256 = _lane_iota((q, GW))
    st_row = lax.broadcasted_iota(jnp.int32, (128, GW), 0)
    st_col = lax.broadcasted_iota(jnp.int32, (128, GW), 1)
    st_mask = (st_row // HD) == (st_col // 128)
    dirs = (0, 1)
    tri_c = (lo_f, up_f)
    tri_r = (up_f, lo_f)
    masks = (lower, upper)
    last = (q - 1, 0)
    spread = [((st_row == d * NH + st_col // HD)).astype(F32).astype(BF16) for d in dirs]

    def chunk_body(ci, carry):
        rows = pl.ds(pl.multiple_of(ci * q, q), q)
        xs_b = xs_ref[rows, :].astype(BF16)
        c_c = cm_ref[rows, :]
        b_t = bt_ref[ci]
        a_c = ac_ref[rows, :]
        a_r = ar_ref[ci]
        dt_r = dtr_ref[ci]
        stack = jnp.concatenate(
            [jnp.where((lane256 // HD) == h, xs_b, jnp.zeros_like(xs_b)) for h in range(NH)], axis=0)
        c_g = jnp.concatenate([jnp.where(lane128 < HD, c_c, jnp.zeros_like(c_c)),
                               jnp.where(lane128 >= HD, c_c, jnp.zeros_like(c_c))], axis=0)
        g_mat = _dot(c_g, b_t)
        cum_c = [_cumsum_rows(tri_c[d], a_c) for d in dirs]
        cum_r = [_cumsum_cols(a_r, tri_r[d]) for d in dirs]
        w_cat, bw_cat = [], []
        for d in dirs:
            w_parts, bw_parts = [], []
            for h in range(NH):
                j = d * NH + h
                g = h // 2
                diff = cum_c[d][:, j:j + 1] - cum_r[d][j:j + 1, :]
                m = jnp.where(masks[d], jnp.exp(jnp.minimum(diff, 0.0)), 0.0)
                w_parts.append((g_mat[g * q:(g + 1) * q, :] * m * dt_r[j:j + 1, :]).astype(BF16))
                to_end = jnp.exp(cum_r[d][j:j + 1, last[d]:last[d] + 1] - cum_r[d][j:j + 1, :])
                bw_parts.append((b_t.astype(F32) * (dt_r[j:j + 1, :] * to_end)).astype(BF16))
            w_cat.append(jnp.concatenate(w_parts, axis=1))
            bw_cat.append(jnp.concatenate(bw_parts, axis=1))
        y_diag = [_dot(w_cat[d], stack) for d in dirs]
        upd = [_dot(bw_cat[d], stack) for d in dirs]
        ecs = []
        for d in dirs:
            e_hi, e_lo = _split_bf16(jnp.exp(cum_c[d]))
            ecs.append(_dot(e_hi, spread[d]) + _dot(e_lo, spread[d]))
        y_ref[rows, :] += y_diag[0] + y_diag[1]
        for d in dirs:
            upd_ref[d, ci] = jnp.where(st_mask, upd[d], 0.0)
            ecs_ref[d, ci] = ecs[d]
        return carry

    lax.fori_loop(0, nc, chunk_body, 0)

    def state_body(i, carry):
        cis = (i, nc - 1 - i)
        rows = [pl.ds(pl.multiple_of(ci * q, q), q) for ci in cis]
        ss = [st_ref[d] for d in dirs]
        y_off = [_dot(cm_ref[rows[d], :], ss[d].astype(BF16)) for d in dirs]
        for d in dirs:
            ecs = ecs_ref[d, cis[d]]
            y_ref[rows[d], :] += y_off[d] * ecs
            st_ref[d] = ss[d] * ecs[last[d]:last[d] + 1, :] + upd_ref[d, cis[d]]
        return carry

    lax.fori_loop(0, nc, state_body, 0)

    y = y_ref[...] * _silu(zx_ref[:, 0:GW])
    y = y * lax.rsqrt(jnp.mean(y * y, axis=-1, keepdims=True) + EPS) * gn_ref[...]
    o_ref[...] = y
    if emit_state:
        for d in range(2):
            st_t = st_ref[d].T
            for h in range(NH):
                blk = st_t[h * HD:(h + 1) * HD, :]
                if h // 2 == 1:
                    blk = pltpu.roll(blk, HD, axis=1)
                so_ref[d, h] = blk[:, 0:HD]


def _ssd_call(proj3, p, s0, seq_len, emit_state):
    bsz = proj3.shape[0]
    nc = seq_len // SSD_Q
    full = lambda shape: pl.BlockSpec(shape, lambda b: (0,) * len(shape))
    in_specs = [pl.BlockSpec((None, seq_len, 3 * GW), lambda b: (b, 0, COL_SSD // (3 * GW))),
                pl.BlockSpec((None, seq_len, 128), lambda b: (b, 0, COL_SMALL // 128)),
                full((4, 2 * GW)), full((1, 128)), full((8, 1)), full((1, 128)), full((8, 1)),
                full((1, GW)), full((1, GW))]
    args = [proj3, proj3, p["conv"], p["dtb_row"], p["dtb_col"], p["an_row"], p["an_col"], p["dvec"], p["gn"]]
    if s0 is not None:
        in_specs.append(pl.BlockSpec((None, 2, 128, GW), lambda b: (b, 0, 0, 0)))
        args.append(s0)
    out_specs = [pl.BlockSpec((None, seq_len, GW), lambda b: (b, 0, 0))]
    out_shape = [jax.ShapeDtypeStruct((bsz, seq_len, GW), F32)]
    if emit_state:
        out_specs.append(pl.BlockSpec((None, 2, NH, HD, HD), lambda b: (b, 0, 0, 0, 0)))
        out_shape.append(jax.ShapeDtypeStruct((bsz, 2, NH, HD, HD), F32))
    scratch = [pltpu.VMEM((seq_len + 16, 2 * GW), F32),
               pltpu.VMEM((seq_len, GW), F32),
               pltpu.VMEM((seq_len, 128), BF16),
               pltpu.VMEM((seq_len, 128), F32),
               pltpu.VMEM((nc, 128, SSD_Q), BF16),
               pltpu.VMEM((nc, 8, SSD_Q), F32),
               pltpu.VMEM((nc, 8, SSD_Q), F32),
               pltpu.VMEM((seq_len, GW), F32),
               pltpu.VMEM((2, 128, GW), F32),
               pltpu.VMEM((2, nc, 128, GW), F32),
               pltpu.VMEM((2, nc, SSD_Q, GW), F32)]
    return pl.pallas_call(
        functools.partial(_ssd_kernel, seq_len=seq_len, has_s0=s0 is not None, emit_state=emit_state),
        grid=(bsz,),
        in_specs=in_specs,
        out_specs=out_specs,
        out_shape=out_shape,
        scratch_shapes=scratch,
        compiler_params=_cparams(1),
        name=f"ssd_{seq_len}",
    )(*args)


def _lru_kernel(*refs, seq_len, has_s0, emit_state):
    it = iter(refs)
    u_ref, conv_ref, w_ref, bias_ref, sp_ref = (next(it) for _ in range(5))
    s0_ref = next(it) if has_s0 else None
    o_ref = next(it)
    so_ref = next(it) if emit_state else None
    pad_ref, af_ref, bf_ref, ab_ref, bb_ref = it
    n = seq_len

    xc = _dwconv(u_ref[:, 0:GW], conv_ref, pad_ref, n, 4)
    ri = jax.nn.sigmoid(_dot(xc.astype(BF16), w_ref[...]) + bias_ref[...])
    log_a = -LRU_C * ri[:, 0:2 * GW] * sp_ref[...]
    a = jnp.exp(log_a)
    bt = jnp.sqrt(jnp.maximum(1.0 - a * a, 0.0)) * ri[:, 2 * GW:4 * GW] * jnp.concatenate([xc, xc], axis=1)

    ones8 = jnp.ones((8, GW), F32)
    zeros8 = jnp.zeros((8, GW), F32)
    for a_ref_, b_ref_, lo in ((af_ref, bf_ref, 0), (ab_ref, bb_ref, GW)):
        a_ref_[0:8, :] = ones8
        a_ref_[8 + n:16 + n, :] = ones8
        b_ref_[0:8, :] = zeros8
        b_ref_[8 + n:16 + n, :] = zeros8
        a_ref_[8:8 + n, :] = a[:, lo:lo + GW]
        b_ref_[8:8 + n, :] = bt[:, lo:lo + GW]
    if has_s0:
        s0 = s0_ref[...]
        bf_ref[8:9, :] = bf_ref[8:9, :] + af_ref[8:9, :] * s0[0:1, :]
        bb_ref[7 + n:8 + n, :] = bb_ref[7 + n:8 + n, :] + ab_ref[7 + n:8 + n, :] * s0[1:2, :]

    d = 1
    while d < n:
        if d < 8:
            a_cur = af_ref[8:8 + n, :]
            af_ref[8:8 + n, :] = a_cur * af_ref[8 - d:8 - d + n, :]
            bf_ref[8:8 + n, :] = a_cur * bf_ref[8 - d:8 - d + n, :] + bf_ref[8:8 + n, :]
            a_cur = ab_ref[8:8 + n, :]
            ab_ref[8:8 + n, :] = a_cur * ab_ref[8 + d:8 + d + n, :]
            bb_ref[8:8 + n, :] = a_cur * bb_ref[8 + d:8 + d + n, :] + bb_ref[8:8 + n, :]
        else:
            m = n - d
            a_cur = af_ref[8 + d:8 + n, :]
            a_new = a_cur * af_ref[8:8 + m, :]
            b_new = a_cur * bf_ref[8:8 + m, :] + bf_ref[8 + d:8 + n, :]
            af_ref[8 + d:8 + n, :] = a_new
            bf_ref[8 + d:8 + n, :] = b_new
            a_cur = ab_ref[8:8 + m, :]
            a_new = a_cur * ab_ref[8 + d:8 + n, :]
            b_new = a_cur * bb_ref[8 + d:8 + n, :] + bb_ref[8:8 + m, :]
            ab_ref[8:8 + m, :] = a_new
            bb_ref[8:8 + m, :] = b_new
        d *= 2

    y = (bf_ref[8:8 + n, :] + bb_ref[8:8 + n, :]) * _gelu_tanh(u_ref[:, GW:2 * GW])
    o_ref[...] = y
    if emit_state:
        so_ref[0:1, :] = bf_ref[7 + n:8 + n, :]
        so_ref[1:2, :] = bb_ref[8:9, :]


def _lru_call(proj3, p, s0, seq_len, emit_state):
    bsz = proj3.shape[0]
    full = lambda shape: pl.BlockSpec(shape, lambda b: (0,) * len(shape))
    in_specs = [pl.BlockSpec((None, seq_len, 2 * GW), lambda b: (b, 0, COL_LRU // (2 * GW))),
                full((4, GW)), full((GW, 4 * GW)), full((1, 4 * GW)), full((1, 2 * GW))]
    args = [proj3, p["conv"], p["w"], p["bias"], p["sp"]]
    if s0 is not None:
        in_specs.append(pl.BlockSpec((None, 2, GW), lambda b: (b, 0, 0)))
        args.append(s0)
    out_specs = [pl.BlockSpec((None, seq_len, GW), lambda b: (b, 0, 0))]
    out_shape = [jax.ShapeDtypeStruct((bsz, seq_len, GW), F32)]
    if emit_state:
        out_specs.append(pl.BlockSpec((None, 2, GW), lambda b: (b, 0, 0)))
        out_shape.append(jax.ShapeDtypeStruct((bsz, 2, GW), F32))
    scratch = [pltpu.VMEM((seq_len + 16, GW), F32)] * 5
    return pl.pallas_call(
        functools.partial(_lru_kernel, seq_len=seq_len, has_s0=s0 is not None, emit_state=emit_state),
        grid=(bsz,),
        in_specs=in_specs,
        out_specs=out_specs,
        out_shape=out_shape,
        scratch_shapes=scratch,
        compiler_params=_cparams(1),
        name=f"rglru_{seq_len}",
    )(*args)


TRI_BASE = 8


def _unit_tri_inverse(a_mats, lower_flags, eye, same_blk):
    c = a_mats[0].shape[0]
    ds = [jnp.where(same_blk[TRI_BASE], a, 0.0) for a in a_mats]
    d_bs = [d.astype(BF16) for d in ds]
    ps = [eye - d for d in ds]
    x_bs = [_dot(d_b, d_b).astype(BF16) for d_b in d_bs]
    rs = [_dot(x_b, jnp.concatenate([p.astype(BF16), x_b], axis=1)) for x_b, p in zip(x_bs, ps)]
    ps = [p + r[:, 0:c] for p, r in zip(ps, rs)]
    ts = [p + _dot(r[:, c:2 * c].astype(BF16), p.astype(BF16)) for p, r in zip(ps, rs)]
    size = 2 * TRI_BASE
    while size <= c:
        half = size // 2
        off = same_blk[size] & jnp.logical_not(same_blk[half])
        e_bs = [jnp.where(off, a, 0.0).astype(BF16) for a in a_mats]
        t_bs = [t.astype(BF16) for t in ts]
        starts = [[blk * size + (half if low else 0) for blk in range(c // size)] for low in lower_flags]
        rows = [jnp.concatenate([t[r0:r0 + half, :] for r0 in st], axis=0).astype(BF16)
                for t, st in zip(ts, starts)]
        us = [_dot(r, e_b).astype(BF16) for r, e_b in zip(rows, e_bs)]
        upds = [_dot(u, t_b) for u, t_b in zip(us, t_bs)]
        new_ts = []
        for t, upd, st, low in zip(ts, upds, starts, lower_flags):
            pieces = []
            for n, r0 in enumerate(st):
                changed = t[r0:r0 + half, :] - upd[n * half:(n + 1) * half, :]
                kept = t[r0 - half:r0, :] if low else t[r0 + half:r0 + size, :]
                pieces += [kept, changed] if low else [changed, kept]
            new_ts.append(jnp.concatenate(pieces, axis=0))
        ts = new_ts
        size *= 2
    return ts


def _gdn_kernel(*refs, seq_len, has_s0, emit_state):
    it = iter(refs)
    u_ref, sm_ref, conv_ref, dtb_r_ref, dtb_c_ref, an_r_ref, an_c_ref, gn_ref, ones_ref = (
        next(it) for _ in range(9))
    s0_ref = next(it) if has_s0 else None
    o_ref = next(it)
    so_ref = next(it) if emit_state else None
    (pad_ref, q_ref, k_ref, v_ref, be_ref, gc_ref, gr_ref, oa_ref, st_ref,
     us_ref, wq_ref, at_ref, kd_ref, dr_ref) = it
    c = GDN_C
    nc = seq_len // c
    ones_bd = ones_ref[...]

    qkv = _silu(_dwconv(u_ref[:, 0:3 * GW], conv_ref, pad_ref, seq_len, 4))
    qq = qkv[:, 0:GW]
    kk = qkv[:, GW:2 * GW]
    q_ref[...] = qq * lax.rsqrt(_seg_sum(qq * qq, ones_bd) + EPS) * (HD ** -0.5)
    k_ref[...] = kk * lax.rsqrt(_seg_sum(kk * kk, ones_bd) + EPS)
    v_ref[...] = qkv[:, 2 * GW:3 * GW]
    sm = sm_ref[...]
    be_ref[...] = jax.nn.sigmoid(sm)
    gc_ref[...] = an_r_ref[...] * _softplus(sm + dtb_r_ref[...])
    sm_t = sm.T
    g_r = an_c_ref[...] * _softplus(sm_t[SM_A:SM_A + 8, :] + dtb_c_ref[...])
    for i in range(nc):
        gr_ref[i] = g_r[:, i * c:(i + 1) * c]
    oa_ref[...] = jnp.zeros(oa_ref.shape, F32)
    if has_s0:
        st_ref[...] = s0_ref[...]
    else:
        st_ref[...] = jnp.zeros(st_ref.shape, F32)

    lower, upper = _tri_masks(c)
    lo_f = lower.astype(F32).astype(BF16)
    up_f = upper.astype(F32).astype(BF16)
    row_i =lax.broadcasted_iota(jnp.int32, (c, c), 0)
    col_i = lax.broadcasted_iota(jnp.int32, (c, c), 1)
    eye = (row_i == col_i).astype(F32)
    same_blk = {}
    size = TRI_BASE
    while size <= c:
        same_blk[size] = (row_i // size) == (col_i // size)
        size *= 2
    lane256 = _lane_iota((c, GW))
    lane512 = _lane_iota((c, 2 * GW))
    bd_row = lax.broadcasted_iota(jnp.int32, (GW, GW), 0)
    bd_col = lax.broadcasted_iota(jnp.int32, (GW, GW), 1)
    bd_mask = (bd_row // HD) == (bd_col // HD)
    kt_row = lax.broadcasted_iota(jnp.int32, (GW, c), 0)

    def stack_heads(x_b, lane):
        zero = jnp.zeros_like(x_b)
        return jnp.concatenate([jnp.where(((lane % GW) // HD) == h, x_b, zero) for h in range(NH)], axis=0)

    dirs = (0, 1)
    tri_c = (lo_f, up_f)
    tri_r = (up_f, lo_f)
    incl = (lower, upper)
    strict = (col_i < row_i, col_i > row_i)
    last = (c - 1, 0)

    chunks_per_step = 2

    def solve_body(step, carry):
        cks = [step * chunks_per_step + n for n in range(chunks_per_step)]
        units = [(n, d) for n in range(chunks_per_step) for d in dirs]
        rows = [pl.ds(pl.multiple_of(ci * c, c), c) for ci in cks]
        q_c = [q_ref[r, :] for r in rows]
        k_c = [k_ref[r, :] for r in rows]
        v_c = [v_ref[r, :] for r in rows]
        be = [be_ref[r, :] for r in rows]
        g_c = [gc_ref[r, :] for r in rows]
        g_r = [gr_ref[ci] for ci in cks]
        cum_c = {(n, d): _cumsum_rows(tri_c[d], g_c[n]) for n, d in units}
        cum_r = {(n, d): _cumsum_cols(g_r[n], tri_r[d]) for n, d in units}
        kq = []
        for n in range(chunks_per_step):
            k_t = k_c[n].T.astype(BF16)
            k_t_heads = jnp.concatenate(
                [jnp.where((kt_row // HD) == h, k_t, jnp.zeros_like(k_t)) for h in range(NH)], axis=1)
            kq.append(_dot(jnp.concatenate([k_c[n], q_c[n]], axis=0).astype(BF16), k_t_heads))
        a_mats, attn, flags = [], {}, []
        for n, d in units:
            for h in range(NH):
                j = d * NH + h
                diff = cum_c[n, d][:, SM_A + j:SM_A + j + 1] - cum_r[n, d][j:j + 1, :]
                dec = jnp.where(incl[d], jnp.exp(jnp.minimum(diff, 0.0)), 0.0)
                a_mats.append(jnp.where(
                    strict[d], kq[n][0:c, h * c:(h + 1) * c] * be[n][:, SM_B + j:SM_B + j + 1] * dec, 0.0))
                attn[n, d, h] = (kq[n][c:2 * c, h * c:(h + 1) * c] * dec).astype(BF16)
                flags.append(d == 0)
        t_mats = _unit_tri_inverse(a_mats, flags, eye, same_blk)
        a_split = [_split_bf16(a) for a in a_mats]
        t_cat, ah_cat, al_cat = {}, {}, {}
        for idx, (n, d) in enumerate(units):
            sl = slice(idx * NH, (idx + 1) * NH)
            t_cat[n, d] = jnp.concatenate([t.astype(BF16) for t in t_mats[sl]], axis=1)
            ah_cat[n, d] = jnp.concatenate([hi for hi, _ in a_split[sl]], axis=1)
            al_cat[n, d] = jnp.concatenate([lo for _, lo in a_split[sl]], axis=1)
        cols = {(n, d): [cum_c[n, d][:, SM_A + d * NH + h:SM_A + d * NH + h + 1] for h in range(NH)]
                for n, d in units}
        lasts = {(n, d): [cum_c[n, d][last[d]:last[d] + 1, SM_A + d * NH + h:SM_A + d * NH + h + 1]
                          for h in range(NH)] for n, d in units}
        b_l = {(n, d): _head_lanes([be[n][:, SM_B + d * NH + h:SM_B + d * NH + h + 1] for h in range(NH)], c)
               for n, d in units}
        e_l = {(n, d): _head_lanes([jnp.exp(cc) for cc in cols[n, d]], c) for n, d in units}
        rhs = {(n, d): jnp.concatenate([v_c[n] * b_l[n, d], k_c[n] * b_l[n, d] * e_l[n, d]], axis=1)
               for n, d in units}
        sol_b = {u: _dot(t_cat[u], stack_heads(rhs[u].astype(BF16), lane512)).astype(BF16)
                 for u in units}
        st_sol = {u: stack_heads(sol_b[u], lane512) for u in units}
        a_sol = {u: _dot(ah_cat[u], st_sol[u]) + _dot(al_cat[u], st_sol[u]) for u in units}
        uw = {}
        for u in units:
            sol = sol_b[u].astype(F32)
            uw[u] = sol + _dot(t_cat[u], stack_heads((rhs[u] - sol - a_sol[u]).astype(BF16), lane512))
        for n, d in units:
            ci = cks[n]
            us_ref[d, ci] = uw[n, d][:, 0:GW]
            wq_ref[d, ci, 0:c, :] = uw[n, d][:, GW:2 * GW].astype(BF16)
            wq_ref[d, ci, c:2 * c, :] = (q_c[n] * e_l[n, d]).astype(BF16)
            at_ref[d, ci] = jnp.concatenate([attn[n, d, h] for h in range(NH)], axis=1)
            k_dec = k_c[n] * _head_lanes([jnp.exp(lasts[n, d][h] - cols[n, d][h]) for h in range(NH)], c)
            kd_ref[d, ci] = k_dec.T.astype(BF16)
            dr_ref[d, ci] = _head_lanes([jnp.exp(lv) for lv in lasts[n, d]], 1)
        return carry

    lax.fori_loop(0, nc // chunks_per_step, solve_body, 0)

    def state_body(i, carry):
        cis = (i, nc - 1 - i)
        ss = [st_ref[d] for d in dirs]
        r2 = [_dot(wq_ref[d, cis[d]], ss[d].astype(BF16)) for d in dirs]
        v_new_b = [(us_ref[d, cis[d]] - r2[d][0:c, :]).astype(BF16) for d in dirs]
        upd = [_dot(kd_ref[d, cis[d]], v_new_b[d]) for d in dirs]
        o_c = [r2[d][c:2 * c, :] + _dot(at_ref[d, cis[d]], stack_heads(v_new_b[d], lane256)) for d in dirs]
        for d in dirs:
            st_ref[d] = ss[d] * dr_ref[d, cis[d]] + jnp.where(bd_mask, upd[d], 0.0)
            rows = pl.ds(pl.multiple_of(cis[d] * c, c), c)
            oa_ref[rows, :] += o_c[d]
        return carry

    lax.fori_loop(0, nc, state_body, 0)

    o = oa_ref[...]
    o = o * lax.rsqrt(_seg_sum(o * o, ones_bd) * (1.0 / HD) + EPS) * gn_ref[...]
    o_ref[...] = o * _silu(u_ref[:, 3 * GW:4 * GW])
    if emit_state:
        for d in range(2):
            for h in range(NH):
                blk = st_ref[d, h * HD:(h + 1) * HD, (h // 2) * 128:(h // 2 + 1) * 128]
                if h % 2 == 1:
                    blk = pltpu.roll(blk, HD, axis=1)
                so_ref[d, h] = blk[:, 0:HD]


def _gdn_call(proj3, p, s0, seq_len, emit_state):
    bsz = proj3.shape[0]
    nc = seq_len // GDN_C
    full = lambda shape: pl.BlockSpec(shape, lambda b: (0,) * len(shape))
    in_specs = [pl.BlockSpec((None, seq_len, 4 * GW), lambda b: (b, 0, COL_GDN // (4 * GW))),
                pl.BlockSpec((None, seq_len, 128), lambda b: (b, 0, COL_SMALL // 128)),
                full((4, 3 * GW)), full((1, 128)), full((8, 1)), full((1, 128)), full((8, 1)),
                full((1, GW)), full((GW, GW))]
    args = [proj3, proj3, p["conv"], p["dtb_row"], p["dtb_col"], p["an_row"], p["an_col"], p["gn"], p["ones"]]
    if s0 is not None:
        in_specs.append(pl.BlockSpec((None, 2, GW, GW), lambda b: (b, 0, 0, 0)))
        args.append(s0)
    out_specs = [pl.BlockSpec((None, seq_len, GW), lambda b: (b, 0, 0))]
    out_shape = [jax.ShapeDtypeStruct((bsz, seq_len, GW), F32)]
    if emit_state:
        out_specs.append(pl.BlockSpec((None, 2, NH, HD, HD), lambda b: (b, 0, 0, 0, 0)))
        out_shape.append(jax.ShapeDtypeStruct((bsz, 2, NH, HD, HD), F32))
    scratch = [pltpu.VMEM((seq_len + 16, 3 * GW), F32),
               pltpu.VMEM((seq_len, GW), F32),
               pltpu.VMEM((seq_len, GW), F32),
               pltpu.VMEM((seq_len, GW), F32),
               pltpu.VMEM((seq_len, 128), F32),
               pltpu.VMEM((seq_len, 128), F32),
               pltpu.VMEM((nc, 8, GDN_C), F32),
               pltpu.VMEM((seq_len, GW), F32),
               pltpu.VMEM((2, GW, GW), F32),
               pltpu.VMEM((2, nc, GDN_C, GW), F32),
               pltpu.VMEM((2, nc, 2 * GDN_C, GW), BF16),
               pltpu.VMEM((2, nc, GDN_C, NH * GDN_C), BF16),
               pltpu.VMEM((2, nc, GW, GDN_C), BF16),
               pltpu.VMEM((2, nc, 1, GW), F32)]
    return pl.pallas_call(
        functools.partial(_gdn_kernel, seq_len=seq_len, has_s0=s0 is not None, emit_state=emit_state),
        grid=(bsz,),
        in_specs=in_specs,
        out_specs=out_specs,
        out_shape=out_shape,
        scratch_shapes=scratch,
        compiler_params=_cparams(1),
        name=f"gdn_{seq_len}",
    )(*args)


def _ffn_kernel(x_ref, ohy_ref, ossd_ref, olru_ref, ogdn_ref, mod_ref, g_ref, wo_ref, wg_ref, wu_ref,
                wd_ref, gfin_ref, o_ref, *, final_norm):
    mod = mod_ref[...]
    ga_m = mod[:, 2 * D_MODEL:3 * D_MODEL]
    sh_f = mod[:, 3 * D_MODEL:4 * D_MODEL]
    sc_f = mod[:, 4 * D_MODEL:5 * D_MODEL]
    ga_f = mod[:, 5 * D_MODEL:6 * D_MODEL]
    mo = None
    for i, r in enumerate((ohy_ref, ossd_ref, olru_ref, ogdn_ref)):
        part = _dot(r[...].astype(BF16), wo_ref[i * GW:(i + 1) * GW, :])
        mo = part if mo is None else mo + part
    x = x_ref[...] + ga_m * mo
    h = x * lax.rsqrt(jnp.mean(x * x, axis=-1, keepdims=True) + EPS) * g_ref[...]
    h = (h * (1.0 + sc_f) + sh_f).astype(BF16)
    ff = None
    for lo, hi in FF_SPLITS:
        gate = _dot(h, wg_ref[:, lo:hi])
        up = _dot(h, wu_ref[:, lo:hi])
        part = _dot((_silu(gate) * up).astype(BF16), wd_ref[lo:hi, :])
        ff = part if ff is None else ff + part
    x = x + ga_f * ff
    if final_norm:
        x = x * lax.rsqrt(jnp.mean(x * x, axis=-1, keepdims=True) + EPS) * gfin_ref[...]
    o_ref[...] = x


def _ffn_call(x2d, outs, mods_l, g, weights, layer, g_final, seq_len, is_lat, final_norm):
    n = x2d.shape[0]
    tm = ROW_TILE
    per_seq = seq_len // tm
    mod_map = (lambda t: (1 + t // per_seq, 0, 0)) if is_lat else (lambda t: (0, 0, 0))
    tile = lambda w: pl.BlockSpec((tm, w), lambda t: (t, 0))
    res = lambda shape: pl.BlockSpec(shape, lambda t: (0, 0), pipeline_mode=pl.Buffered(1))
    wres = lambda shape: pl.BlockSpec((None,) + shape, lambda t: (layer, 0, 0), pipeline_mode=pl.Buffered(1))
    return pl.pallas_call(
        functools.partial(_ffn_kernel, final_norm=final_norm),
        grid=(n // tm,),
        in_specs=[tile(D_MODEL), tile(GW), tile(GW), tile(GW), tile(GW),
                  pl.BlockSpec((None, 1, 6 * D_MODEL), mod_map),
                  res((1, D_MODEL)), wres((D_MODEL, D_MODEL)), wres((D_MODEL, D_FF)), wres((D_MODEL, D_FF)),
                  wres((D_FF, D_MODEL)), res((1, D_MODEL))],
        out_specs=tile(D_MODEL),
        out_shape=jax.ShapeDtypeStruct((n, D_MODEL), F32),
        compiler_params=_cparams(1),
        name="outproj_ffn",
    )(x2d, *outs, mods_l, g.reshape(1, D_MODEL), *weights, g_final.reshape(1, D_MODEL))


def _grid_pos_embed(n_tokens):
    rows = n_tokens // GRID_W
    rr, cc = np.meshgrid(np.arange(rows, dtype=np.float32), np.arange(GRID_W, dtype=np.float32), indexing="ij")
    quarter = D_MODEL // 4
    omega = (1.0 / (np.float32(POS_BASE) ** (np.arange(quarter, dtype=np.float32) / quarter))).astype(np.float32)

    def enc(pos):
        ang = pos.reshape(-1)[:, None] * omega[None, :]
        return np.concatenate([np.sin(ang), np.cos(ang)], axis=-1)

    return jnp.asarray(np.concatenate([enc(rr), enc(cc)], axis=-1).astype(np.float32))


def _pad_small(vec8, offset):
    row = jnp.zeros((1, 128), F32).at[0, offset:offset + 8].set(vec8)
    return row, vec8.reshape(8, 1)


def _pack_w_in(w):
    hy_in, ssd_in, lru_in = 768, 776, 512
    o_ssd = hy_in
    o_lru = o_ssd + ssd_in
    o_gdn = o_lru + lru_in
    w = w.astype(BF16)
    small = jnp.concatenate([w[..., o_ssd + 768:o_ssd + 776],
                             w[..., o_gdn + 1024:o_gdn + 1040],
                             jnp.zeros(w.shape[:-1] + (IN_PACK - 3096,), BF16)], axis=-1)
    return (w[..., 0:o_ssd + 768],
            w[..., o_lru:o_gdn + 1024],
            small)


def _ssd_state_in(s):
    st = jnp.transpose(s, (0, 1, 4, 2, 3))
    grp = (jnp.arange(NH) // 2)[None, :] == jnp.arange(2)[:, None]
    full = st[:, :, None] * grp[None, None, :, None, :, None].astype(s.dtype)
    return full.reshape(s.shape[0], 2, 128, GW)


def _gdn_state_in(s):
    bsz = s.shape[0]
    eye = jnp.eye(NH, dtype=s.dtype)
    full = s[:, :, :, :, None, :] * eye[None, None, :, None, :, None]
    return full.reshape(bsz, 2, GW, GW)


def kernel(x_prompt, x_sample, state_lru, state_ssd, state_gdn, c, c_ctx, w_mod, b_mod, g_mix, g_ffn, g_final, w_in, w_out, hy_conv, hy_w1, hy_b1, hy_w2, hy_b2, hy_w3, hy_freq, hy_bias, ssd_conv, ssd_dt_bias, ssd_a_log, ssd_d, ssd_norm, lru_conv, lru_w_r, lru_b_r, lru_w_i, lru_b_i, lru_lambda, gdn_conv, gdn_dt_bias, gdn_a_log, gdn_norm, w_gate, w_up, w_down):
    b_ctx, l_ctx, _ = x_prompt.shape
    b_lat, l_lat, _ = x_sample.shape

    cvec = jnp.zeros((16, D_MODEL), F32).at[0].set(c_ctx).at[1:1 + b_lat].set(c)
    mods = _mod_call(cvec, w_mod, b_mod).reshape(DEPTH, 16, 1, 6 * D_MODEL)

    hw = {"w1": jnp.zeros((DEPTH, 128, 128), F32).at[:, :HY_EMB, :HY_HIDDEN].set(hy_w1),
          "b1": jnp.zeros((DEPTH, 1, 128), F32).at[:, 0, :HY_HIDDEN].set(hy_b1),
          "w2": jnp.zeros((DEPTH, 128, 128), F32).at[:, :HY_HIDDEN, :HY_HIDDEN].set(hy_w2),
          "b2": jnp.zeros((DEPTH, 1, 128), F32).at[:, 0, :HY_HIDDEN].set(hy_b2),
          "w3": jnp.zeros((DEPTH, 128, 2 * GW), F32).at[:, :HY_HIDDEN, :].set(hy_w3),
          "freq": jnp.zeros((DEPTH, 2, 128), F32).at[:, :, :HY_HIDDEN].set(hy_freq)}
    filt = {l_ctx: _hyfilt_call(l_ctx, hw), l_lat: _hyfilt_call(l_lat, hw)}

    ones_bd = jnp.asarray(np.kron(np.eye(NH, dtype=np.float32), np.ones((HD, HD), np.float32))).astype(BF16)
    pos = _grid_pos_embed(l_lat)

    x_ctx = x_prompt.reshape(b_ctx * l_ctx, D_MODEL)
    x_lat = x_sample.reshape(b_lat * l_lat, D_MODEL)
    w_pack = _pack_w_in(w_in)
    ffn_w = tuple(w.astype(BF16) for w in (w_out, w_gate, w_up, w_down))
    lru_out, ssd_out, gdn_out = [], [], []
    for l in range(DEPTH):
        ssd_dtb_r, ssd_dtb_c = _pad_small(ssd_dt_bias[l].reshape(8), SM_DT)
        ssd_an_r, ssd_an_c = _pad_small(-jnp.exp(ssd_a_log[l].reshape(8)), SM_DT)
        ssd_p = {"conv": ssd_conv[l], "dtb_row": ssd_dtb_r, "dtb_col": ssd_dtb_c, "an_row": ssd_an_r,
                 "an_col": ssd_an_c, "dvec": jnp.repeat(ssd_d[l], HD).reshape(1, GW),
                 "gn": ssd_norm[l].reshape(1, GW)}
        eye_h = jnp.eye(NH, dtype=F32)
        blockdiag = lambda w: (w[:, :, :, None, :] * eye_h[None, :, None, :, None]).reshape(2, GW, GW)
        w_r, w_i = blockdiag(lru_w_r[l]), blockdiag(lru_w_i[l])
        lru_p = {"conv": lru_conv[l],
                 "w": jnp.concatenate([w_r[0], w_r[1], w_i[0], w_i[1]], axis=1).astype(BF16),
                 "bias": jnp.concatenate([lru_b_r[l].reshape(1, 2 * GW), lru_b_i[l].reshape(1, 2 * GW)], axis=1),
                 "sp": jax.nn.softplus(-lru_lambda[l]).reshape(1, 2 * GW)}
        gdn_dtb_r, gdn_dtb_c = _pad_small(gdn_dt_bias[l].reshape(8), SM_A)
        gdn_an_r, gdn_an_c = _pad_small(-jnp.exp(gdn_a_log[l].reshape(8)), SM_A)
        gdn_p = {"conv": gdn_conv[l], "dtb_row": gdn_dtb_r, "dtb_col": gdn_dtb_c, "an_row": gdn_an_r,
                 "an_col": gdn_an_c, "gn": jnp.tile(gdn_norm[l], NH).reshape(1, GW), "ones": ones_bd}
        last = l == DEPTH - 1

        proj = _inproj_call(x_ctx, None, mods[l], g_mix[l], w_pack, l, l_ctx, False).reshape(b_ctx, l_ctx, IN_PACK)
        o_hy = _hyena_call(proj, hy_conv[l], hy_bias[l], filt[l_ctx][0][l], filt[l_ctx][1][l], l_ctx, 4)
        o_ssd, s_ssd = _ssd_call(proj, ssd_p, None, l_ctx, True)
        o_lru, s_lru = _lru_call(proj, lru_p, None, l_ctx, True)
        o_gdn, s_gdn = _gdn_call(proj, gdn_p, None, l_ctx, True)
        outs = [o.reshape(b_ctx * l_ctx, GW) for o in (o_hy, o_ssd, o_lru, o_gdn)]
        x_ctx = _ffn_call(x_ctx, outs, mods[l], g_ffn[l], ffn_w, l, g_final, l_ctx, False, last)
        lru_out.append(s_lru)
        ssd_out.append(s_ssd)
        gdn_out.append(s_gdn)

        if l == 0:
            proj, x_lat = _inproj_call(x_lat, pos, mods[l], g_mix[l], w_pack, l, l_lat, True)
        else:
            proj = _inproj_call(x_lat, None, mods[l], g_mix[l], w_pack, l, l_lat, True)
        proj = proj.reshape(b_lat, l_lat, IN_PACK)
        o_hy = _hyena_call(proj, hy_conv[l], hy_bias[l], filt[l_lat][0][l], filt[l_lat][1][l], l_lat, 1)
        o_ssd, = _ssd_call(proj, ssd_p, _ssd_state_in(state_ssd[:, l]), l_lat, False)
        o_lru, = _lru_call(proj, lru_p, state_lru[:, l], l_lat, False)
        o_gdn, = _gdn_call(proj, gdn_p, _gdn_state_in(state_gdn[:, l]), l_lat, False)
        outs = [o.reshape(b_lat * l_lat, GW) for o in (o_hy, o_ssd, o_lru, o_gdn)]
        x_lat = _ffn_call(x_lat, outs, mods[l], g_ffn[l], ffn_w, l, g_final, l_lat, True, last)

    return (x_ctx.reshape(b_ctx, l_ctx, D_MODEL), x_lat.reshape(b_lat, l_lat, D_MODEL),
            jnp.stack(lru_out, axis=1), jnp.stack(ssd_out, axis=1), jnp.stack(gdn_out, axis=1))
```

```python
import functools
import math

import ml_dtypes
import numpy as np
import jax
import jax.numpy as jnp
from jax import lax
from jax.experimental import pallas as pl
from jax.experimental.pallas import tpu as pltpu

F32 = jnp.float32
BF16 = jnp.bfloat16
HI = lax.Precision.HIGHEST

D_MODEL = 1024
DEPTH = 2
EPS = 1e-6
GRID_W = 64
POS_BASE = 10000.0
GW = 256
D_FF = 2816
HD = 64
NH = 4
HY_BANDS = 16
HY_EMB = 1 + 2 * HY_BANDS
HY_HIDDEN = 64
LRU_C = 8.0

COL_HY = 0
COL_SSD = 768
COL_LRU = 1536
COL_GDN = 2048
COL_SMALL = 3072
IN_PACK = 3200
SM_DT, SM_B, SM_A = 0, 8, 16

SSD_Q = 128
GDN_C = 128
ROW_TILE = 512
V7X_MXU_DIM = 256
FF_SPLITS = ((0, 6 * V7X_MXU_DIM), (6 * V7X_MXU_DIM, D_FF))
V7X_VMEM_LIMIT = 56 * 1024 * 1024


def _cparams(n_axes=1):
    return pltpu.CompilerParams(dimension_semantics=("arbitrary",) * n_axes,
                                vmem_limit_bytes=V7X_VMEM_LIMIT)


def _silu(x):
    return x * jax.nn.sigmoid(x)


def _softplus(x):
    return jnp.maximum(x, 0.0) + jnp.log1p(jnp.exp(-jnp.abs(x)))


def _gelu_tanh(x):
    return 0.5 * x * (1.0 + jnp.tanh(math.sqrt(2.0 / math.pi) * (x + 0.044715 * (x * x * x))))


def _split_bf16(x):
    hi = x.astype(BF16)
    lo = (x - hi.astype(F32)).astype(BF16)
    return hi, lo


def _dot(a, b):
    return jnp.dot(a, b, preferred_element_type=F32)


def _dot_hi(a, b):
    return jnp.dot(a, b, precision=HI, preferred_element_type=F32)


def _cumsum_rows(tri_b, x):
    x1 = x.astype(BF16)
    r = x - x1.astype(F32)
    x2 = r.astype(BF16)
    x3 = (r - x2.astype(F32)).astype(BF16)
    return _dot(tri_b, x1) + _dot(tri_b, x2) + _dot(tri_b, x3)


def _cumsum_cols(x, tri_b):
    x1 = x.astype(BF16)
    r = x - x1.astype(F32)
    x2 = r.astype(BF16)
    x3 = (r - x2.astype(F32)).astype(BF16)
    return _dot(x1, tri_b) + _dot(x2, tri_b) + _dot(x3, tri_b)


def _dot3_const(ah, al, x):
    xh, xl = _split_bf16(x)
    return _dot(ah, xh) + _dot(ah, xl) + _dot(al, xh)


def _seg_sum(x, ones_bd):
    xh, xl = _split_bf16(x)
    return _dot(xh, ones_bd) + _dot(xl, ones_bd)


def _lane_iota(shape):
    return lax.broadcasted_iota(jnp.int32, shape, len(shape) - 1)


def _head_lanes(cols, rows):
    lane = _lane_iota((rows, GW))
    out = jnp.broadcast_to(cols[NH - 1], (rows, GW))
    for h in range(NH - 2, -1, -1):
        out = jnp.where(lane < (h + 1) * HD, cols[h], out)
    return out


def _dwconv(x, w_ref, pad_ref, n_rows, taps):
    ch = x.shape[1]
    left = taps // 2
    pad_ref[0:8, 0:ch] = jnp.zeros((8, ch), F32)
    pad_ref[8 + n_rows:16 + n_rows, 0:ch] = jnp.zeros((8, ch), F32)
    pad_ref[8:8 + n_rows, 0:ch] = x
    acc = None
    for j in range(taps):
        off = 8 - left + j
        term = pad_ref[off:off + n_rows, 0:ch] * w_ref[j:j + 1, :]
        acc = term if acc is None else acc + term
    return acc


def _mod_kernel(c_ref, w_ref, b_ref, o_ref):
    s_hi, s_lo = _split_bf16(_silu(c_ref[...]))
    w_hi, w_lo = _split_bf16(w_ref[...])
    o_ref[...] = _dot(s_hi, w_hi) + _dot(s_hi, w_lo) + _dot(s_lo, w_hi) + b_ref[...]


def _mod_call(cvec, w_mod, b_mod):
    tn = 1536
    n = w_mod.shape[-1]
    return pl.pallas_call(
        _mod_kernel,
        grid=(DEPTH, n // tn),
        in_specs=[pl.BlockSpec((16, D_MODEL), lambda l, j: (0, 0)),
                  pl.BlockSpec((None, D_MODEL, tn), lambda l, j: (l, 0, j)),
                  pl.BlockSpec((None, 1, tn), lambda l, j: (l, 0, j))],
        out_specs=pl.BlockSpec((None, 16, tn), lambda l, j: (l, 0, j)),
        out_shape=jax.ShapeDtypeStruct((DEPTH, 16, n), F32),
        compiler_params=_cparams(2),
        name="adaln_mod",
    )(cvec, w_mod, b_mod.reshape(DEPTH, 1, n))


def _inproj_kernel(*refs, add_pos):
    if add_pos:
        x_ref, pos_ref, mod_ref, g_ref, wa_ref, wb_ref, wc_ref, o_ref, x0_ref = refs
        x = x_ref[...] + pos_ref[...]
        x0_ref[...] = x
    else:
        x_ref, mod_ref, g_ref, wa_ref, wb_ref, wc_ref, o_ref = refs
        x = x_ref[...]
    mod = mod_ref[...]
    sh = mod[:, 0:D_MODEL]
    sc = mod[:, D_MODEL:2 * D_MODEL]
    h = x * lax.rsqrt(jnp.mean(x * x, axis=-1, keepdims=True) + EPS) * g_ref[...]
    h = (h * (1.0 + sc) + sh).astype(BF16)
    o_ref[:, 0:COL_LRU] = _dot(h, wa_ref[...])
    o_ref[:, COL_LRU:COL_SMALL] = _dot(h, wb_ref[...])
    o_ref[:, COL_SMALL:IN_PACK] = _dot(h, wc_ref[...])


def _inproj_call(x2d, pos, mods_l, g, w_pack, layer, seq_len, is_lat):
    n = x2d.shape[0]
    tm = ROW_TILE
    per_seq = max(seq_len // tm, 1)
    mod_map = (lambda t: (1 + t // per_seq, 0, 0)) if is_lat else (lambda t: (0, 0, 0))
    in_specs = [pl.BlockSpec((tm, D_MODEL), lambda t: (t, 0))]
    args = [x2d]
    if pos is not None:
        in_specs.append(pl.BlockSpec((tm, D_MODEL), lambda t: (t % per_seq, 0)))
        args.append(pos)
    in_specs += [pl.BlockSpec((None, 1, 6 * D_MODEL), mod_map),
                 pl.BlockSpec((1, D_MODEL), lambda t: (0, 0)),
                 pl.BlockSpec((None, D_MODEL, COL_LRU), lambda t: (layer, 0, 0)),
                 pl.BlockSpec((None, D_MODEL, COL_SMALL - COL_LRU), lambda t: (layer, 0, 0)),
                 pl.BlockSpec((None, D_MODEL, IN_PACK - COL_SMALL), lambda t: (layer, 0, 0))]
    args += [mods_l, g.reshape(1, D_MODEL), *w_pack]
    out_specs = pl.BlockSpec((tm, IN_PACK), lambda t: (t, 0))
    out_shape = jax.ShapeDtypeStruct((n, IN_PACK), F32)
    if pos is not None:
        out_specs = [out_specs, pl.BlockSpec((tm, D_MODEL), lambda t: (t, 0))]
        out_shape = [out_shape, jax.ShapeDtypeStruct((n, D_MODEL), F32)]
    return pl.pallas_call(
        functools.partial(_inproj_kernel, add_pos=pos is not None),
        grid=(n // tm,),
        in_specs=in_specs,
        out_specs=out_specs,
        out_shape=out_shape,
        compiler_params=_cparams(1),
        name="inproj_lat0" if pos is not None else "inproj",
    )(*args)


@functools.lru_cache(maxsize=None)
def _dft_tables(seq_len):
    n = 2 * seq_len - 1
    idx = np.arange(seq_len, dtype=np.int64)
    ang = (2.0 * np.pi / n) * ((idx[:, None] * idx[None, :]) % n).astype(np.float64)
    out = []
    for tab in (np.cos(ang), np.sin(ang)):
        hi = tab.astype(ml_dtypes.bfloat16)
        lo = (tab - hi.astype(np.float64)).astype(ml_dtypes.bfloat16)
        out += [hi, lo]
    return tuple(out)


@functools.lru_cache(maxsize=None)
def _hyena_consts(seq_len):
    t = np.linspace(0.0, 1.0, seq_len, dtype=np.float32)[:, None]
    w = np.float32(2.0 * math.pi / seq_len) * np.arange(seq_len, dtype=np.float32)[:, None]
    bands = np.linspace(1e-4, HY_BANDS - 1, HY_BANDS, dtype=np.float32)[None, :]
    feats = np.concatenate([t, np.cos(bands * w), -np.sin(bands * w)], axis=-1).astype(np.float32)
    feats_p = np.zeros((seq_len, 128), np.float32)
    feats_p[:, :HY_EMB] = feats
    max_decay = math.log(1e-2) / 0.3
    min_decay = math.log(1e-2) / 1.5
    deltas = np.abs(np.linspace(min_decay, max_decay, GW, dtype=np.float32))
    decay = np.exp(-t * deltas).astype(np.float32)
    return feats_p, np.concatenate([decay, decay], axis=1)


def _hyfilt_kernel(feat_ref, dec_ref, w1_ref, b1_ref, w2_ref, b2_ref, w3_ref, fr_ref,
                   ch_ref, cl_ref, sh_ref, sl_ref, hr_ref, hi_ref, *, seq_len):
    fr = fr_ref[...]
    h = jnp.sin(fr[0:1, :] * (_dot_hi(feat_ref[...], w1_ref[...]) + b1_ref[...]))
    h = jnp.sin(fr[1:2, :] * (_dot_hi(h, w2_ref[...]) + b2_ref[...]))
    h = _dot_hi(h, w3_ref[...]) * dec_ref[...]
    den = jnp.sum(jnp.abs(h[:, :GW]) + jnp.abs(h[:, GW:]), axis=0, keepdims=True)
    h = h / jnp.concatenate([den, den], axis=1)
    hc = _dot3_const(ch_ref[...], cl_ref[...], h)
    hs = _dot3_const(sh_ref[...], sl_ref[...], h)
    n = 2 * seq_len - 1
    row = lax.broadcasted_iota(jnp.int32, (seq_len, GW), 0)
    wgt = jnp.where(row == 0, 1.0 / n, 2.0 / n)
    hr_ref[...] = (hc[:, :GW] + hc[:, GW:]) * wgt
    hi_ref[...] = (hs[:, GW:] - hs[:, :GW]) * wgt


def _hyfilt_call(seq_len, hw):
    feats, decay = _hyena_consts(seq_len)
    tabs = _dft_tables(seq_len)
    full = lambda shape: pl.BlockSpec(shape, lambda l: (0,) * len(shape))
    lay = lambda shape: pl.BlockSpec((None,) + shape, lambda l: (l,) + (0,) * len(shape))
    return pl.pallas_call(
        functools.partial(_hyfilt_kernel, seq_len=seq_len),
        grid=(DEPTH,),
        in_specs=[full((seq_len, 128)), full((seq_len, 2 * GW)),
                  lay((128, 128)), lay((1, 128)), lay((128, 128)), lay((1, 128)), lay((128, 2 * GW)),
                  lay((2, 128))] + [full((seq_len, seq_len))] * 4,
        out_specs=[lay((seq_len, GW)), lay((seq_len, GW))],
        out_shape=[jax.ShapeDtypeStruct((DEPTH, seq_len, GW), F32)] * 2,
        compiler_params=_cparams(1),
        name=f"hyena_filter_{seq_len}",
    )(feats, decay, hw["w1"], hw["b1"], hw["w2"], hw["b2"], hw["w3"], hw["freq"], *tabs)


def _hyena_kernel(u_ref, cw_ref, bias_ref, hr_ref, hi_ref, ch_ref, cl_ref, sh_ref, sl_ref,
                  o_ref, pad_ref, *, seq_len, n_seq):
    for b in range(n_seq):
        uc = _dwconv(u_ref[b], cw_ref, pad_ref, seq_len, 3)
        x0 = uc[:, 0:GW]
        z = uc[:, 2 * GW:3 * GW] * uc[:, GW:2 * GW]
        za = _dot3_const(ch_ref[...], cl_ref[...], z)
        zb = _dot3_const(sh_ref[...], sl_ref[...], z)
        hr = hr_ref[...]
        hi = hi_ref[...]
        yr = za * hr + zb * hi
        yi = za * hi - zb * hr
        y = _dot3_const(ch_ref[...], cl_ref[...], yr) - _dot3_const(sh_ref[...], sl_ref[...], yi)
        o_ref[b] = x0 * (y + z * bias_ref[...])


def _hyena_call(proj3, conv_w, bias, hr, hi, seq_len, n_seq):
    bsz = proj3.shape[0]
    tabs = _dft_tables(seq_len)
    full = lambda shape: pl.BlockSpec(shape, lambda b: (0,) * len(shape))
    return pl.pallas_call(
        functools.partial(_hyena_kernel, seq_len=seq_len, n_seq=n_seq),
        grid=(bsz // n_seq,),
        in_specs=[pl.BlockSpec((n_seq, seq_len, 3 * GW), lambda b: (b, 0, COL_HY // (3 * GW))),
                  full((3, 3 * GW)), full((1, GW)), full((seq_len, GW)), full((seq_len, GW))]
                 + [full((seq_len, seq_len))] * 4,
        out_specs=pl.BlockSpec((n_seq, seq_len, GW), lambda b: (b, 0, 0)),
        out_shape=jax.ShapeDtypeStruct((bsz, seq_len, GW), F32),
        scratch_shapes=[pltpu.VMEM((seq_len + 16, 3 * GW), F32)],
        compiler_params=_cparams(1),
        name=f"hyena_{seq_len}",
    )(proj3, conv_w, bias.reshape(1, GW), hr, hi, *tabs)


def _tri_masks(q):
    row = lax.broadcasted_iota(jnp.int32, (q, q), 0)
    col = lax.broadcasted_iota(jnp.int32, (q, q), 1)
    return col <= row, col >= row


def _ssd_kernel(*refs, seq_len, has_s0, emit_state):
    it = iter(refs)
    zx_ref, sm_ref, conv_ref, dtb_r_ref, dtb_c_ref, an_r_ref, an_c_ref, dvec_ref, gn_ref = (
        next(it) for _ in range(9))
    s0_ref = next(it) if has_s0 else None
    o_ref = next(it)
    so_ref = next(it) if emit_state else None
    pad_ref, xs_ref, cm_ref, ac_ref, bt_ref, ar_ref, dtr_ref, y_ref, st_ref, upd_ref, ecs_ref = it
    q = SSD_Q
    nc = seq_len // q

    xbc = _silu(_dwconv(zx_ref[:, GW:3 * GW], conv_ref, pad_ref, seq_len, 4))
    xs_ref[...] = xbc[:, 0:GW]
    cm_ref[...] = xbc[:, GW + 128:GW + 256].astype(BF16)
    y_ref[...] = xbc[:, 0:GW] * dvec_ref[...]
    bm_t = xbc[:, GW:GW + 128].T.astype(BF16)
    sm = sm_ref[...]
    ac_ref[...] = _softplus(sm + dtb_r_ref[...]) * an_r_ref[...]
    sm_t = sm.T
    dtr = _softplus(sm_t[SM_DT:SM_DT + 8, :] + dtb_c_ref[...])
    ar = dtr * an_c_ref[...]
    for c in range(nc):
        bt_ref[c] = bm_t[:, c * q:(c + 1) * q]
        ar_ref[c] = ar[:, c * q:(c + 1) * q]
        dtr_ref[c] = dtr[:, c * q:(c + 1) * q]
    if has_s0:
        st_ref[...] = s0_ref[...]
    else:
        st_ref[...] = jnp.zeros(st_ref.shape, F32)

    lower, upper = _tri_masks(q)
    lo_f = lower.astype(F32).astype(BF16)
    up_f = upper.astype(F32).astype(BF16)
    lane128 = _lane_iota((q, 128))
    lane256 = _lane_iota((q, GW))
    st_row = lax.broadcasted_iota(jnp.int32, (128, GW), 0)
    st_col = lax.broadcasted_iota(jnp.int32, (128, GW), 1)
    st_mask = (st_row // HD) == (st_col // 128)
    dirs = (0, 1)
    tri_c = (lo_f, up_f)
    tri_r = (up_f, lo_f)
    masks = (lower, upper)
    last = (q - 1, 0)
    spread = [((st_row == d * NH + st_col // HD)).astype(F32).astype(BF16) for d in dirs]

    def chunk_body(ci, carry):
        rows = pl.ds(pl.multiple_of(ci * q, q), q)
        xs_b = xs_ref[rows, :].astype(BF16)
        c_c = cm_ref[rows, :]
        b_t = bt_ref[ci]
        a_c = ac_ref[rows, :]
        a_r = ar_ref[ci]
        dt_r = dtr_ref[ci]
        stack = jnp.concatenate(
            [jnp.where((lane256 // HD) == h, xs_b, jnp.zeros_like(xs_b)) for h in range(NH)], axis=0)
        c_g = jnp.concatenate([jnp.where(lane128 < HD, c_c, jnp.zeros_like(c_c)),
                               jnp.where(lane128 >= HD, c_c, jnp.zeros_like(c_c))], axis=0)
        g_mat = _dot(c_g, b_t)
        cum_c = [_cumsum_rows(tri_c[d], a_c) for d in dirs]
        cum_r = [_cumsum_cols(a_r, tri_r[d]) for d in dirs]
        w_cat, bw_cat = [], []
        for d in dirs:
            w_parts, bw_parts = [], []
            for h in range(NH):
                j = d * NH + h
                g = h // 2
                diff = cum_c[d][:, j:j + 1] - cum_r[d][j:j + 1, :]
                m = jnp.where(masks[d], jnp.exp(jnp.minimum(diff, 0.0)), 0.0)
                w_parts.append((g_mat[g * q:(g + 1) * q, :] * m * dt_r[j:j + 1, :]).astype(BF16))
                to_end = jnp.exp(cum_r[d][j:j + 1, last[d]:last[d] + 1] - cum_r[d][j:j + 1, :])
                bw_parts.append((b_t.astype(F32) * (dt_r[j:j + 1, :] * to_end)).astype(BF16))
            w_cat.append(jnp.concatenate(w_parts, axis=1))
            bw_cat.append(jnp.concatenate(bw_parts, axis=1))
        y_diag = [_dot(w_cat[d], stack) for d in dirs]
        upd = [_dot(bw_cat[d], stack) for d in dirs]
        ecs = []
        for d in dirs:
            e_hi, e_lo = _split_bf16(jnp.exp(cum_c[d]))
            ecs.append(_dot(e_hi, spread[d]) + _dot(e_lo, spread[d]))
        y_ref[rows, :] += y_diag[0] + y_diag[1]
        for d in dirs:
            upd_ref[d, ci] = jnp.where(st_mask, upd[d], 0.0)
            ecs_ref[d, ci] = ecs[d]
        return carry

    lax.fori_loop(0, nc, chunk_body, 0)

    def state_body(i, carry):
        cis = (i, nc - 1 - i)
        rows = [pl.ds(pl.multiple_of(ci * q, q), q) for ci in cis]
        ss = [st_ref[d] for d in dirs]
        y_off = [_dot(cm_ref[rows[d], :], ss[d].astype(BF16)) for d in dirs]
        for d in dirs:
            ecs = ecs_ref[d, cis[d]]
            y_ref[rows[d], :] += y_off[d] * ecs
            st_ref[d] = ss[d] * ecs[last[d]:last[d] + 1, :] + upd_ref[d, cis[d]]
        return carry

    lax.fori_loop(0, nc, state_body, 0)

    y = y_ref[...] * _silu(zx_ref[:, 0:GW])
    y = y * lax.rsqrt(jnp.mean(y * y, axis=-1, keepdims=True) + EPS) * gn_ref[...]
    o_ref[...] = y
    if emit_state:
        for d in range(2):
            st_t = st_ref[d].T
            for h in range(NH):
                blk = st_t[h * HD:(h + 1) * HD, :]
                if h // 2 == 1:
                    blk = pltpu.roll(blk, HD, axis=1)
                so_ref[d, h] = blk[:, 0:HD]


def _ssd_call(proj3, p, s0, seq_len, emit_state):
    bsz = proj3.shape[0]
    nc = seq_len // SSD_Q
    full = lambda shape: pl.BlockSpec(shape, lambda b: (0,) * len(shape))
    in_specs = [pl.BlockSpec((None, seq_len, 3 * GW), lambda b: (b, 0, COL_SSD // (3 * GW))),
                pl.BlockSpec((None, seq_len, 128), lambda b: (b, 0, COL_SMALL // 128)),
                full((4, 2 * GW)), full((1, 128)), full((8, 1)), full((1, 128)), full((8, 1)),
                full((1, GW)), full((1, GW))]
    args = [proj3, proj3, p["conv"], p["dtb_row"], p["dtb_col"], p["an_row"], p["an_col"], p["dvec"], p["gn"]]
    if s0 is not None:
        in_specs.append(pl.BlockSpec((None, 2, 128, GW), lambda b: (b, 0, 0, 0)))
        args.append(s0)
    out_specs = [pl.BlockSpec((None, seq_len, GW), lambda b: (b, 0, 0))]
    out_shape = [jax.ShapeDtypeStruct((bsz, seq_len, GW), F32)]
    if emit_state:
        out_specs.append(pl.BlockSpec((None, 2, NH, HD, HD), lambda b: (b, 0, 0, 0, 0)))
        out_shape.append(jax.ShapeDtypeStruct((bsz, 2, NH, HD, HD), F32))
    scratch = [pltpu.VMEM((seq_len + 16, 2 * GW), F32),
               pltpu.VMEM((seq_len, GW), F32),
               pltpu.VMEM((seq_len, 128), BF16),
               pltpu.VMEM((seq_len, 128), F32),
               pltpu.VMEM((nc, 128, SSD_Q), BF16),
               pltpu.VMEM((nc, 8, SSD_Q), F32),
               pltpu.VMEM((nc, 8, SSD_Q), F32),
               pltpu.VMEM((seq_len, GW), F32),
               pltpu.VMEM((2, 128, GW), F32),
               pltpu.VMEM((2, nc, 128, GW), F32),
               pltpu.VMEM((2, nc, SSD_Q, GW), F32)]
    return pl.pallas_call(
        functools.partial(_ssd_kernel, seq_len=seq_len, has_s0=s0 is not None, emit_state=emit_state),
        grid=(bsz,),
        in_specs=in_specs,
        out_specs=out_specs,
        out_shape=out_shape,
        scratch_shapes=scratch,
        compiler_params=_cparams(1),
        name=f"ssd_{seq_len}",
    )(*args)


def _lru_kernel(*refs, seq_len, has_s0, emit_state):
    it = iter(refs)
    u_ref, conv_ref, w_ref, bias_ref, sp_ref = (next(it) for _ in range(5))
    s0_ref = next(it) if has_s0 else None
    o_ref = next(it)
    so_ref = next(it) if emit_state else None
    pad_ref, af_ref, bf_ref, ab_ref, bb_ref = it
    n = seq_len

    xc = _dwconv(u_ref[:, 0:GW], conv_ref, pad_ref, n, 4)
    ri = jax.nn.sigmoid(_dot(xc.astype(BF16), w_ref[...]) + bias_ref[...])
    log_a = -LRU_C * ri[:, 0:2 * GW] * sp_ref[...]
    a = jnp.exp(log_a)
    bt = jnp.sqrt(jnp.maximum(1.0 - a * a, 0.0)) * ri[:, 2 * GW:4 * GW] * jnp.concatenate([xc, xc], axis=1)

    ones8 = jnp.ones((8, GW), F32)
    zeros8 = jnp.zeros((8, GW), F32)
    for a_ref_, b_ref_, lo in ((af_ref, bf_ref, 0), (ab_ref, bb_ref, GW)):
        a_ref_[0:8, :] = ones8
        a_ref_[8 + n:16 + n, :] = ones8
        b_ref_[0:8, :] = zeros8
        b_ref_[8 + n:16 + n, :] = zeros8
        a_ref_[8:8 + n, :] = a[:, lo:lo + GW]
        b_ref_[8:8 + n, :] = bt[:, lo:lo + GW]
    if has_s0:
        s0 = s0_ref[...]
        bf_ref[8:9, :] = bf_ref[8:9, :] + af_ref[8:9, :] * s0[0:1, :]
        bb_ref[7 + n:8 + n, :] = bb_ref[7 + n:8 + n, :] + ab_ref[7 + n:8 + n, :] * s0[1:2, :]

    d = 1
    while d < n:
        if d < 8:
            a_cur = af_ref[8:8 + n, :]
            af_ref[8:8 + n, :] = a_cur * af_ref[8 - d:8 - d + n, :]
            bf_ref[8:8 + n, :] = a_cur * bf_ref[8 - d:8 - d + n, :] + bf_ref[8:8 + n, :]
            a_cur = ab_ref[8:8 + n, :]
            ab_ref[8:8 + n, :] = a_cur * ab_ref[8 + d:8 + d + n, :]
            bb_ref[8:8 + n, :] = a_cur * bb_ref[8 + d:8 + d + n, :] + bb_ref[8:8 + n, :]
        else:
            m = n - d
            a_cur = af_ref[8 + d:8 + n, :]
            a_new = a_cur * af_ref[8:8 + m, :]
            b_new = a_cur * bf_ref[8:8 + m, :] + bf_ref[8 + d:8 + n, :]
            af_ref[8 + d:8 + n, :] = a_new
            bf_ref[8 + d:8 + n, :] = b_new
            a_cur = ab_ref[8:8 + m, :]
            a_new = a_cur * ab_ref[8 + d:8 + n, :]
            b_new = a_cur * bb_ref[8 + d:8 + n, :] + bb_ref[8:8 + m, :]
            ab_ref[8:8 + m, :] = a_new
            bb_ref[8:8 + m, :] = b_new
        d *= 2

    y = (bf_ref[8:8 + n, :] + bb_ref[8:8 + n, :]) * _gelu_tanh(u_ref[:, GW:2 * GW])
    o_ref[...] = y
    if emit_state:
        so_ref[0:1, :] = bf_ref[7 + n:8 + n, :]
        so_ref[1:2, :] = bb_ref[8:9, :]


def _lru_call(proj3, p, s0, seq_len, emit_state):
    bsz = proj3.shape[0]
    full = lambda shape: pl.BlockSpec(shape, lambda b: (0,) * len(shape))
    in_specs = [pl.BlockSpec((None, seq_len, 2 * GW), lambda b: (b, 0, COL_LRU // (2 * GW))),
                full((4, GW)), full((GW, 4 * GW)), full((1, 4 * GW)), full((1, 2 * GW))]
    args = [proj3, p["conv"], p["w"], p["bias"], p["sp"]]
    if s0 is not None:
        in_specs.append(pl.BlockSpec((None, 2, GW), lambda b: (b, 0, 0)))
        args.append(s0)
    out_specs = [pl.BlockSpec((None, seq_len, GW), lambda b: (b, 0, 0))]
    out_shape = [jax.ShapeDtypeStruct((bsz, seq_len, GW), F32)]
    if emit_state:
        out_specs.append(pl.BlockSpec((None, 2, GW), lambda b: (b, 0, 0)))
        out_shape.append(jax.ShapeDtypeStruct((bsz, 2, GW), F32))
    scratch = [pltpu.VMEM((seq_len + 16, GW), F32)] * 5
    return pl.pallas_call(
        functools.partial(_lru_kernel, seq_len=seq_len, has_s0=s0 is not None, emit_state=emit_state),
        grid=(bsz,),
        in_specs=in_specs,
        out_specs=out_specs,
        out_shape=out_shape,
        scratch_shapes=scratch,
        compiler_params=_cparams(1),
        name=f"rglru_{seq_len}",
    )(*args)


TRI_BASE = 8


def _unit_tri_inverse(a_mats, lower_flags, eye, same_blk):
    c = a_mats[0].shape[0]
    ds = [jnp.where(same_blk[TRI_BASE], a, 0.0) for a in a_mats]
    d_bs = [d.astype(BF16) for d in ds]
    ps = [eye - d for d in ds]
    x_bs = [_dot(d_b, d_b).astype(BF16) for d_b in d_bs]
    rs = [_dot(x_b, jnp.concatenate([p.astype(BF16), x_b], axis=1)) for x_b, p in zip(x_bs, ps)]
    ps = [p + r[:, 0:c] for p, r in zip(ps, rs)]
    ts = [p + _dot(r[:, c:2 * c].astype(BF16), p.astype(BF16)) for p, r in zip(ps, rs)]
    size = 2 * TRI_BASE
    while size <= c:
        half = size // 2
        off = same_blk[size] & jnp.logical_not(same_blk[half])
        e_bs = [jnp.where(off, a, 0.0).astype(BF16) for a in a_mats]
        t_bs = [t.astype(BF16) for t in ts]
        starts = [[blk * size + (half if low else 0) for blk in range(c // size)] for low in lower_flags]
        rows = [jnp.concatenate([t[r0:r0 + half, :] for r0 in st], axis=0).astype(BF16)
                for t, st in zip(ts, starts)]
        us = [_dot(r, e_b).astype(BF16) for r, e_b in zip(rows, e_bs)]
        upds = [_dot(u, t_b) for u, t_b in zip(us, t_bs)]
        new_ts = []
        for t, upd, st, low in zip(ts, upds, starts, lower_flags):
            pieces = []
            for n, r0 in enumerate(st):
                changed = t[r0:r0 + half, :] - upd[n * half:(n + 1) * half, :]
                kept = t[r0 - half:r0, :] if low else t[r0 + half:r0 + size, :]
                pieces += [kept, changed] if low else [changed, kept]
            new_ts.append(jnp.concatenate(pieces, axis=0))
        ts = new_ts
        size *= 2
    return ts


def _gdn_kernel(*refs, seq_len, has_s0, emit_state):
    it = iter(refs)
    u_ref, sm_ref, conv_ref, dtb_r_ref, dtb_c_ref, an_r_ref, an_c_ref, gn_ref, ones_ref = (
        next(it) for _ in range(9))
    s0_ref = next(it) if has_s0 else None
    o_ref = next(it)
    so_ref = next(it) if emit_state else None
    (pad_ref, q_ref, k_ref, v_ref, be_ref, gc_ref, gr_ref, oa_ref, st_ref,
     us_ref, wq_ref, at_ref, kd_ref, dr_ref) = it
    c = GDN_C
    nc = seq_len // c
    ones_bd = ones_ref[...]

    qkv = _silu(_dwconv(u_ref[:, 0:3 * GW], conv_ref, pad_ref, seq_len, 4))
    qq = qkv[:, 0:GW]
    kk = qkv[:, GW:2 * GW]
    q_ref[...] = qq * lax.rsqrt(_seg_sum(qq * qq, ones_bd) + EPS) * (HD ** -0.5)
    k_ref[...] = kk * lax.rsqrt(_seg_sum(kk * kk, ones_bd) + EPS)
    v_ref[...] = qkv[:, 2 * GW:3 * GW]
    sm = sm_ref[...]
    be_ref[...] = jax.nn.sigmoid(sm)
    gc_ref[...] = an_r_ref[...] * _softplus(sm + dtb_r_ref[...])
    sm_t = sm.T
    g_r = an_c_ref[...] * _softplus(sm_t[SM_A:SM_A + 8, :] + dtb_c_ref[...])
    for i in range(nc):
        gr_ref[i] = g_r[:, i * c:(i + 1) * c]
    oa_ref[...] = jnp.zeros(oa_ref.shape, F32)
    if has_s0:
        st_ref[...] = s0_ref[...]
    else:
        st_ref[...] = jnp.zeros(st_ref.shape, F32)

    lower, upper = _tri_masks(c)
    lo_f = lower.astype(F32).astype(BF16)
    up_f = upper.astype(F32).astype(BF16)
    row_i =lax.broadcasted_iota(jnp.int32, (c, c), 0)
    col_i = lax.broadcasted_iota(jnp.int32, (c, c), 1)
    eye = (row_i == col_i).astype(F32)
    same_blk = {}
    size = TRI_BASE
    while size <= c:
        same_blk[size] = (row_i // size) == (col_i // size)
        size *= 2
    lane256 = _lane_iota((c, GW))
    lane512 = _lane_iota((c, 2 * GW))
    bd_row = lax.broadcasted_iota(jnp.int32, (GW, GW), 0)
    bd_col = lax.broadcasted_iota(jnp.int32, (GW, GW), 1)
    bd_mask = (bd_row // HD) == (bd_col // HD)
    kt_row = lax.broadcasted_iota(jnp.int32, (GW, c), 0)

    def stack_heads(x_b, lane):
        zero = jnp.zeros_like(x_b)
        return jnp.concatenate([jnp.where(((lane % GW) // HD) == h, x_b, zero) for h in range(NH)], axis=0)

    dirs = (0, 1)
    tri_c = (lo_f, up_f)
    tri_r = (up_f, lo_f)
    incl = (lower, upper)
    strict = (col_i < row_i, col_i > row_i)
    last = (c - 1, 0)

    chunks_per_step = 2

    def solve_body(step, carry):
        cks = [step * chunks_per_step + n for n in range(chunks_per_step)]
        units = [(n, d) for n in range(chunks_per_step) for d in dirs]
        rows = [pl.ds(pl.multiple_of(ci * c, c), c) for ci in cks]
        q_c = [q_ref[r, :] for r in rows]
        k_c = [k_ref[r, :] for r in rows]
        v_c = [v_ref[r, :] for r in rows]
        be = [be_ref[r, :] for r in rows]
        g_c = [gc_ref[r, :] for r in rows]
        g_r = [gr_ref[ci] for ci in cks]
        cum_c = {(n, d): _cumsum_rows(tri_c[d], g_c[n]) for n, d in units}
        cum_r = {(n, d): _cumsum_cols(g_r[n], tri_r[d]) for n, d in units}
        kq = []
        for n in range(chunks_per_step):
            k_t = k_c[n].T.astype(BF16)
            k_t_heads = jnp.concatenate(
                [jnp.where((kt_row // HD) == h, k_t, jnp.zeros_like(k_t)) for h in range(NH)], axis=1)
            kq.append(_dot(jnp.concatenate([k_c[n], q_c[n]], axis=0).astype(BF16), k_t_heads))
        a_mats, attn, flags = [], {}, []
        for n, d in units:
            for h in range(NH):
                j = d * NH + h
                diff = cum_c[n, d][:, SM_A + j:SM_A + j + 1] - cum_r[n, d][j:j + 1, :]
                dec = jnp.where(incl[d], jnp.exp(jnp.minimum(diff, 0.0)), 0.0)
                a_mats.append(jnp.where(
                    strict[d], kq[n][0:c, h * c:(h + 1) * c] * be[n][:, SM_B + j:SM_B + j + 1] * dec, 0.0))
                attn[n, d, h] = (kq[n][c:2 * c, h * c:(h + 1) * c] * dec).astype(BF16)
                flags.append(d == 0)
        t_mats = _unit_tri_inverse(a_mats, flags, eye, same_blk)
        t_cat = {u: jnp.concatenate([t.astype(BF16) for t in t_mats[idx * NH:(idx + 1) * NH]], axis=1)
                 for idx, u in enumerate(units)}
        cols = {(n, d): [cum_c[n, d][:, SM_A + d * NH + h:SM_A + d * NH + h + 1] for h in range(NH)]
                for n, d in units}
        lasts = {(n, d): [cum_c[n, d][last[d]:last[d] + 1, SM_A + d * NH + h:SM_A + d * NH + h + 1]
                          for h in range(NH)] for n, d in units}
        b_l = {(n, d): _head_lanes([be[n][:, SM_B + d * NH + h:SM_B + d * NH + h + 1] for h in range(NH)], c)
               for n, d in units}
        e_l = {(n, d): _head_lanes([jnp.exp(cc) for cc in cols[n, d]], c) for n, d in units}
        rhs = {(n, d): jnp.concatenate([v_c[n] * b_l[n, d], k_c[n] * b_l[n, d] * e_l[n, d]], axis=1)
               for n, d in units}
        uw = {u: _dot(t_cat[u], stack_heads(rhs[u].astype(BF16), lane512)) for u in units}
        for n, d in units:
            ci = cks[n]
            us_ref[d, ci] = uw[n, d][:, 0:GW]
            wq_ref[d, ci, 0:c, :] = uw[n, d][:, GW:2 * GW].astype(BF16)
            wq_ref[d, ci, c:2 * c, :] = (q_c[n] * e_l[n, d]).astype(BF16)
            at_ref[d, ci] = jnp.concatenate([attn[n, d, h] for h in range(NH)], axis=1)
            k_dec = k_c[n] * _head_lanes([jnp.exp(lasts[n, d][h] - cols[n, d][h]) for h in range(NH)], c)
            kd_ref[d, ci] = k_dec.T.astype(BF16)
            dr_ref[d, ci] = _head_lanes([jnp.exp(lv) for lv in lasts[n, d]], 1)
        return carry

    lax.fori_loop(0, nc // chunks_per_step, solve_body, 0)

    def state_body(i, carry):
        cis = (i, nc - 1 - i)
        ss = [st_ref[d] for d in dirs]
        r2 = [_dot(wq_ref[d, cis[d]], ss[d].astype(BF16)) for d in dirs]
        v_new_b = [(us_ref[d, cis[d]] - r2[d][0:c, :]).astype(BF16) for d in dirs]
        upd = [_dot(kd_ref[d, cis[d]], v_new_b[d]) for d in dirs]
        o_c = [r2[d][c:2 * c, :] + _dot(at_ref[d, cis[d]], stack_heads(v_new_b[d], lane256)) for d in dirs]
        for d in dirs:
            st_ref[d] = ss[d] * dr_ref[d, cis[d]] + jnp.where(bd_mask, upd[d], 0.0)
            rows = pl.ds(pl.multiple_of(cis[d] * c, c), c)
            oa_ref[rows, :] += o_c[d]
        return carry

    lax.fori_loop(0, nc, state_body, 0)

    o = oa_ref[...]
    o = o * lax.rsqrt(_seg_sum(o * o, ones_bd) * (1.0 / HD) + EPS) * gn_ref[...]
    o_ref[...] = o * _silu(u_ref[:, 3 * GW:4 * GW])
    if emit_state:
        for d in range(2):
            for h in range(NH):
                blk = st_ref[d, h * HD:(h + 1) * HD, (h // 2) * 128:(h // 2 + 1) * 128]
                if h % 2 == 1:
                    blk = pltpu.roll(blk, HD, axis=1)
                so_ref[d, h] = blk[:, 0:HD]


def _gdn_call(proj3, p, s0, seq_len, emit_state):
    bsz = proj3.shape[0]
    nc = seq_len // GDN_C
    full = lambda shape: pl.BlockSpec(shape, lambda b: (0,) * len(shape))
    in_specs = [pl.BlockSpec((None, seq_len, 4 * GW), lambda b: (b, 0, COL_GDN // (4 * GW))),
                pl.BlockSpec((None, seq_len, 128), lambda b: (b, 0, COL_SMALL // 128)),
                full((4, 3 * GW)), full((1, 128)), full((8, 1)), full((1, 128)), full((8, 1)),
                full((1, GW)), full((GW, GW))]
    args = [proj3, proj3, p["conv"], p["dtb_row"], p["dtb_col"], p["an_row"], p["an_col"], p["gn"], p["ones"]]
    if s0 is not None:
        in_specs.append(pl.BlockSpec((None, 2, GW, GW), lambda b: (b, 0, 0, 0)))
        args.append(s0)
    out_specs = [pl.BlockSpec((None, seq_len, GW), lambda b: (b, 0, 0))]
    out_shape = [jax.ShapeDtypeStruct((bsz, seq_len, GW), F32)]
    if emit_state:
        out_specs.append(pl.BlockSpec((None, 2, NH, HD, HD), lambda b: (b, 0, 0, 0, 0)))
        out_shape.append(jax.ShapeDtypeStruct((bsz, 2, NH, HD, HD), F32))
    scratch = [pltpu.VMEM((seq_len + 16, 3 * GW), F32),
               pltpu.VMEM((seq_len, GW), F32),
               pltpu.VMEM((seq_len, GW), F32),
               pltpu.VMEM((seq_len, GW), F32),
               pltpu.VMEM((seq_len, 128), F32),
               pltpu.VMEM((seq_len, 128), F32),
               pltpu.VMEM((nc, 8, GDN_C), F32),
               pltpu.VMEM((seq_len, GW), F32),
               pltpu.VMEM((2, GW, GW), F32),
               pltpu.VMEM((2, nc, GDN_C, GW), F32),
               pltpu.VMEM((2, nc, 2 * GDN_C, GW), BF16),
               pltpu.VMEM((2, nc, GDN_C, NH * GDN_C), BF16),
               pltpu.VMEM((2, nc, GW, GDN_C), BF16),
               pltpu.VMEM((2, nc, 1, GW), F32)]
    return pl.pallas_call(
        functools.partial(_gdn_kernel, seq_len=seq_len, has_s0=s0 is not None, emit_state=emit_state),
        grid=(bsz,),
        in_specs=in_specs,
        out_specs=out_specs,
        out_shape=out_shape,
        scratch_shapes=scratch,
        compiler_params=_cparams(1),
        name=f"gdn_{seq_len}",
    )(*args)


def _ffn_kernel(x_ref, ohy_ref, ossd_ref, olru_ref, ogdn_ref, mod_ref, g_ref, wo_ref, wg_ref, wu_ref,
                wd_ref, gfin_ref, o_ref, *, final_norm):
    mod = mod_ref[...]
    ga_m = mod[:, 2 * D_MODEL:3 * D_MODEL]
    sh_f = mod[:, 3 * D_MODEL:4 * D_MODEL]
    sc_f = mod[:, 4 * D_MODEL:5 * D_MODEL]
    ga_f = mod[:, 5 * D_MODEL:6 * D_MODEL]
    mo = None
    for i, r in enumerate((ohy_ref, ossd_ref, olru_ref, ogdn_ref)):
        part = _dot(r[...].astype(BF16), wo_ref[i * GW:(i + 1) * GW, :])
        mo = part if mo is None else mo + part
    x = x_ref[...] + ga_m * mo
    h = x * lax.rsqrt(jnp.mean(x * x, axis=-1, keepdims=True) + EPS) * g_ref[...]
    h = (h * (1.0 + sc_f) + sh_f).astype(BF16)
    ff = None
    for lo, hi in FF_SPLITS:
        gate = _dot(h, wg_ref[:, lo:hi])
        up = _dot(h, wu_ref[:, lo:hi])
        part = _dot((_silu(gate) * up).astype(BF16), wd_ref[lo:hi, :])
        ff = part if ff is None else ff + part
    x = x + ga_f * ff
    if final_norm:
        x = x * lax.rsqrt(jnp.mean(x * x, axis=-1, keepdims=True) + EPS) * gfin_ref[...]
    o_ref[...] = x


def _ffn_call(x2d, outs, mods_l, g, weights, layer, g_final, seq_len, is_lat, final_norm):
    n = x2d.shape[0]
    tm = ROW_TILE
    per_seq = seq_len // tm
    mod_map = (lambda t: (1 + t // per_seq, 0, 0)) if is_lat else (lambda t: (0, 0, 0))
    tile = lambda w: pl.BlockSpec((tm, w), lambda t: (t, 0))
    res = lambda shape: pl.BlockSpec(shape, lambda t: (0, 0), pipeline_mode=pl.Buffered(1))
    wres = lambda shape: pl.BlockSpec((None,) + shape, lambda t: (layer, 0, 0), pipeline_mode=pl.Buffered(1))
    return pl.pallas_call(
        functools.partial(_ffn_kernel, final_norm=final_norm),
        grid=(n // tm,),
        in_specs=[tile(D_MODEL), tile(GW), tile(GW), tile(GW), tile(GW),
                  pl.BlockSpec((None, 1, 6 * D_MODEL), mod_map),
                  res((1, D_MODEL)), wres((D_MODEL, D_MODEL)), wres((D_MODEL, D_FF)), wres((D_MODEL, D_FF)),
                  wres((D_FF, D_MODEL)), res((1, D_MODEL))],
        out_specs=tile(D_MODEL),
        out_shape=jax.ShapeDtypeStruct((n, D_MODEL), F32),
        compiler_params=_cparams(1),
        name="outproj_ffn",
    )(x2d, *outs, mods_l, g.reshape(1, D_MODEL), *weights, g_final.reshape(1, D_MODEL))


def _grid_pos_embed(n_tokens):
    rows = n_tokens // GRID_W
    rr, cc = np.meshgrid(np.arange(rows, dtype=np.float32), np.arange(GRID_W, dtype=np.float32), indexing="ij")
    quarter = D_MODEL // 4
    omega = (1.0 / (np.float32(POS_BASE) ** (np.arange(quarter, dtype=np.float32) / quarter))).astype(np.float32)

    def enc(pos):
        ang = pos.reshape(-1)[:, None] * omega[None, :]
        return np.concatenate([np.sin(ang), np.cos(ang)], axis=-1)

    return jnp.asarray(np.concatenate([enc(rr), enc(cc)], axis=-1).astype(np.float32))


def _pad_small(vec8, offset):
    row = jnp.zeros((1, 128), F32).at[0, offset:offset + 8].set(vec8)
    return row, vec8.reshape(8, 1)


def _pack_w_in(w):
    hy_in, ssd_in, lru_in = 768, 776, 512
    o_ssd = hy_in
    o_lru = o_ssd + ssd_in
    o_gdn = o_lru + lru_in
    w = w.astype(BF16)
    small = jnp.concatenate([w[..., o_ssd + 768:o_ssd + 776],
                             w[..., o_gdn + 1024:o_gdn + 1040],
                             jnp.zeros(w.shape[:-1] + (IN_PACK - 3096,), BF16)], axis=-1)
    return (w[..., 0:o_ssd + 768],
            w[..., o_lru:o_gdn + 1024],
            small)


def _ssd_state_in(s):
    st = jnp.swapaxes(s, -1, -2)
    zero = jnp.zeros_like(st[:, :, 0])
    rows = [jnp.concatenate([st[:, :, h] if h // 2 == g else zero for h in range(NH)], axis=-1)
            for g in range(2)]
    return jnp.concatenate(rows, axis=-2)


def _gdn_state_in(s):
    zero = jnp.zeros_like(s[:, :, 0])
    rows = [jnp.concatenate([s[:, :, h] if h == g else zero for h in range(NH)], axis=-1) for g in range(NH)]
    return jnp.concatenate(rows, axis=-2)


def kernel(x_prompt, x_sample, state_lru, state_ssd, state_gdn, c, c_ctx, w_mod, b_mod, g_mix, g_ffn, g_final, w_in, w_out, hy_conv, hy_w1, hy_b1, hy_w2, hy_b2, hy_w3, hy_freq, hy_bias, ssd_conv, ssd_dt_bias, ssd_a_log, ssd_d, ssd_norm, lru_conv, lru_w_r, lru_b_r, lru_w_i, lru_b_i, lru_lambda, gdn_conv, gdn_dt_bias, gdn_a_log, gdn_norm, w_gate, w_up, w_down):
    b_ctx, l_ctx, _ = x_prompt.shape
    b_lat, l_lat, _ = x_sample.shape

    cvec = jnp.zeros((16, D_MODEL), F32).at[0].set(c_ctx).at[1:1 + b_lat].set(c)
    mods = _mod_call(cvec, w_mod, b_mod).reshape(DEPTH, 16, 1, 6 * D_MODEL)

    hw = {"w1": jnp.zeros((DEPTH, 128, 128), F32).at[:, :HY_EMB, :HY_HIDDEN].set(hy_w1),
          "b1": jnp.zeros((DEPTH, 1, 128), F32).at[:, 0, :HY_HIDDEN].set(hy_b1),
          "w2": jnp.zeros((DEPTH, 128, 128), F32).at[:, :HY_HIDDEN, :HY_HIDDEN].set(hy_w2),
          "b2": jnp.zeros((DEPTH, 1, 128), F32).at[:, 0, :HY_HIDDEN].set(hy_b2),
          "w3": jnp.zeros((DEPTH, 128, 2 * GW), F32).at[:, :HY_HIDDEN, :].set(hy_w3),
          "freq": jnp.zeros((DEPTH, 2, 128), F32).at[:, :, :HY_HIDDEN].set(hy_freq)}
    filt = {l_ctx: _hyfilt_call(l_ctx, hw), l_lat: _hyfilt_call(l_lat, hw)}

    ones_bd = jnp.asarray(np.kron(np.eye(NH, dtype=np.float32), np.ones((HD, HD), np.float32))).astype(BF16)
    pos = _grid_pos_embed(l_lat)

    x_ctx = x_prompt.reshape(b_ctx * l_ctx, D_MODEL)
    x_lat = x_sample.reshape(b_lat * l_lat, D_MODEL)
    w_pack = _pack_w_in(w_in)
    ffn_w = tuple(w.astype(BF16) for w in (w_out, w_gate, w_up, w_down))
    lru_out, ssd_out, gdn_out = [], [], []
    for l in range(DEPTH):
        ssd_dtb_r, ssd_dtb_c = _pad_small(ssd_dt_bias[l].reshape(8), SM_DT)
        ssd_an_r, ssd_an_c = _pad_small(-jnp.exp(ssd_a_log[l].reshape(8)), SM_DT)
        ssd_p = {"conv": ssd_conv[l], "dtb_row": ssd_dtb_r, "dtb_col": ssd_dtb_c, "an_row": ssd_an_r,
                 "an_col": ssd_an_c, "dvec": jnp.repeat(ssd_d[l], HD).reshape(1, GW),
                 "gn": ssd_norm[l].reshape(1, GW)}
        eye_h = jnp.eye(NH, dtype=F32)
        blockdiag = lambda w: (w[:, :, :, None, :] * eye_h[None, :, None, :, None]).reshape(2, GW, GW)
        w_r, w_i = blockdiag(lru_w_r[l]), blockdiag(lru_w_i[l])
        lru_p = {"conv": lru_conv[l],
                 "w": jnp.concatenate([w_r[0], w_r[1], w_i[0], w_i[1]], axis=1).astype(BF16),
                 "bias": jnp.concatenate([lru_b_r[l].reshape(1, 2 * GW), lru_b_i[l].reshape(1, 2 * GW)], axis=1),
                 "sp": jax.nn.softplus(-lru_lambda[l]).reshape(1, 2 * GW)}
        gdn_dtb_r, gdn_dtb_c = _pad_small(gdn_dt_bias[l].reshape(8), SM_A)
        gdn_an_r, gdn_an_c = _pad_small(-jnp.exp(gdn_a_log[l].reshape(8)), SM_A)
        gdn_p = {"conv": gdn_conv[l], "dtb_row": gdn_dtb_r, "dtb_col": gdn_dtb_c, "an_row": gdn_an_r,
                 "an_col": gdn_an_c, "gn": jnp.tile(gdn_norm[l], NH).reshape(1, GW), "ones": ones_bd}
        last = l == DEPTH - 1

        proj = _inproj_call(x_ctx, None, mods[l], g_mix[l], w_pack, l, l_ctx, False).reshape(b_ctx, l_ctx, IN_PACK)
        o_hy = _hyena_call(proj, hy_conv[l], hy_bias[l], filt[l_ctx][0][l], filt[l_ctx][1][l], l_ctx, 4)
        o_ssd, s_ssd = _ssd_call(proj, ssd_p, None, l_ctx, True)
        o_lru, s_lru = _lru_call(proj, lru_p, None, l_ctx, True)
        o_gdn, s_gdn = _gdn_call(proj, gdn_p, None, l_ctx, True)
        outs = [o.reshape(b_ctx * l_ctx, GW) for o in (o_hy, o_ssd, o_lru, o_gdn)]
        x_ctx = _ffn_call(x_ctx, outs, mods[l], g_ffn[l], ffn_w, l, g_final, l_ctx, False, last)
        lru_out.append(s_lru)
        ssd_out.append(s_ssd)
        gdn_out.append(s_gdn)

        if l == 0:
            proj, x_lat = _inproj_call(x_lat, pos, mods[l], g_mix[l], w_pack, l, l_lat, True)
        else:
            proj = _inproj_call(x_lat, None, mods[l], g_mix[l], w_pack, l, l_lat, True)
        proj = proj.reshape(b_lat, l_lat, IN_PACK)
        o_hy = _hyena_call(proj, hy_conv[l], hy_bias[l], filt[l_lat][0][l], filt[l_lat][1][l], l_lat, 1)
        o_ssd, = _ssd_call(proj, ssd_p, _ssd_state_in(state_ssd[:, l]), l_lat, False)
        o_lru, = _lru_call(proj, lru_p, state_lru[:, l], l_lat, False)
        o_gdn, = _gdn_call(proj, gdn_p, _gdn_state_in(state_gdn[:, l]), l_lat, False)
        outs = [o.reshape(b_lat * l_lat, GW) for o in (o_hy, o_ssd, o_lru, o_gdn)]
        x_lat = _ffn_call(x_lat, outs, mods[l], g_ffn[l], ffn_w, l, g_final, l_lat, True, last)

    return (x_ctx.reshape(b_ctx, l_ctx, D_MODEL), x_lat.reshape(b_lat, l_lat, D_MODEL),
            jnp.stack(lru_out, axis=1), jnp.stack(ssd_out, axis=1), jnp.stack(gdn_out, axis=1))
```

```python
import functools
import math

import ml_dtypes
import numpy as np
import jax
import jax.numpy as jnp
from jax import lax
from jax.experimental import pallas as pl
from jax.experimental.pallas import tpu as pltpu

F32 = jnp.float32
BF16 = jnp.bfloat16
HI = lax.Precision.HIGHEST

D_MODEL = 1024
DEPTH = 2
EPS = 1e-6
GRID_W = 64
POS_BASE = 10000.0
GW = 256
D_FF = 2816
HD = 64
NH = 4
HY_BANDS = 16
HY_EMB = 1 + 2 * HY_BANDS
HY_HIDDEN = 64
LRU_C = 8.0

COL_HY = 0
COL_SSD = 768
COL_LRU = 1536
COL_GDN = 2048
COL_SMALL = 3072
IN_PACK = 3200
SM_DT, SM_B, SM_A = 0, 8, 16

SSD_Q = 128
GDN_C = 128
ROW_TILE = 512
ROW_BLOCK = 64
NORM_BLOCK = 256
V7X_MXU_DIM = 256
FF_SPLITS = ((0, 6 * V7X_MXU_DIM), (6 * V7X_MXU_DIM, D_FF))
V7X_VMEM_LIMIT = 56 * 1024 * 1024


def _cparams(n_axes=1):
    return pltpu.CompilerParams(dimension_semantics=("arbitrary",) * n_axes,
                                vmem_limit_bytes=V7X_VMEM_LIMIT)


def _silu(x):
    return x * jax.nn.sigmoid(x)


def _softplus(x):
    return jnp.maximum(x, 0.0) + jnp.log1p(jnp.exp(-jnp.abs(x)))


def _gelu_tanh(x):
    return 0.5 * x * (1.0 + jnp.tanh(math.sqrt(2.0 / math.pi) * (x + 0.044715 * (x * x * x))))


def _split_bf16(x):
    hi = x.astype(BF16)
    lo = (x - hi.astype(F32)).astype(BF16)
    return hi, lo


def _dot(a, b):
    return jnp.dot(a, b, preferred_element_type=F32)


def _dot_hi(a, b):
    return jnp.dot(a, b, precision=HI, preferred_element_type=F32)


def _cumsum_rows(tri_b, x):
    x1 = x.astype(BF16)
    r = x - x1.astype(F32)
    x2 = r.astype(BF16)
    x3 = (r - x2.astype(F32)).astype(BF16)
    return _dot(tri_b, x1) + _dot(tri_b, x2) + _dot(tri_b, x3)


def _cumsum_cols(x, tri_b):
    x1 = x.astype(BF16)
    r = x - x1.astype(F32)
    x2 = r.astype(BF16)
    x3 = (r - x2.astype(F32)).astype(BF16)
    return _dot(x1, tri_b) + _dot(x2, tri_b) + _dot(x3, tri_b)


def _dot3_const(ah, al, x):
    xh, xl = _split_bf16(x)
    return _dot(ah, xh) + _dot(ah, xl) + _dot(al, xh)


def _seg_sum(x, ones_bd):
    xh, xl = _split_bf16(x)
    return _dot(xh, ones_bd) + _dot(xl, ones_bd)


def _lane_iota(shape):
    return lax.broadcasted_iota(jnp.int32, shape, len(shape) - 1)


def _head_lanes(cols, rows):
    lane = _lane_iota((rows, GW))
    out = jnp.broadcast_to(cols[NH - 1], (rows, GW))
    for h in range(NH - 2, -1, -1):
        out = jnp.where(lane < (h + 1) * HD, cols[h], out)
    return out


def _dwconv_tiles(load_rows, ch, w_ref, pad_ref, n_rows, taps, tile, emit):
    left = taps // 2
    pad_ref[0:8, 0:ch] = jnp.zeros((8, ch), F32)
    pad_ref[8 + n_rows:16 + n_rows, 0:ch] = jnp.zeros((8, ch), F32)
    for r0 in range(0, n_rows, tile):
        pad_ref[8 + r0:8 + r0 + tile, 0:ch] = load_rows(r0, tile)
    w_rows = [w_ref[j:j + 1, :] for j in range(taps)]
    for r0 in range(0, n_rows, tile):
        acc = None
        for j in range(taps):
            off = 8 - left + j + r0
            term = pad_ref[off:off + tile, 0:ch] * w_rows[j]
            acc = term if acc is None else acc + term
        emit(r0, acc)


def _mod_kernel(c_ref, w_ref, b_ref, o_ref):
    s_hi, s_lo = _split_bf16(_silu(c_ref[...]))
    w_hi, w_lo = _split_bf16(w_ref[...])
    o_ref[...] = _dot(s_hi, w_hi) + _dot(s_hi, w_lo) + _dot(s_lo, w_hi) + b_ref[...]


def _mod_call(cvec, w_mod, b_mod):
    tn = 1536
    n = w_mod.shape[-1]
    return pl.pallas_call(
        _mod_kernel,
        grid=(DEPTH, n // tn),
        in_specs=[pl.BlockSpec((16, D_MODEL), lambda l, j: (0, 0)),
                  pl.BlockSpec((None, D_MODEL, tn), lambda l, j: (l, 0, j)),
                  pl.BlockSpec((None, 1, tn), lambda l, j: (l, 0, j))],
        out_specs=pl.BlockSpec((None, 16, tn), lambda l, j: (l, 0, j)),
        out_shape=jax.ShapeDtypeStruct((DEPTH, 16, n), F32),
        compiler_params=_cparams(2),
        name="adaln_mod",
    )(cvec, w_mod, b_mod.reshape(DEPTH, 1, n))


def _inproj_kernel(*refs, add_pos):
    if add_pos:
        x_ref, pos_ref, mod_ref, g_ref, wa_ref, wb_ref, wc_ref, o_ref, x0_ref = refs
        x = x_ref[...] + pos_ref[...]
        x0_ref[...] = x
    else:
        x_ref, mod_ref, g_ref, wa_ref, wb_ref, wc_ref, o_ref = refs
        x = x_ref[...]
    mod = mod_ref[...]
    sh = mod[:, 0:D_MODEL]
    sc = mod[:, D_MODEL:2 * D_MODEL]
    h = x * lax.rsqrt(jnp.mean(x * x, axis=-1, keepdims=True) + EPS) * g_ref[...]
    h = (h * (1.0 + sc) + sh).astype(BF16)
    o_ref[:, 0:COL_LRU] = _dot(h, wa_ref[...])
    o_ref[:, COL_LRU:COL_SMALL] = _dot(h, wb_ref[...])
    o_ref[:, COL_SMALL:IN_PACK] = _dot(h, wc_ref[...])


def _inproj_call(x2d, pos, mods_l, g, w_pack, layer, seq_len, is_lat):
    n = x2d.shape[0]
    tm = ROW_TILE
    per_seq = max(seq_len // tm, 1)
    mod_map = (lambda t: (1 + t // per_seq, 0, 0)) if is_lat else (lambda t: (0, 0, 0))
    in_specs = [pl.BlockSpec((tm, D_MODEL), lambda t: (t, 0))]
    args = [x2d]
    if pos is not None:
        in_specs.append(pl.BlockSpec((tm, D_MODEL), lambda t: (t % per_seq, 0)))
        args.append(pos)
    in_specs += [pl.BlockSpec((None, 1, 6 * D_MODEL), mod_map),
                 pl.BlockSpec((1, D_MODEL), lambda t: (0, 0)),
                 pl.BlockSpec((None, D_MODEL, COL_LRU), lambda t: (layer, 0, 0)),
                 pl.BlockSpec((None, D_MODEL, COL_SMALL - COL_LRU), lambda t: (layer, 0, 0)),
                 pl.BlockSpec((None, D_MODEL, IN_PACK - COL_SMALL), lambda t: (layer, 0, 0))]
    args += [mods_l, g.reshape(1, D_MODEL), *w_pack]
    out_specs = pl.BlockSpec((tm, IN_PACK), lambda t: (t, 0))
    out_shape = jax.ShapeDtypeStruct((n, IN_PACK), F32)
    if pos is not None:
        out_specs = [out_specs, pl.BlockSpec((tm, D_MODEL), lambda t: (t, 0))]
        out_shape = [out_shape, jax.ShapeDtypeStruct((n, D_MODEL), F32)]
    return pl.pallas_call(
        functools.partial(_inproj_kernel, add_pos=pos is not None),
        grid=(n // tm,),
        in_specs=in_specs,
        out_specs=out_specs,
        out_shape=out_shape,
        compiler_params=_cparams(1),
        name="inproj_lat0" if pos is not None else "inproj",
    )(*args)


@functools.lru_cache(maxsize=None)
def _dft_tables(seq_len):
    n = 2 * seq_len - 1
    idx = np.arange(seq_len, dtype=np.int64)
    ang = (2.0 * np.pi / n) * ((idx[:, None] * idx[None, :]) % n).astype(np.float64)
    out = []
    for tab in (np.cos(ang), np.sin(ang)):
        hi = tab.astype(ml_dtypes.bfloat16)
        lo = (tab - hi.astype(np.float64)).astype(ml_dtypes.bfloat16)
        out += [hi, lo]
    return tuple(out)


@functools.lru_cache(maxsize=None)
def _hyena_consts(seq_len):
    t = np.linspace(0.0, 1.0, seq_len, dtype=np.float32)[:, None]
    w = np.float32(2.0 * math.pi / seq_len) * np.arange(seq_len, dtype=np.float32)[:, None]
    bands = np.linspace(1e-4, HY_BANDS - 1, HY_BANDS, dtype=np.float32)[None, :]
    feats = np.concatenate([t, np.cos(bands * w), -np.sin(bands * w)], axis=-1).astype(np.float32)
    feats_p = np.zeros((seq_len, 128), np.float32)
    feats_p[:, :HY_EMB] = feats
    max_decay = math.log(1e-2) / 0.3
    min_decay = math.log(1e-2) / 1.5
    deltas = np.abs(np.linspace(min_decay, max_decay, GW, dtype=np.float32))
    decay = np.exp(-t * deltas).astype(np.float32)
    return feats_p, np.concatenate([decay, decay], axis=1)


def _hyfilt_kernel(feat_ref, dec_ref, w1_ref, b1_ref, w2_ref, b2_ref, w3_ref, fr_ref,
                   ch_ref, cl_ref, sh_ref, sl_ref, hr_ref, hi_ref, *, seq_len):
    fr = fr_ref[...]
    h = jnp.sin(fr[0:1, :] * (_dot_hi(feat_ref[...], w1_ref[...]) + b1_ref[...]))
    h = jnp.sin(fr[1:2, :] * (_dot_hi(h, w2_ref[...]) + b2_ref[...]))
    h = _dot_hi(h, w3_ref[...]) * dec_ref[...]
    den = jnp.sum(jnp.abs(h[:, :GW]) + jnp.abs(h[:, GW:]), axis=0, keepdims=True)
    h = h / jnp.concatenate([den, den], axis=1)
    hc = _dot3_const(ch_ref[...], cl_ref[...], h)
    hs = _dot3_const(sh_ref[...], sl_ref[...], h)
    n = 2 * seq_len - 1
    row = lax.broadcasted_iota(jnp.int32, (seq_len, GW), 0)
    wgt = jnp.where(row == 0, 1.0 / n, 2.0 / n)
    hr_ref[...] = (hc[:, :GW] + hc[:, GW:]) * wgt
    hi_ref[...] = (hs[:, GW:] - hs[:, :GW]) * wgt


def _hyfilt_call(seq_len, hw):
    feats, decay = _hyena_consts(seq_len)
    tabs = _dft_tables(seq_len)
    full = lambda shape: pl.BlockSpec(shape, lambda l: (0,) * len(shape))
    lay = lambda shape: pl.BlockSpec((None,) + shape, lambda l: (l,) + (0,) * len(shape))
    return pl.pallas_call(
        functools.partial(_hyfilt_kernel, seq_len=seq_len),
        grid=(DEPTH,),
        in_specs=[full((seq_len, 128)), full((seq_len, 2 * GW)),
                  lay((128, 128)), lay((1, 128)), lay((128, 128)), lay((1, 128)), lay((128, 2 * GW)),
                  lay((2, 128))] + [full((seq_len, seq_len))] * 4,
        out_specs=[lay((seq_len, GW)), lay((seq_len, GW))],
        out_shape=[jax.ShapeDtypeStruct((DEPTH, seq_len, GW), F32)] * 2,
        compiler_params=_cparams(1),
        name=f"hyena_filter_{seq_len}",
    )(feats, decay, hw["w1"], hw["b1"], hw["w2"], hw["b2"], hw["w3"], hw["freq"], *tabs)


def _hyena_seq(load_rows, store_rows, cw_ref, bias_ref, hr_ref, hi_ref, tabs, pad_ref, x0_ref, z_ref, seq_len):
    ch_ref, cl_ref, sh_ref, sl_ref = tabs
    blk = ROW_BLOCK // 2

    def emit_conv(r0, uc):
        x0_ref[r0:r0 + blk, :] = uc[:, 0:GW]
        z_ref[r0:r0 + blk, :] = uc[:, 2 * GW:3 * GW] * uc[:, GW:2 * GW]

    _dwconv_tiles(load_rows, 3 * GW, cw_ref, pad_ref, seq_len, 3, blk, emit_conv)
    z = z_ref[...]
    za = _dot3_const(ch_ref[...], cl_ref[...], z)
    zb = _dot3_const(sh_ref[...], sl_ref[...], z)
    hr = hr_ref[...]
    hi = hi_ref[...]
    yr = za * hr + zb * hi
    yi = za * hi - zb * hr
    y = _dot3_const(ch_ref[...], cl_ref[...], yr) - _dot3_const(sh_ref[...], sl_ref[...], yi)
    bias = bias_ref[...]
    for r0 in range(0, seq_len, ROW_BLOCK):
        rows = slice(r0, r0 + ROW_BLOCK)
        store_rows(r0, x0_ref[rows, :] * (y[rows, :] + z[rows, :] * bias))


def _tri_masks(q):
    row = lax.broadcasted_iota(jnp.int32, (q, q), 0)
    col = lax.broadcasted_iota(jnp.int32, (q, q), 1)
    return col <= row, col >= row


def _ssd_kernel(*refs, seq_len, has_s0, emit_state):
    it = iter(refs)
    zx_ref, sm_ref, conv_ref, dtb_r_ref, dtb_c_ref, an_r_ref, an_c_ref, dvec_ref, gn_ref = (
        next(it) for _ in range(9))
    s0_ref = next(it) if has_s0 else None
    o_ref = next(it)
    so_ref = next(it) if emit_state else None
    pad_ref, xs_ref, bm_ref, cm_ref, ac_ref, bt_ref, ar_ref, dtr_ref, y_ref, st_ref, upd_ref, ecs_ref = it
    q = SSD_Q
    nc = seq_len // q

    dvec = dvec_ref[...]

    def emit_conv(r0, acc):
        t = _silu(acc)
        rows = slice(r0, r0 + ROW_BLOCK)
        xs_ref[rows, :] = t[:, 0:GW]
        y_ref[rows, :] = t[:, 0:GW] * dvec
        bm_ref[rows, :] = t[:, GW:GW + 128]
        cm_ref[rows, :] = t[:, GW + 128:GW + 256].astype(BF16)

    _dwconv_tiles(lambda r0, nr: zx_ref[r0:r0 + nr, GW:3 * GW], 2 * GW, conv_ref, pad_ref, seq_len, 4,
                  ROW_BLOCK, emit_conv)
    sm = sm_ref[...]
    ac_ref[...] = _softplus(sm + dtb_r_ref[...]) * an_r_ref[...]
    sm_t = sm.T
    dtr = _softplus(sm_t[SM_DT:SM_DT + 8, :] + dtb_c_ref[...])
    ar = dtr * an_c_ref[...]
    for c in range(nc):
        bt_ref[c] = bm_ref[c * q:(c + 1) * q, :].T.astype(BF16)
        ar_ref[c] = ar[:, c * q:(c + 1) * q]
        dtr_ref[c] = dtr[:, c * q:(c + 1) * q]
    if has_s0:
        st_ref[...] = s0_ref[...]
    else:
        st_ref[...] = jnp.zeros(st_ref.shape, F32)

    lower, upper = _tri_masks(q)
    lo_f = lower.astype(F32).astype(BF16)
    up_f = upper.astype(F32).astype(BF16)
    lane128 = _lane_iota((q, 128))
    lane256 = _lane_iota((q, GW))
    st_row = lax.broadcasted_iota(jnp.int32, (128, GW), 0)
    st_col = lax.broadcasted_iota(jnp.int32, (128, GW), 1)
    st_mask = (st_row // HD) == (st_col // 128)
    dirs = (0, 1)
    tri_c = (lo_f, up_f)
    tri_r = (up_f, lo_f)
    masks = (lower, upper)
    last = (q - 1, 0)
    spread = [((st_row == d * NH + st_col // HD)).astype(F32).astype(BF16) for d in dirs]

    def chunk_body(ci, carry):
        rows = pl.ds(pl.multiple_of(ci * q, q), q)
        xs_b = xs_ref[rows, :].astype(BF16)
        c_c = cm_ref[rows, :]
        b_t = bt_ref[ci]
        a_c = ac_ref[rows, :]
        a_r = ar_ref[ci]
        dt_r = dtr_ref[ci]
        stack = jnp.concatenate(
            [jnp.where((lane256 // HD) == h, xs_b, jnp.zeros_like(xs_b)) for h in range(NH)], axis=0)
        c_g = jnp.concatenate([jnp.where(lane128 < HD, c_c, jnp.zeros_like(c_c)),
                               jnp.where(lane128 >= HD, c_c, jnp.zeros_like(c_c))], axis=0)
        g_mat = _dot(c_g, b_t)
        cum_c = [_cumsum_rows(tri_c[d], a_c) for d in dirs]
        cum_r = [_cumsum_cols(a_r, tri_r[d]) for d in dirs]
        w_cat, bw_cat = [], []
        for d in dirs:
            w_parts, bw_parts = [], []
            for h in range(NH):
                j = d * NH + h
                g = h // 2
                diff = cum_c[d][:, j:j + 1] - cum_r[d][j:j + 1, :]
                m = jnp.where(masks[d], jnp.exp(jnp.minimum(diff, 0.0)), 0.0)
                w_parts.append((g_mat[g * q:(g + 1) * q, :] * m * dt_r[j:j + 1, :]).astype(BF16))
                to_end = jnp.exp(cum_r[d][j:j + 1, last[d]:last[d] + 1] - cum_r[d][j:j + 1, :])
                bw_parts.append((b_t.astype(F32) * (dt_r[j:j + 1, :] * to_end)).astype(BF16))
            w_cat.append(jnp.concatenate(w_parts, axis=1))
            bw_cat.append(jnp.concatenate(bw_parts, axis=1))
        y_diag = [_dot(w_cat[d], stack) for d in dirs]
        upd = [_dot(bw_cat[d], stack) for d in dirs]
        ecs = []
        for d in dirs:
            e_hi, e_lo = _split_bf16(jnp.exp(cum_c[d]))
            ecs.append(_dot(e_hi, spread[d]) + _dot(e_lo, spread[d]))
        y_ref[rows, :] += y_diag[0] + y_diag[1]
        for d in dirs:
            upd_ref[d, ci] = jnp.where(st_mask, upd[d], 0.0)
            ecs_ref[d, ci] = ecs[d]
        return carry

    lax.fori_loop(0, nc, chunk_body, 0)

    def state_body(i, carry):
        cis = (i, nc - 1 - i)
        rows = [pl.ds(pl.multiple_of(ci * q, q), q) for ci in cis]
        ss = [st_ref[d] for d in dirs]
        y_off = [_dot(cm_ref[rows[d], :], ss[d].astype(BF16)) for d in dirs]
        for d in dirs:
            ecs = ecs_ref[d, cis[d]]
            y_ref[rows[d], :] += y_off[d] * ecs
            st_ref[d] = ss[d] * ecs[last[d]:last[d] + 1, :] + upd_ref[d, cis[d]]
        return carry

    lax.fori_loop(0, nc, state_body, 0)

    gn = gn_ref[...]
    for r0 in range(0, seq_len, ROW_BLOCK):
        rows = slice(r0, r0 + ROW_BLOCK)
        y = y_ref[rows, :] * _silu(zx_ref[rows, 0:GW])
        o_ref[rows, :] = y * lax.rsqrt(jnp.mean(y * y, axis=-1, keepdims=True) + EPS) * gn
    if emit_state:
        for d in range(2):
            st_t = st_ref[d].T
            for h in range(NH):
                blk = st_t[h * HD:(h + 1) * HD, :]
                if h // 2 == 1:
                    blk = pltpu.roll(blk, HD, axis=1)
                so_ref[d, h] = blk[:, 0:HD]


def _ssd_call(proj3, p, s0, seq_len, emit_state):
    bsz = proj3.shape[0]
    nc = seq_len // SSD_Q
    full = lambda shape: pl.BlockSpec(shape, lambda b: (0,) * len(shape))
    in_specs = [pl.BlockSpec((None, seq_len, 3 * GW), lambda b: (b, 0, COL_SSD // (3 * GW))),
                pl.BlockSpec((None, seq_len, 128), lambda b: (b, 0, COL_SMALL // 128)),
                full((4, 2 * GW)), full((1, 128)), full((8, 1)), full((1, 128)), full((8, 1)),
                full((1, GW)), full((1, GW))]
    args = [proj3, proj3, p["conv"], p["dtb_row"], p["dtb_col"], p["an_row"], p["an_col"], p["dvec"], p["gn"]]
    if s0 is not None:
        in_specs.append(pl.BlockSpec((None, 2, 128, GW), lambda b: (b, 0, 0, 0)))
        args.append(s0)
    out_specs = [pl.BlockSpec((None, seq_len, GW), lambda b: (b, 0, 0))]
    out_shape = [jax.ShapeDtypeStruct((bsz, seq_len, GW), F32)]
    if emit_state:
        out_specs.append(pl.BlockSpec((None, 2, NH, HD, HD), lambda b: (b, 0, 0, 0, 0)))
        out_shape.append(jax.ShapeDtypeStruct((bsz, 2, NH, HD, HD), F32))
    scratch = [pltpu.VMEM((seq_len + 16, 2 * GW), F32),
               pltpu.VMEM((seq_len, GW), F32),
               pltpu.VMEM((seq_len, 128), F32),
               pltpu.VMEM((seq_len, 128), BF16),
               pltpu.VMEM((seq_len, 128), F32),
               pltpu.VMEM((nc, 128, SSD_Q), BF16),
               pltpu.VMEM((nc, 8, SSD_Q), F32),
               pltpu.VMEM((nc, 8, SSD_Q), F32),
               pltpu.VMEM((seq_len, GW), F32),
               pltpu.VMEM((2, 128, GW), F32),
               pltpu.VMEM((2, nc, 128, GW), F32),
               pltpu.VMEM((2, nc, SSD_Q, GW), F32)]
    return pl.pallas_call(
        functools.partial(_ssd_kernel, seq_len=seq_len, has_s0=s0 is not None, emit_state=emit_state),
        grid=(bsz,),
        in_specs=in_specs,
        out_specs=out_specs,
        out_shape=out_shape,
        scratch_shapes=scratch,
        compiler_params=_cparams(1),
        name=f"ssd_{seq_len}",
    )(*args)


def _lru_gates(load_rows, conv_ref, w_ref, bias_ref, sp_ref, pad_ref, xc_ref, scan_refs, n):
    blk = 2 * ROW_BLOCK

    def emit_conv(r0, xc):
        xc_ref[r0:r0 + blk, :] = xc

    _dwconv_tiles(load_rows, GW, conv_ref, pad_ref, n, 4, blk, emit_conv)
    for r0 in range(0, n, blk):
        xc = xc_ref[r0:r0 + blk, :]
        xc_b = xc.astype(BF16)
        for d in range(2):
            a_ref, b_ref = scan_refs[2 * d], scan_refs[2 * d + 1]
            rc = slice(d * GW, (d + 1) * GW)
            ic = slice((2 + d) * GW, (3 + d) * GW)
            r = jax.nn.sigmoid(_dot(xc_b, w_ref[:, rc]) + bias_ref[:, rc])
            i = jax.nn.sigmoid(_dot(xc_b, w_ref[:, ic]) + bias_ref[:, ic])
            a = jnp.exp(-LRU_C * r * sp_ref[:, rc])
            a_ref[8 + r0:8 + r0 + blk, :] = a
            b_ref[8 + r0:8 + r0 + blk, :] = jnp.sqrt(jnp.maximum(1.0 - a * a, 0.0)) * i * xc


def _lru_scan(s0, scan_refs, n):
    af_ref, bf_ref, ab_ref, bb_ref = scan_refs
    ones8 = jnp.ones((8, GW), F32)
    zeros8 = jnp.zeros((8, GW), F32)
    for a_ref_, b_ref_ in ((af_ref, bf_ref), (ab_ref, bb_ref)):
        a_ref_[0:8, :] = ones8
        a_ref_[8 + n:16 + n, :] = ones8
        b_ref_[0:8, :] = zeros8
        b_ref_[8 + n:16 + n, :] = zeros8
    if s0 is not None:
        bf_ref[8:9, :] = bf_ref[8:9, :] + af_ref[8:9, :] * s0[0:1, :]
        bb_ref[7 + n:8 + n, :] = bb_ref[7 + n:8 + n, :] + ab_ref[7 + n:8 + n, :] * s0[1:2, :]

    d = 1
    while d < n:
        if d < 8:
            a_cur = af_ref[8:8 + n, :]
            af_ref[8:8 + n, :] = a_cur * af_ref[8 - d:8 - d + n, :]
            bf_ref[8:8 + n, :] = a_cur * bf_ref[8 - d:8 - d + n, :] + bf_ref[8:8 + n, :]
            a_cur = ab_ref[8:8 + n, :]
            ab_ref[8:8 + n, :] = a_cur * ab_ref[8 + d:8 + d + n, :]
            bb_ref[8:8 + n, :] = a_cur * bb_ref[8 + d:8 + d + n, :] + bb_ref[8:8 + n, :]
        else:
            m = n - d
            a_cur = af_ref[8 + d:8 + n, :]
            a_new = a_cur * af_ref[8:8 + m, :]
            b_new = a_cur * bf_ref[8:8 + m, :] + bf_ref[8 + d:8 + n, :]
            af_ref[8 + d:8 + n, :] = a_new
            bf_ref[8 + d:8 + n, :] = b_new
            a_cur = ab_ref[8:8 + m, :]
            a_new = a_cur * ab_ref[8 + d:8 + n, :]
            b_new = a_cur * bb_ref[8 + d:8 + n, :] + bb_ref[8:8 + m, :]
            ab_ref[8:8 + m, :] = a_new
            bb_ref[8:8 + m, :] = b_new
        d *= 2


def _hy_lru_kernel(*refs, seq_len, n_seq, has_s0, emit_state):
    it = iter(refs)
    (uh_ref, ul_ref, hcw_ref, hbias_ref, hr_ref, hi_ref, ch_ref, cl_ref, sh_ref, sl_ref,
     lcw_ref, lw_ref, lbias_ref, lsp_ref) = (next(it) for _ in range(14))
    s0_ref = next(it) if has_s0 else None
    ohy_ref = next(it)
    olru_ref = next(it)
    so_ref = next(it) if emit_state else None
    hpad_ref, x0_ref, z_ref, lpad_ref, xc_ref, af_ref, bf_ref, ab_ref, bb_ref = it
    n = seq_len
    tabs = (ch_ref, cl_ref, sh_ref, sl_ref)
    scan_refs = (af_ref, bf_ref, ab_ref, bb_ref)
    for b in range(n_seq):
        def store_hy(r0, rows_out, b=b):
            ohy_ref[b, r0:r0 + ROW_BLOCK, :] = rows_out

        _hyena_seq(lambda r0, nr, b=b: uh_ref[b, r0:r0 + nr, :], store_hy, hcw_ref, hbias_ref, hr_ref, hi_ref,
                   tabs, hpad_ref, x0_ref, z_ref, n)
        _lru_gates(lambda r0, nr, b=b: ul_ref[b, r0:r0 + nr, 0:GW], lcw_ref, lw_ref, lbias_ref, lsp_ref,
                   lpad_ref, xc_ref, scan_refs, n)
        _lru_scan(s0_ref[b] if has_s0 else None, scan_refs, n)
        for r0 in range(0, n, ROW_BLOCK):
            rows = slice(r0, r0 + ROW_BLOCK)
            h = bf_ref[8 + r0:8 + r0 + ROW_BLOCK, :] + bb_ref[8 + r0:8 + r0 + ROW_BLOCK, :]
            olru_ref[b, rows, :] = h * _gelu_tanh(ul_ref[b, rows, GW:2 * GW])
        if emit_state:
            so_ref[b, 0:1, :] = bf_ref[7 + n:8 + n, :]
            so_ref[b, 1:2, :] = bb_ref[8:9, :]


def _hy_lru_call(proj3, hy, lru, s0, seq_len, n_seq, emit_state):
    bsz = proj3.shape[0]
    tabs = _dft_tables(seq_len)
    full = lambda shape: pl.BlockSpec(shape, lambda b: (0,) * len(shape))
    const = lambda shape: pl.BlockSpec(shape, lambda b: (0,) * len(shape), pipeline_mode=pl.Buffered(1))
    in_specs = [pl.BlockSpec((n_seq, seq_len, 3 * GW), lambda b: (b, 0, COL_HY // (3 * GW))),
                pl.BlockSpec((n_seq, seq_len, 2 * GW), lambda b: (b, 0, COL_LRU // (2 * GW))),
                full((3, 3 * GW)), full((1, GW)), const((seq_len, GW)), const((seq_len, GW))]
    in_specs += [const((seq_len, seq_len))] * 4
    in_specs += [full((4, GW)), const((GW, 4 * GW)), full((1, 4 * GW)), full((1, 2 * GW))]
    args = [proj3, proj3, hy["conv"], hy["bias"], hy["hr"], hy["hi"], *tabs,
            lru["conv"], lru["w"], lru["bias"], lru["sp"]]
    if s0 is not None:
        in_specs.append(pl.BlockSpec((n_seq, 2, GW), lambda b: (b, 0, 0)))
        args.append(s0)
    out_specs = [pl.BlockSpec((n_seq, seq_len, GW), lambda b: (b, 0, 0))] * 2
    out_shape = [jax.ShapeDtypeStruct((bsz, seq_len, GW), F32)] * 2
    if emit_state:
        out_specs.append(pl.BlockSpec((n_seq, 2, GW), lambda b: (b, 0, 0)))
        out_shape.append(jax.ShapeDtypeStruct((bsz, 2, GW), F32))
    scratch = [pltpu.VMEM((seq_len + 16, 3 * GW), F32),
               pltpu.VMEM((seq_len, GW), F32),
               pltpu.VMEM((seq_len, GW), F32),
               pltpu.VMEM((seq_len + 16, GW), F32),
               pltpu.VMEM((seq_len, GW), F32)]
    scratch += [pltpu.VMEM((seq_len + 16, GW), F32)] * 4
    return pl.pallas_call(
        functools.partial(_hy_lru_kernel, seq_len=seq_len, n_seq=n_seq, has_s0=s0 is not None,
                          emit_state=emit_state),
        grid=(bsz // n_seq,),
        in_specs=in_specs,
        out_specs=out_specs,
        out_shape=out_shape,
        scratch_shapes=scratch,
        compiler_params=_cparams(1),
        name=f"hyena_rglru_{seq_len}",
    )(*args)


TRI_BASE = 8


def _unit_tri_inverse(a_mats, lower_flags, eye, same_blk):
    c = a_mats[0].shape[0]
    ds = [jnp.where(same_blk[TRI_BASE], a, 0.0) for a in a_mats]
    d_bs = [d.astype(BF16) for d in ds]
    ps = [eye - d for d in ds]
    x_bs = [_dot(d_b, d_b).astype(BF16) for d_b in d_bs]
    rs = [_dot(x_b, jnp.concatenate([p.astype(BF16), x_b], axis=1)) for x_b, p in zip(x_bs, ps)]
    ps = [p + r[:, 0:c] for p, r in zip(ps, rs)]
    ts = [p + _dot(r[:, c:2 * c].astype(BF16), p.astype(BF16)) for p, r in zip(ps, rs)]
    size = 2 * TRI_BASE
    while size <= c:
        half = size // 2
        off = same_blk[size] & jnp.logical_not(same_blk[half])
        e_bs = [jnp.where(off, a, 0.0).astype(BF16) for a in a_mats]
        t_bs = [t.astype(BF16) for t in ts]
        starts = [[blk * size + (half if low else 0) for blk in range(c // size)] for low in lower_flags]
        rows = [jnp.concatenate([t[r0:r0 + half, :] for r0 in st], axis=0).astype(BF16)
                for t, st in zip(ts, starts)]
        us = [_dot(r, e_b).astype(BF16) for r, e_b in zip(rows, e_bs)]
        upds = [_dot(u, t_b) for u, t_b in zip(us, t_bs)]
        new_ts = []
        for t, upd, st, low in zip(ts, upds, starts, lower_flags):
            pieces = []
            for n, r0 in enumerate(st):
                changed = t[r0:r0 + half, :] - upd[n * half:(n + 1) * half, :]
                kept = t[r0 - half:r0, :] if low else t[r0 + half:r0 + size, :]
                pieces += [kept, changed] if low else [changed, kept]
            new_ts.append(jnp.concatenate(pieces, axis=0))
        ts = new_ts
        size *= 2
    return ts


def _gdn_kernel(*refs, seq_len, has_s0, emit_state):
    it = iter(refs)
    u_ref, sm_ref, conv_ref, dtb_r_ref, dtb_c_ref, an_r_ref, an_c_ref, gn_ref, ones_ref = (
        next(it) for _ in range(9))
    s0_ref = next(it) if has_s0 else None
    o_ref = next(it)
    so_ref = next(it) if emit_state else None
    (pad_ref, q_ref, k_ref, v_ref, be_ref, gc_ref, gr_ref, oa_ref, st_ref,
     us_ref, wq_ref, at_ref, kd_ref, dr_ref) = it
    c = GDN_C
    nc = seq_len // c
    ones_bd = ones_ref[...]

    def emit_conv(r0, acc):
        t = _silu(acc)
        rows = slice(r0, r0 + ROW_BLOCK // 2)
        q_ref[rows, :] = t[:, 0:GW]
        k_ref[rows, :] = t[:, GW:2 * GW]
        v_ref[rows, :] = t[:, 2 * GW:3 * GW]

    _dwconv_tiles(lambda r0, nr: u_ref[r0:r0 + nr, 0:3 * GW], 3 * GW, conv_ref, pad_ref, seq_len, 4,
                  ROW_BLOCK // 2, emit_conv)
    for r0 in range(0, seq_len, NORM_BLOCK):
        rows = slice(r0, r0 + NORM_BLOCK)
        qq = q_ref[rows, :]
        kk = k_ref[rows, :]
        q_ref[rows, :] = qq * lax.rsqrt(_seg_sum(qq * qq, ones_bd) + EPS) * (HD ** -0.5)
        k_ref[rows, :] = kk * lax.rsqrt(_seg_sum(kk * kk, ones_bd) + EPS)
    sm = sm_ref[...]
    be_ref[...] = jax.nn.sigmoid(sm)
    gc_ref[...] = an_r_ref[...] * _softplus(sm + dtb_r_ref[...])
    sm_t = sm.T
    g_r = an_c_ref[...] * _softplus(sm_t[SM_A:SM_A + 8, :] + dtb_c_ref[...])
    for i in range(nc):
        gr_ref[i] = g_r[:, i * c:(i + 1) * c]
    oa_ref[...] = jnp.zeros(oa_ref.shape, F32)
    if has_s0:
        st_ref[...] = s0_ref[...]
    else:
        st_ref[...] = jnp.zeros(st_ref.shape, F32)

    lower, upper = _tri_masks(c)
    lo_f = lower.astype(F32).astype(BF16)
    up_f = upper.astype(F32).astype(BF16)
    row_i =lax.broadcasted_iota(jnp.int32, (c, c), 0)
    col_i = lax.broadcasted_iota(jnp.int32, (c, c), 1)
    eye = (row_i == col_i).astype(F32)
    same_blk = {}
    size = TRI_BASE
    while size <= c:
        same_blk[size] = (row_i // size) == (col_i // size)
        size *= 2
    lane256 = _lane_iota((c, GW))
    lane512 = _lane_iota((c, 2 * GW))
    bd_row = lax.broadcasted_iota(jnp.int32, (GW, GW), 0)
    bd_col = lax.broadcasted_iota(jnp.int32, (GW, GW), 1)
    bd_mask = (bd_row // HD) == (bd_col // HD)
    kt_row = lax.broadcasted_iota(jnp.int32, (GW, c), 0)

    def stack_heads(x_b, lane):
        zero = jnp.zeros_like(x_b)
        return jnp.concatenate([jnp.where(((lane % GW) // HD) == h, x_b, zero) for h in range(NH)], axis=0)

    dirs = (0, 1)
    tri_c = (lo_f, up_f)
    tri_r = (up_f, lo_f)
    incl = (lower, upper)
    strict = (col_i < row_i, col_i > row_i)
    last = (c - 1, 0)

    chunks_per_step = 2

    def solve_body(step, carry):
        cks = [step * chunks_per_step + n for n in range(chunks_per_step)]
        units = [(n, d) for n in range(chunks_per_step) for d in dirs]
        rows = [pl.ds(pl.multiple_of(ci * c, c), c) for ci in cks]
        q_c = [q_ref[r, :] for r in rows]
        k_c = [k_ref[r, :] for r in rows]
        v_c = [v_ref[r, :] for r in rows]
        be = [be_ref[r, :] for r in rows]
        g_c = [gc_ref[r, :] for r in rows]
        g_r = [gr_ref[ci] for ci in cks]
        cum_c = {(n, d): _cumsum_rows(tri_c[d], g_c[n]) for n, d in units}
        cum_r = {(n, d): _cumsum_cols(g_r[n], tri_r[d]) for n, d in units}
        kq = []
        for n in range(chunks_per_step):
            k_t = k_c[n].T.astype(BF16)
            k_t_heads = jnp.concatenate(
                [jnp.where((kt_row // HD) == h, k_t, jnp.zeros_like(k_t)) for h in range(NH)], axis=1)
            kq.append(_dot(jnp.concatenate([k_c[n], q_c[n]], axis=0).astype(BF16), k_t_heads))
        a_mats, attn, flags = [], {}, []
        for n, d in units:
            for h in range(NH):
                j = d * NH + h
                diff = cum_c[n, d][:, SM_A + j:SM_A + j + 1] - cum_r[n, d][j:j + 1, :]
                dec = jnp.where(incl[d], jnp.exp(jnp.minimum(diff, 0.0)), 0.0)
                a_mats.append(jnp.where(
                    strict[d], kq[n][0:c, h * c:(h + 1) * c] * be[n][:, SM_B + j:SM_B + j + 1] * dec, 0.0))
                attn[n, d, h] = (kq[n][c:2 * c, h * c:(h + 1) * c] * dec).astype(BF16)
                flags.append(d == 0)
        t_mats = _unit_tri_inverse(a_mats, flags, eye, same_blk)
        t_cat = {u: jnp.concatenate([t.astype(BF16) for t in t_mats[idx * NH:(idx + 1) * NH]], axis=1)
                 for idx, u in enumerate(units)}
        cols = {(n, d): [cum_c[n, d][:, SM_A + d * NH + h:SM_A + d * NH + h + 1] for h in range(NH)]
                for n, d in units}
        lasts = {(n, d): [cum_c[n, d][last[d]:last[d] + 1, SM_A + d * NH + h:SM_A + d * NH + h + 1]
                          for h in range(NH)] for n, d in units}
        b_l = {(n, d): _head_lanes([be[n][:, SM_B + d * NH + h:SM_B + d * NH + h + 1] for h in range(NH)], c)
               for n, d in units}
        e_l = {(n, d): _head_lanes([jnp.exp(cc) for cc in cols[n, d]], c) for n, d in units}
        rhs = {(n, d): jnp.concatenate([v_c[n] * b_l[n, d], k_c[n] * b_l[n, d] * e_l[n, d]], axis=1)
               for n, d in units}
        uw = {u: _dot(t_cat[u], stack_heads(rhs[u].astype(BF16), lane512)) for u in units}
        for n, d in units:
            ci = cks[n]
            us_ref[d, ci] = uw[n, d][:, 0:GW]
            wq_ref[d, ci, 0:c, :] = uw[n, d][:, GW:2 * GW].astype(BF16)
            wq_ref[d, ci, c:2 * c, :] = (q_c[n] * e_l[n, d]).astype(BF16)
            at_ref[d, ci] = jnp.concatenate([attn[n, d, h] for h in range(NH)], axis=1)
            k_dec = k_c[n] * _head_lanes([jnp.exp(lasts[n, d][h] - cols[n, d][h]) for h in range(NH)], c)
            kd_ref[d, ci] = k_dec.T.astype(BF16)
            dr_ref[d, ci] = _head_lanes([jnp.exp(lv) for lv in lasts[n, d]], 1)
        return carry

    lax.fori_loop(0, nc // chunks_per_step, solve_body, 0)

    def state_body(i, carry):
        cis = (i, nc - 1 - i)
        ss = [st_ref[d] for d in dirs]
        r2 = [_dot(wq_ref[d, cis[d]], ss[d].astype(BF16)) for d in dirs]
        v_new_b = [(us_ref[d, cis[d]] - r2[d][0:c, :]).astype(BF16) for d in dirs]
        upd = [_dot(kd_ref[d, cis[d]], v_new_b[d]) for d in dirs]
        o_c = [r2[d][c:2 * c, :] + _dot(at_ref[d, cis[d]], stack_heads(v_new_b[d], lane256)) for d in dirs]
        for d in dirs:
            st_ref[d] = ss[d] * dr_ref[d, cis[d]] + jnp.where(bd_mask, upd[d], 0.0)
            rows = pl.ds(pl.multiple_of(cis[d] * c, c), c)
            oa_ref[rows, :] += o_c[d]
        return carry

    lax.fori_loop(0, nc, state_body, 0)

    gn = gn_ref[...]
    for r0 in range(0, seq_len, NORM_BLOCK):
        rows = slice(r0, r0 + NORM_BLOCK)
        o = oa_ref[rows, :]
        o = o * lax.rsqrt(_seg_sum(o * o, ones_bd) * (1.0 / HD) + EPS) * gn
        o_ref[rows, :] = o * _silu(u_ref[rows, 3 * GW:4 * GW])
    if emit_state:
        for d in range(2):
            for h in range(NH):
                blk = st_ref[d, h * HD:(h + 1) * HD, (h // 2) * 128:(h // 2 + 1) * 128]
                if h % 2 == 1:
                    blk = pltpu.roll(blk, HD, axis=1)
                so_ref[d, h] = blk[:, 0:HD]


def _gdn_call(proj3, p, s0, seq_len, emit_state):
    bsz = proj3.shape[0]
    nc = seq_len // GDN_C
    full = lambda shape: pl.BlockSpec(shape, lambda b: (0,) * len(shape))
    in_specs = [pl.BlockSpec((None, seq_len, 4 * GW), lambda b: (b, 0, COL_GDN // (4 * GW))),
                pl.BlockSpec((None, seq_len, 128), lambda b: (b, 0, COL_SMALL // 128)),
                full((4, 3 * GW)), full((1, 128)), full((8, 1)), full((1, 128)), full((8, 1)),
                full((1, GW)), full((GW, GW))]
    args = [proj3, proj3, p["conv"], p["dtb_row"], p["dtb_col"], p["an_row"], p["an_col"], p["gn"], p["ones"]]
    if s0 is not None:
        in_specs.append(pl.BlockSpec((None, 2, GW, GW), lambda b: (b, 0, 0, 0)))
        args.append(s0)
    out_specs = [pl.BlockSpec((None, seq_len, GW), lambda b: (b, 0, 0))]
    out_shape = [jax.ShapeDtypeStruct((bsz, seq_len, GW), F32)]
    if emit_state:
        out_specs.append(pl.BlockSpec((None, 2, NH, HD, HD), lambda b: (b, 0, 0, 0, 0)))
        out_shape.append(jax.ShapeDtypeStruct((bsz, 2, NH, HD, HD), F32))
    scratch = [pltpu.VMEM((seq_len + 16, 3 * GW), F32),
               pltpu.VMEM((seq_len, GW), F32),
               pltpu.VMEM((seq_len, GW), F32),
               pltpu.VMEM((seq_len, GW), F32),
               pltpu.VMEM((seq_len, 128), F32),
               pltpu.VMEM((seq_len, 128), F32),
               pltpu.VMEM((nc, 8, GDN_C), F32),
               pltpu.VMEM((seq_len, GW), F32),
               pltpu.VMEM((2, GW, GW), F32),
               pltpu.VMEM((2, nc, GDN_C, GW), F32),
               pltpu.VMEM((2, nc, 2 * GDN_C, GW), BF16),
               pltpu.VMEM((2, nc, GDN_C, NH * GDN_C), BF16),
               pltpu.VMEM((2, nc, GW, GDN_C), BF16),
               pltpu.VMEM((2, nc, 1, GW), F32)]
    return pl.pallas_call(
        functools.partial(_gdn_kernel, seq_len=seq_len, has_s0=s0 is not None, emit_state=emit_state),
        grid=(bsz,),
        in_specs=in_specs,
        out_specs=out_specs,
        out_shape=out_shape,
        scratch_shapes=scratch,
        compiler_params=_cparams(1),
        name=f"gdn_{seq_len}",
    )(*args)


def _ffn_kernel(x_ref, ohy_ref, ossd_ref, olru_ref, ogdn_ref, mod_ref, g_ref, wo_ref, wg_ref, wu_ref,
                wd_ref, gfin_ref, o_ref, *, final_norm):
    mod = mod_ref[...]
    ga_m = mod[:, 2 * D_MODEL:3 * D_MODEL]
    sh_f = mod[:, 3 * D_MODEL:4 * D_MODEL]
    sc_f = mod[:, 4 * D_MODEL:5 * D_MODEL]
    ga_f = mod[:, 5 * D_MODEL:6 * D_MODEL]
    mo = None
    for i, r in enumerate((ohy_ref, ossd_ref, olru_ref, ogdn_ref)):
        part = _dot(r[...].astype(BF16), wo_ref[i * GW:(i + 1) * GW, :])
        mo = part if mo is None else mo + part
    x = x_ref[...] + ga_m * mo
    h = x * lax.rsqrt(jnp.mean(x * x, axis=-1, keepdims=True) + EPS) * g_ref[...]
    h = (h * (1.0 + sc_f) + sh_f).astype(BF16)
    ff = None
    for lo, hi in FF_SPLITS:
        gate = _dot(h, wg_ref[:, lo:hi])
        up = _dot(h, wu_ref[:, lo:hi])
        part = _dot((_silu(gate) * up).astype(BF16), wd_ref[lo:hi, :])
        ff = part if ff is None else ff + part
    x = x + ga_f * ff
    if final_norm:
        x = x * lax.rsqrt(jnp.mean(x * x, axis=-1, keepdims=True) + EPS) * gfin_ref[...]
    o_ref[...] = x


def _ffn_call(x2d, outs, mods_l, g, weights, layer, g_final, seq_len, is_lat, final_norm):
    n = x2d.shape[0]
    tm = ROW_TILE
    per_seq = seq_len // tm
    mod_map = (lambda t: (1 + t // per_seq, 0, 0)) if is_lat else (lambda t: (0, 0, 0))
    tile = lambda w: pl.BlockSpec((tm, w), lambda t: (t, 0))
    res = lambda shape: pl.BlockSpec(shape, lambda t: (0, 0), pipeline_mode=pl.Buffered(1))
    wres = lambda shape: pl.BlockSpec((None,) + shape, lambda t: (layer, 0, 0), pipeline_mode=pl.Buffered(1))
    return pl.pallas_call(
        functools.partial(_ffn_kernel, final_norm=final_norm),
        grid=(n // tm,),
        in_specs=[tile(D_MODEL), tile(GW), tile(GW), tile(GW), tile(GW),
                  pl.BlockSpec((None, 1, 6 * D_MODEL), mod_map),
                  res((1, D_MODEL)), wres((D_MODEL, D_MODEL)), wres((D_MODEL, D_FF)), wres((D_MODEL, D_FF)),
                  wres((D_FF, D_MODEL)), res((1, D_MODEL))],
        out_specs=tile(D_MODEL),
        out_shape=jax.ShapeDtypeStruct((n, D_MODEL), F32),
        compiler_params=_cparams(1),
        name="outproj_ffn",
    )(x2d, *outs, mods_l, g.reshape(1, D_MODEL), *weights, g_final.reshape(1, D_MODEL))


def _grid_pos_embed(n_tokens):
    rows = n_tokens // GRID_W
    rr, cc = np.meshgrid(np.arange(rows, dtype=np.float32), np.arange(GRID_W, dtype=np.float32), indexing="ij")
    quarter = D_MODEL // 4
    omega = (1.0 / (np.float32(POS_BASE) ** (np.arange(quarter, dtype=np.float32) / quarter))).astype(np.float32)

    def enc(pos):
        ang = pos.reshape(-1)[:, None] * omega[None, :]
        return np.concatenate([np.sin(ang), np.cos(ang)], axis=-1)

    return jnp.asarray(np.concatenate([enc(rr), enc(cc)], axis=-1).astype(np.float32))


def _pad_small(vec8, offset):
    row = jnp.zeros((1, 128), F32).at[0, offset:offset + 8].set(vec8)
    return row, vec8.reshape(8, 1)


def _pack_w_in(w):
    hy_in, ssd_in, lru_in = 768, 776, 512
    o_ssd = hy_in
    o_lru = o_ssd + ssd_in
    o_gdn = o_lru + lru_in
    w = w.astype(BF16)
    small = jnp.concatenate([w[..., o_ssd + 768:o_ssd + 776],
                             w[..., o_gdn + 1024:o_gdn + 1040],
                             jnp.zeros(w.shape[:-1] + (IN_PACK - 3096,), BF16)], axis=-1)
    return (w[..., 0:o_ssd + 768],
            w[..., o_lru:o_gdn + 1024],
            small)


def _ssd_state_in(s):
    st = jnp.swapaxes(s, -1, -2)
    zero = jnp.zeros_like(st[:, :, 0])
    rows = [jnp.concatenate([st[:, :, h] if h // 2 == g else zero for h in range(NH)], axis=-1)
            for g in range(2)]
    return jnp.concatenate(rows, axis=-2)


def _gdn_state_in(s):
    zero = jnp.zeros_like(s[:, :, 0])
    rows = [jnp.concatenate([s[:, :, h] if h == g else zero for h in range(NH)], axis=-1) for g in range(NH)]
    return jnp.concatenate(rows, axis=-2)


def kernel(x_prompt, x_sample, state_lru, state_ssd, state_gdn, c, c_ctx, w_mod, b_mod, g_mix, g_ffn, g_final, w_in, w_out, hy_conv, hy_w1, hy_b1, hy_w2, hy_b2, hy_w3, hy_freq, hy_bias, ssd_conv, ssd_dt_bias, ssd_a_log, ssd_d, ssd_norm, lru_conv, lru_w_r, lru_b_r, lru_w_i, lru_b_i, lru_lambda, gdn_conv, gdn_dt_bias, gdn_a_log, gdn_norm, w_gate, w_up, w_down):
    b_ctx, l_ctx, _ = x_prompt.shape
    b_lat, l_lat, _ = x_sample.shape

    cvec = jnp.zeros((16, D_MODEL), F32).at[0].set(c_ctx).at[1:1 + b_lat].set(c)
    mods = _mod_call(cvec, w_mod, b_mod).reshape(DEPTH, 16, 1, 6 * D_MODEL)

    hw = {"w1": jnp.zeros((DEPTH, 128, 128), F32).at[:, :HY_EMB, :HY_HIDDEN].set(hy_w1),
          "b1": jnp.zeros((DEPTH, 1, 128), F32).at[:, 0, :HY_HIDDEN].set(hy_b1),
          "w2": jnp.zeros((DEPTH, 128, 128), F32).at[:, :HY_HIDDEN, :HY_HIDDEN].set(hy_w2),
          "b2": jnp.zeros((DEPTH, 1, 128), F32).at[:, 0, :HY_HIDDEN].set(hy_b2),
          "w3": jnp.zeros((DEPTH, 128, 2 * GW), F32).at[:, :HY_HIDDEN, :].set(hy_w3),
          "freq": jnp.zeros((DEPTH, 2, 128), F32).at[:, :, :HY_HIDDEN].set(hy_freq)}
    filt = {l_ctx: _hyfilt_call(l_ctx, hw), l_lat: _hyfilt_call(l_lat, hw)}

    ones_bd = jnp.asarray(np.kron(np.eye(NH, dtype=np.float32), np.ones((HD, HD), np.float32))).astype(BF16)
    pos = _grid_pos_embed(l_lat)

    x_ctx = x_prompt.reshape(b_ctx * l_ctx, D_MODEL)
    x_lat = x_sample.reshape(b_lat * l_lat, D_MODEL)
    w_pack = _pack_w_in(w_in)
    ffn_w = tuple(w.astype(BF16) for w in (w_out, w_gate, w_up, w_down))
    lru_out, ssd_out, gdn_out = [], [], []
    for l in range(DEPTH):
        ssd_dtb_r, ssd_dtb_c = _pad_small(ssd_dt_bias[l].reshape(8), SM_DT)
        ssd_an_r, ssd_an_c = _pad_small(-jnp.exp(ssd_a_log[l].reshape(8)), SM_DT)
        ssd_p = {"conv": ssd_conv[l], "dtb_row": ssd_dtb_r, "dtb_col": ssd_dtb_c, "an_row": ssd_an_r,
                 "an_col": ssd_an_c, "dvec": jnp.repeat(ssd_d[l], HD).reshape(1, GW),
                 "gn": ssd_norm[l].reshape(1, GW)}
        eye_h = jnp.eye(NH, dtype=F32)
        blockdiag = lambda w: (w[:, :, :, None, :] * eye_h[None, :, None, :, None]).reshape(2, GW, GW)
        w_r, w_i = blockdiag(lru_w_r[l]), blockdiag(lru_w_i[l])
        lru_p = {"conv": lru_conv[l],
                 "w": jnp.concatenate([w_r[0], w_r[1], w_i[0], w_i[1]], axis=1).astype(BF16),
                 "bias": jnp.concatenate([lru_b_r[l].reshape(1, 2 * GW), lru_b_i[l].reshape(1, 2 * GW)], axis=1),
                 "sp": jax.nn.softplus(-lru_lambda[l]).reshape(1, 2 * GW)}
        gdn_dtb_r, gdn_dtb_c = _pad_small(gdn_dt_bias[l].reshape(8), SM_A)
        gdn_an_r, gdn_an_c = _pad_small(-jnp.exp(gdn_a_log[l].reshape(8)), SM_A)
        gdn_p = {"conv": gdn_conv[l], "dtb_row": gdn_dtb_r, "dtb_col": gdn_dtb_c, "an_row": gdn_an_r,
                 "an_col": gdn_an_c, "gn": jnp.tile(gdn_norm[l], NH).reshape(1, GW), "ones": ones_bd}
        last = l == DEPTH - 1

        proj = _inproj_call(x_ctx, None, mods[l], g_mix[l], w_pack, l, l_ctx, False).reshape(b_ctx, l_ctx, IN_PACK)
        hy_p = {"conv": hy_conv[l], "bias": hy_bias[l].reshape(1, GW)}
        o_hy, o_lru, s_lru = _hy_lru_call(
            proj, dict(hy_p, hr=filt[l_ctx][0][l], hi=filt[l_ctx][1][l]), lru_p, None, l_ctx, 4, True)
        o_ssd, s_ssd = _ssd_call(proj, ssd_p, None, l_ctx, True)
        o_gdn, s_gdn = _gdn_call(proj, gdn_p, None, l_ctx, True)
        outs = [o.reshape(b_ctx * l_ctx, GW) for o in (o_hy, o_ssd, o_lru, o_gdn)]
        x_ctx = _ffn_call(x_ctx, outs, mods[l], g_ffn[l], ffn_w, l, g_final, l_ctx, False, last)
        lru_out.append(s_lru)
        ssd_out.append(s_ssd)
        gdn_out.append(s_gdn)

        if l == 0:
            proj, x_lat = _inproj_call(x_lat, pos, mods[l], g_mix[l], w_pack, l, l_lat, True)
        else:
            proj = _inproj_call(x_lat, None, mods[l], g_mix[l], w_pack, l, l_lat, True)
        proj = proj.reshape(b_lat, l_lat, IN_PACK)
        o_hy, o_lru = _hy_lru_call(
            proj, dict(hy_p, hr=filt[l_lat][0][l], hi=filt[l_lat][1][l]), lru_p, state_lru[:, l], l_lat, 1, False)
        o_ssd, = _ssd_call(proj, ssd_p, _ssd_state_in(state_ssd[:, l]), l_lat, False)
        o_gdn, = _gdn_call(proj, gdn_p, _gdn_state_in(state_gdn[:, l]), l_lat, False)
        outs = [o.reshape(b_lat * l_lat, GW) for o in (o_hy, o_ssd, o_lru, o_gdn)]
        x_lat = _ffn_call(x_lat, outs, mods[l], g_ffn[l], ffn_w, l, g_final, l_lat, True, last)

    return (x_ctx.reshape(b_ctx, l_ctx, D_MODEL), x_lat.reshape(b_lat, l_lat, D_MODEL),
            jnp.stack(lru_out, axis=1), jnp.stack(ssd_out, axis=1), jnp.stack(gdn_out, axis=1))
```

```python
import functools
import math

import ml_dtypes
import numpy as np
import jax
import jax.numpy as jnp
from jax import lax
from jax.experimental import pallas as pl
from jax.experimental.pallas import tpu as pltpu

F32 = jnp.float32
BF16 = jnp.bfloat16
HI = lax.Precision.HIGHEST

D_MODEL = 1024
DEPTH = 2
EPS = 1e-6
GRID_W = 64
POS_BASE = 10000.0
GW = 256
D_FF = 2816
HD = 64
NH = 4
HY_BANDS = 16
HY_EMB = 1 + 2 * HY_BANDS
HY_HIDDEN = 64
LRU_C = 8.0

COL_HY = 0
COL_SSD = 768
COL_LRU = 1536
COL_GDN = 2048
COL_SMALL = 3072
IN_PACK = 3200
SM_DT, SM_B, SM_A = 0, 8, 16

SSD_Q = 128
GDN_C = 128
ROW_TILE = 512
ROW_BLOCK = 64
NORM_BLOCK = 256
V7X_MXU_DIM = 256
FF_SPLITS = ((0, 6 * V7X_MXU_DIM), (6 * V7X_MXU_DIM, D_FF))
V7X_VMEM_LIMIT = 56 * 1024 * 1024


def _cparams(n_axes=1):
    return pltpu.CompilerParams(dimension_semantics=("arbitrary",) * n_axes,
                                vmem_limit_bytes=V7X_VMEM_LIMIT)


def _silu(x):
    return x * jax.nn.sigmoid(x)


def _softplus(x):
    return jnp.maximum(x, 0.0) + jnp.log1p(jnp.exp(-jnp.abs(x)))


def _gelu_tanh(x):
    return 0.5 * x * (1.0 + jnp.tanh(math.sqrt(2.0 / math.pi) * (x + 0.044715 * (x * x * x))))


def _split_bf16(x):
    hi = x.astype(BF16)
    lo = (x - hi.astype(F32)).astype(BF16)
    return hi, lo


def _dot(a, b):
    return jnp.dot(a, b, preferred_element_type=F32)


def _dot_hi(a, b):
    return jnp.dot(a, b, precision=HI, preferred_element_type=F32)


def _cumsum_rows(tri_b, x):
    x1 = x.astype(BF16)
    r = x - x1.astype(F32)
    x2 = r.astype(BF16)
    x3 = (r - x2.astype(F32)).astype(BF16)
    return _dot(tri_b, x1) + _dot(tri_b, x2) + _dot(tri_b, x3)


def _cumsum_cols(x, tri_b):
    x1 = x.astype(BF16)
    r = x - x1.astype(F32)
    x2 = r.astype(BF16)
    x3 = (r - x2.astype(F32)).astype(BF16)
    return _dot(x1, tri_b) + _dot(x2, tri_b) + _dot(x3, tri_b)


def _dot3_const(ah, al, x):
    xh, xl = _split_bf16(x)
    return _dot(ah, xh) + _dot(ah, xl) + _dot(al, xh)


def _seg_sum(x, ones_bd):
    xh, xl = _split_bf16(x)
    return _dot(xh, ones_bd) + _dot(xl, ones_bd)


def _pass_other_layers(so_ref, prev_ref, layer):
    for other in range(DEPTH):
        if other != layer:
            so_ref[other] = prev_ref[other] if prev_ref is not None else jnp.zeros(so_ref.shape[1:], F32)


def _lane_iota(shape):
    return lax.broadcasted_iota(jnp.int32, shape, len(shape) - 1)


def _head_lanes(cols, rows):
    lane = _lane_iota((rows, GW))
    out = jnp.broadcast_to(cols[NH - 1], (rows, GW))
    for h in range(NH - 2, -1, -1):
        out = jnp.where(lane < (h + 1) * HD, cols[h], out)
    return out


def _dwconv_tiles(load_rows, ch, w_ref, pad_ref, n_rows, taps, tile, emit):
    left = taps // 2
    pad_ref[0:8, 0:ch] = jnp.zeros((8, ch), F32)
    pad_ref[8 + n_rows:16 + n_rows, 0:ch] = jnp.zeros((8, ch), F32)
    for r0 in range(0, n_rows, tile):
        pad_ref[8 + r0:8 + r0 + tile, 0:ch] = load_rows(r0, tile)
    w_rows = [w_ref[j:j + 1, :] for j in range(taps)]
    for r0 in range(0, n_rows, tile):
        acc = None
        for j in range(taps):
            off = 8 - left + j + r0
            term = pad_ref[off:off + tile, 0:ch] * w_rows[j]
            acc = term if acc is None else acc + term
        emit(r0, acc)


def _mod_kernel(c_ref, w_ref, b_ref, o_ref):
    s_hi, s_lo = _split_bf16(_silu(c_ref[...]))
    w_hi, w_lo = _split_bf16(w_ref[...])
    o_ref[...] = _dot(s_hi, w_hi) + _dot(s_hi, w_lo) + _dot(s_lo, w_hi) + b_ref[...]


def _mod_call(cvec, w_mod, b_mod):
    tn = 1536
    n = w_mod.shape[-1]
    return pl.pallas_call(
        _mod_kernel,
        grid=(DEPTH, n // tn),
        in_specs=[pl.BlockSpec((16, D_MODEL), lambda l, j: (0, 0)),
                  pl.BlockSpec((None, D_MODEL, tn), lambda l, j: (l, 0, j)),
                  pl.BlockSpec((None, 1, tn), lambda l, j: (l, 0, j))],
        out_specs=pl.BlockSpec((None, 16, tn), lambda l, j: (l, 0, j)),
        out_shape=jax.ShapeDtypeStruct((DEPTH, 16, n), F32),
        compiler_params=_cparams(2),
        name="adaln_mod",
    )(cvec, w_mod, b_mod.reshape(DEPTH, 1, n))


def _inproj_kernel(*refs, add_pos):
    if add_pos:
        x_ref, pos_ref, mod_ref, g_ref, wa_ref, wb_ref, wc_ref, o_ref, x0_ref = refs
        x = x_ref[...] + pos_ref[...]
        x0_ref[...] = x
    else:
        x_ref, mod_ref, g_ref, wa_ref, wb_ref, wc_ref, o_ref = refs
        x = x_ref[...]
    mod = mod_ref[...]
    sh = mod[:, 0:D_MODEL]
    sc = mod[:, D_MODEL:2 * D_MODEL]
    h = x * lax.rsqrt(jnp.mean(x * x, axis=-1, keepdims=True) + EPS) * g_ref[...]
    h = (h * (1.0 + sc) + sh).astype(BF16)
    o_ref[:, 0:COL_LRU] = _dot(h, wa_ref[...])
    o_ref[:, COL_LRU:COL_SMALL] = _dot(h, wb_ref[...])
    o_ref[:, COL_SMALL:IN_PACK] = _dot(h, wc_ref[...])


def _inproj_call(x2d, pos, mods_l, g, w_pack, layer, seq_len, is_lat):
    n = x2d.shape[0]
    tm = ROW_TILE
    per_seq = max(seq_len // tm, 1)
    mod_map = (lambda t: (1 + t // per_seq, 0, 0)) if is_lat else (lambda t: (0, 0, 0))
    in_specs = [pl.BlockSpec((tm, D_MODEL), lambda t: (t, 0))]
    args = [x2d]
    if pos is not None:
        in_specs.append(pl.BlockSpec((tm, D_MODEL), lambda t: (t % per_seq, 0)))
        args.append(pos)
    in_specs += [pl.BlockSpec((None, 1, 6 * D_MODEL), mod_map),
                 pl.BlockSpec((1, D_MODEL), lambda t: (0, 0)),
                 pl.BlockSpec((None, D_MODEL, COL_LRU), lambda t: (layer, 0, 0)),
                 pl.BlockSpec((None, D_MODEL, COL_SMALL - COL_LRU), lambda t: (layer, 0, 0)),
                 pl.BlockSpec((None, D_MODEL, IN_PACK - COL_SMALL), lambda t: (layer, 0, 0))]
    args += [mods_l, g.reshape(1, D_MODEL), *w_pack]
    out_specs = pl.BlockSpec((tm, IN_PACK), lambda t: (t, 0))
    out_shape = jax.ShapeDtypeStruct((n, IN_PACK), F32)
    if pos is not None:
        out_specs = [out_specs, pl.BlockSpec((tm, D_MODEL), lambda t: (t, 0))]
        out_shape = [out_shape, jax.ShapeDtypeStruct((n, D_MODEL), F32)]
    return pl.pallas_call(
        functools.partial(_inproj_kernel, add_pos=pos is not None),
        grid=(n // tm,),
        in_specs=in_specs,
        out_specs=out_specs,
        out_shape=out_shape,
        compiler_params=_cparams(1),
        name="inproj_lat0" if pos is not None else "inproj",
    )(*args)


@functools.lru_cache(maxsize=None)
def _dft_tables(seq_len):
    n = 2 * seq_len - 1
    idx = np.arange(seq_len, dtype=np.int64)
    ang = (2.0 * np.pi / n) * ((idx[:, None] * idx[None, :]) % n).astype(np.float64)
    out = []
    for tab in (np.cos(ang), np.sin(ang)):
        hi = tab.astype(ml_dtypes.bfloat16)
        lo = (tab - hi.astype(np.float64)).astype(ml_dtypes.bfloat16)
        out += [hi, lo]
    return tuple(out)


@functools.lru_cache(maxsize=None)
def _hyena_consts(seq_len):
    t = np.linspace(0.0, 1.0, seq_len, dtype=np.float32)[:, None]
    w = np.float32(2.0 * math.pi / seq_len) * np.arange(seq_len, dtype=np.float32)[:, None]
    bands = np.linspace(1e-4, HY_BANDS - 1, HY_BANDS, dtype=np.float32)[None, :]
    feats = np.concatenate([t, np.cos(bands * w), -np.sin(bands * w)], axis=-1).astype(np.float32)
    feats_p = np.zeros((seq_len, 128), np.float32)
    feats_p[:, :HY_EMB] = feats
    max_decay = math.log(1e-2) / 0.3
    min_decay = math.log(1e-2) / 1.5
    deltas = np.abs(np.linspace(min_decay, max_decay, GW, dtype=np.float32))
    decay = np.exp(-t * deltas).astype(np.float32)
    return feats_p, np.concatenate([decay, decay], axis=1)


def _hyfilt_kernel(feat_ref, dec_ref, w1_ref, b1_ref, w2_ref, b2_ref, w3_ref, fr_ref,
                   ch_ref, cl_ref, sh_ref, sl_ref, hr_ref, hi_ref, *, seq_len):
    fr = fr_ref[...]
    h = jnp.sin(fr[0:1, :] * (_dot_hi(feat_ref[...], w1_ref[...]) + b1_ref[...]))
    h = jnp.sin(fr[1:2, :] * (_dot_hi(h, w2_ref[...]) + b2_ref[...]))
    h = _dot_hi(h, w3_ref[...]) * dec_ref[...]
    den = jnp.sum(jnp.abs(h[:, :GW]) + jnp.abs(h[:, GW:]), axis=0, keepdims=True)
    h = h / jnp.concatenate([den, den], axis=1)
    hc = _dot3_const(ch_ref[...], cl_ref[...], h)
    hs = _dot3_const(sh_ref[...], sl_ref[...], h)
    n = 2 * seq_len - 1
    row = lax.broadcasted_iota(jnp.int32, (seq_len, GW), 0)
    wgt = jnp.where(row == 0, 1.0 / n, 2.0 / n)
    hr_ref[...] = (hc[:, :GW] + hc[:, GW:]) * wgt
    hi_ref[...] = (hs[:, GW:] - hs[:, :GW]) * wgt


def _hyfilt_call(seq_len, hw):
    feats, decay = _hyena_consts(seq_len)
    tabs = _dft_tables(seq_len)
    full = lambda shape: pl.BlockSpec(shape, lambda l: (0,) * len(shape))
    lay = lambda shape: pl.BlockSpec((None,) + shape, lambda l: (l,) + (0,) * len(shape))
    return pl.pallas_call(
        functools.partial(_hyfilt_kernel, seq_len=seq_len),
        grid=(DEPTH,),
        in_specs=[full((seq_len, 128)), full((seq_len, 2 * GW)),
                  lay((128, 128)), lay((1, 128)), lay((128, 128)), lay((1, 128)), lay((128, 2 * GW)),
                  lay((2, 128))] + [full((seq_len, seq_len))] * 4,
        out_specs=[lay((seq_len, GW)), lay((seq_len, GW))],
        out_shape=[jax.ShapeDtypeStruct((DEPTH, seq_len, GW), F32)] * 2,
        compiler_params=_cparams(1),
        name=f"hyena_filter_{seq_len}",
    )(feats, decay, hw["w1"], hw["b1"], hw["w2"], hw["b2"], hw["w3"], hw["freq"], *tabs)


def _hyena_seq(load_rows, store_rows, cw_ref, bias_ref, hr_ref, hi_ref, tabs, pad_ref, x0_ref, z_ref, seq_len):
    ch_ref, cl_ref, sh_ref, sl_ref = tabs
    blk = ROW_BLOCK // 2

    def emit_conv(r0, uc):
        x0_ref[r0:r0 + blk, :] = uc[:, 0:GW]
        z_ref[r0:r0 + blk, :] = uc[:, 2 * GW:3 * GW] * uc[:, GW:2 * GW]

    _dwconv_tiles(load_rows, 3 * GW, cw_ref, pad_ref, seq_len, 3, blk, emit_conv)
    z = z_ref[...]
    za = _dot3_const(ch_ref[...], cl_ref[...], z)
    zb = _dot3_const(sh_ref[...], sl_ref[...], z)
    hr = hr_ref[...]
    hi = hi_ref[...]
    yr = za * hr + zb * hi
    yi = za * hi - zb * hr
    y = _dot3_const(ch_ref[...], cl_ref[...], yr) - _dot3_const(sh_ref[...], sl_ref[...], yi)
    bias = bias_ref[...]
    for r0 in range(0, seq_len, ROW_BLOCK):
        rows = slice(r0, r0 + ROW_BLOCK)
        store_rows(r0, x0_ref[rows, :] * (y[rows, :] + z[rows, :] * bias))


def _tri_masks(q):
    row = lax.broadcasted_iota(jnp.int32, (q, q), 0)
    col = lax.broadcasted_iota(jnp.int32, (q, q), 1)
    return col <= row, col >= row


def _ssd_kernel(*refs, seq_len, has_s0, emit_state, layer, has_prev):
    it = iter(refs)
    zx_ref, sm_ref, conv_ref, dtb_r_ref, dtb_c_ref, an_r_ref, an_c_ref, dvec_ref, gn_ref = (
        next(it) for _ in range(9))
    s0_ref = next(it) if has_s0 else None
    prev_ref = next(it) if has_prev else None
    o_ref = next(it)
    so_ref = next(it) if emit_state else None
    pad_ref, xs_ref, bm_ref, cm_ref, ac_ref, bt_ref, ar_ref, dtr_ref, y_ref, st_ref, upd_ref, ecs_ref = it
    q = SSD_Q
    nc = seq_len // q

    dvec = dvec_ref[...]

    def emit_conv(r0, acc):
        t = _silu(acc)
        rows = slice(r0, r0 + ROW_BLOCK)
        xs_ref[rows, :] = t[:, 0:GW]
        y_ref[rows, :] = t[:, 0:GW] * dvec
        bm_ref[rows, :] = t[:, GW:GW + 128]
        cm_ref[rows, :] = t[:, GW + 128:GW + 256].astype(BF16)

    _dwconv_tiles(lambda r0, nr: zx_ref[r0:r0 + nr, GW:3 * GW], 2 * GW, conv_ref, pad_ref, seq_len, 4,
                  ROW_BLOCK, emit_conv)
    sm = sm_ref[...]
    ac_ref[...] = _softplus(sm + dtb_r_ref[...]) * an_r_ref[...]
    sm_t = sm.T
    dtr = _softplus(sm_t[SM_DT:SM_DT + 8, :] + dtb_c_ref[...])
    ar = dtr * an_c_ref[...]
    for c in range(nc):
        bt_ref[c] = bm_ref[c * q:(c + 1) * q, :].T.astype(BF16)
        ar_ref[c] = ar[:, c * q:(c + 1) * q]
        dtr_ref[c] = dtr[:, c * q:(c + 1) * q]
    if has_s0:
        st_ref[...] = s0_ref[...]
    else:
        st_ref[...] = jnp.zeros(st_ref.shape, F32)

    lower, upper = _tri_masks(q)
    lo_f = lower.astype(F32).astype(BF16)
    up_f = upper.astype(F32).astype(BF16)
    lane128 = _lane_iota((q, 128))
    lane256 = _lane_iota((q, GW))
    st_row = lax.broadcasted_iota(jnp.int32, (128, GW), 0)
    st_col = lax.broadcasted_iota(jnp.int32, (128, GW), 1)
    st_mask = (st_row // HD) == (st_col // 128)
    dirs = (0, 1)
    tri_c = (lo_f, up_f)
    tri_r = (up_f, lo_f)
    masks = (lower, upper)
    last = (q - 1, 0)
    spread = [((st_row == d * NH + st_col // HD)).astype(F32).astype(BF16) for d in dirs]

    def chunk_body(ci, carry):
        rows = pl.ds(pl.multiple_of(ci * q, q), q)
        xs_b = xs_ref[rows, :].astype(BF16)
        c_c = cm_ref[rows, :]
        b_t = bt_ref[ci]
        a_c = ac_ref[rows, :]
        a_r = ar_ref[ci]
        dt_r = dtr_ref[ci]
        stack = jnp.concatenate(
            [jnp.where((lane256 // HD) == h, xs_b, jnp.zeros_like(xs_b)) for h in range(NH)], axis=0)
        c_g = jnp.concatenate([jnp.where(lane128 < HD, c_c, jnp.zeros_like(c_c)),
                               jnp.where(lane128 >= HD, c_c, jnp.zeros_like(c_c))], axis=0)
        g_mat = _dot(c_g, b_t)
        cum_c = [_cumsum_rows(tri_c[d], a_c) for d in dirs]
        cum_r = [_cumsum_cols(a_r, tri_r[d]) for d in dirs]
        w_cat, bw_cat = [], []
        for d in dirs:
            w_parts, bw_parts = [], []
            for h in range(NH):
                j = d * NH + h
                g = h // 2
                diff = cum_c[d][:, j:j + 1] - cum_r[d][j:j + 1, :]
                m = jnp.where(masks[d], jnp.exp(jnp.minimum(diff, 0.0)), 0.0)
                w_parts.append((g_mat[g * q:(g + 1) * q, :] * m * dt_r[j:j + 1, :]).astype(BF16))
                to_end = jnp.exp(cum_r[d][j:j + 1, last[d]:last[d] + 1] - cum_r[d][j:j + 1, :])
                bw_parts.append((b_t.astype(F32) * (dt_r[j:j + 1, :] * to_end)).astype(BF16))
            w_cat.append(jnp.concatenate(w_parts, axis=1))
            bw_cat.append(jnp.concatenate(bw_parts, axis=1))
        y_diag = [_dot(w_cat[d], stack) for d in dirs]
        upd = [_dot(bw_cat[d], stack) for d in dirs]
        ecs = []
        for d in dirs:
            e_hi, e_lo = _split_bf16(jnp.exp(cum_c[d]))
            ecs.append(_dot(e_hi, spread[d]) + _dot(e_lo, spread[d]))
        y_ref[rows, :] += y_diag[0] + y_diag[1]
        for d in dirs:
            upd_ref[d, ci] = jnp.where(st_mask, upd[d], 0.0)
            ecs_ref[d, ci] = ecs[d]
        return carry

    lax.fori_loop(0, nc, chunk_body, 0)

    def state_body(i, carry):
        cis = (i, nc - 1 - i)
        rows = [pl.ds(pl.multiple_of(ci * q, q), q) for ci in cis]
        ss = [st_ref[d] for d in dirs]
        y_off = [_dot(cm_ref[rows[d], :], ss[d].astype(BF16)) for d in dirs]
        for d in dirs:
            ecs = ecs_ref[d, cis[d]]
            y_ref[rows[d], :] += y_off[d] * ecs
            st_ref[d] = ss[d] * ecs[last[d]:last[d] + 1, :] + upd_ref[d, cis[d]]
        return carry

    lax.fori_loop(0, nc, state_body, 0)

    gn = gn_ref[...]
    for r0 in range(0, seq_len, ROW_BLOCK):
        rows = slice(r0, r0 + ROW_BLOCK)
        y = y_ref[rows, :] * _silu(zx_ref[rows, 0:GW])
        o_ref[rows, :] = y * lax.rsqrt(jnp.mean(y * y, axis=-1, keepdims=True) + EPS) * gn
    if emit_state:
        for d in range(2):
            st_t = st_ref[d].T
            for h in range(NH):
                blk = st_t[h * HD:(h + 1) * HD, :]
                if h // 2 == 1:
                    blk = pltpu.roll(blk, HD, axis=1)
                so_ref[layer, d, h] = blk[:, 0:HD]
        _pass_other_layers(so_ref, prev_ref, layer)


def _ssd_call(proj3, p, s0, seq_len, emit_state, layer=0, prev=None):
    bsz = proj3.shape[0]
    nc = seq_len // SSD_Q
    full = lambda shape: pl.BlockSpec(shape, lambda b: (0,) * len(shape))
    in_specs = [pl.BlockSpec((None, seq_len, 3 * GW), lambda b: (b, 0, COL_SSD // (3 * GW))),
                pl.BlockSpec((None, seq_len, 128), lambda b: (b, 0, COL_SMALL // 128)),
                full((4, 2 * GW)), full((1, 128)), full((8, 1)), full((1, 128)), full((8, 1)),
                full((1, GW)), full((1, GW))]
    args = [proj3, proj3, p["conv"], p["dtb_row"], p["dtb_col"], p["an_row"], p["an_col"], p["dvec"], p["gn"]]
    if s0 is not None:
        in_specs.append(pl.BlockSpec((None, 2, 128, GW), lambda b: (b, 0, 0, 0)))
        args.append(s0)
    out_specs = [pl.BlockSpec((None, seq_len, GW), lambda b: (b, 0, 0))]
    out_shape = [jax.ShapeDtypeStruct((bsz, seq_len, GW), F32)]
    state_spec = pl.BlockSpec((None, DEPTH, 2, NH, HD, HD), lambda b: (b, 0, 0, 0, 0, 0))
    if prev is not None:
        in_specs.append(state_spec)
        args.append(prev)
    if emit_state:
        out_specs.append(state_spec)
        out_shape.append(jax.ShapeDtypeStruct((bsz, DEPTH, 2, NH, HD, HD), F32))
    scratch = [pltpu.VMEM((seq_len + 16, 2 * GW), F32),
               pltpu.VMEM((seq_len, GW), F32),
               pltpu.VMEM((seq_len, 128), F32),
               pltpu.VMEM((seq_len, 128), BF16),
               pltpu.VMEM((seq_len, 128), F32),
               pltpu.VMEM((nc, 128, SSD_Q), BF16),
               pltpu.VMEM((nc, 8, SSD_Q), F32),
               pltpu.VMEM((nc, 8, SSD_Q), F32),
               pltpu.VMEM((seq_len, GW), F32),
               pltpu.VMEM((2, 128, GW), F32),
               pltpu.VMEM((2, nc, 128, GW), F32),
               pltpu.VMEM((2, nc, SSD_Q, GW), F32)]
    return pl.pallas_call(
        functools.partial(_ssd_kernel, seq_len=seq_len, has_s0=s0 is not None, emit_state=emit_state,
                          layer=layer, has_prev=prev is not None),
        grid=(bsz,),
        in_specs=in_specs,
        out_specs=out_specs,
        out_shape=out_shape,
        scratch_shapes=scratch,
        compiler_params=_cparams(1),
        name=f"ssd_{seq_len}",
    )(*args)


def _lru_gates(load_rows, conv_ref, w_ref, bias_ref, sp_ref, pad_ref, xc_ref, scan_refs, n):
    blk = 2 * ROW_BLOCK

    def emit_conv(r0, xc):
        xc_ref[r0:r0 + blk, :] = xc

    _dwconv_tiles(load_rows, GW, conv_ref, pad_ref, n, 4, blk, emit_conv)
    for r0 in range(0, n, blk):
        xc = xc_ref[r0:r0 + blk, :]
        xc_b = xc.astype(BF16)
        for d in range(2):
            a_ref, b_ref = scan_refs[2 * d], scan_refs[2 * d + 1]
            rc = slice(d * GW, (d + 1) * GW)
            ic = slice((2 + d) * GW, (3 + d) * GW)
            r = jax.nn.sigmoid(_dot(xc_b, w_ref[:, rc]) + bias_ref[:, rc])
            i = jax.nn.sigmoid(_dot(xc_b, w_ref[:, ic]) + bias_ref[:, ic])
            a = jnp.exp(-LRU_C * r * sp_ref[:, rc])
            a_ref[8 + r0:8 + r0 + blk, :] = a
            b_ref[8 + r0:8 + r0 + blk, :] = jnp.sqrt(jnp.maximum(1.0 - a * a, 0.0)) * i * xc


def _lru_scan(s0, scan_refs, n):
    af_ref, bf_ref, ab_ref, bb_ref = scan_refs
    ones8 = jnp.ones((8, GW), F32)
    zeros8 = jnp.zeros((8, GW), F32)
    for a_ref_, b_ref_ in ((af_ref, bf_ref), (ab_ref, bb_ref)):
        a_ref_[0:8, :] = ones8
        a_ref_[8 + n:16 + n, :] = ones8
        b_ref_[0:8, :] = zeros8
        b_ref_[8 + n:16 + n, :] = zeros8
    if s0 is not None:
        bf_ref[8:9, :] = bf_ref[8:9, :] + af_ref[8:9, :] * s0[0:1, :]
        bb_ref[7 + n:8 + n, :] = bb_ref[7 + n:8 + n, :] + ab_ref[7 + n:8 + n, :] * s0[1:2, :]

    d = 1
    while d < n:
        if d < 8:
            a_cur = af_ref[8:8 + n, :]
            af_ref[8:8 + n, :] = a_cur * af_ref[8 - d:8 - d + n, :]
            bf_ref[8:8 + n, :] = a_cur * bf_ref[8 - d:8 - d + n, :] + bf_ref[8:8 + n, :]
            a_cur = ab_ref[8:8 + n, :]
            ab_ref[8:8 + n, :] = a_cur * ab_ref[8 + d:8 + d + n, :]
            bb_ref[8:8 + n, :] = a_cur * bb_ref[8 + d:8 + d + n, :] + bb_ref[8:8 + n, :]
        else:
            m = n - d
            a_cur = af_ref[8 + d:8 + n, :]
            a_new = a_cur * af_ref[8:8 + m, :]
            b_new = a_cur * bf_ref[8:8 + m, :] + bf_ref[8 + d:8 + n, :]
            af_ref[8 + d:8 + n, :] = a_new
            bf_ref[8 + d:8 + n, :] = b_new
            a_cur = ab_ref[8:8 + m, :]
            a_new = a_cur * ab_ref[8 + d:8 + n, :]
            b_new = a_cur * bb_ref[8 + d:8 + n, :] + bb_ref[8:8 + m, :]
            ab_ref[8:8 + m, :] = a_new
            bb_ref[8:8 + m, :] = b_new
        d *= 2


def _hy_lru_kernel(*refs, seq_len, n_seq, has_s0, emit_state, layer, has_prev):
    it = iter(refs)
    (uh_ref, ul_ref, hcw_ref, hbias_ref, hr_ref, hi_ref, ch_ref, cl_ref, sh_ref, sl_ref,
     lcw_ref, lw_ref, lbias_ref, lsp_ref) = (next(it) for _ in range(14))
    s0_ref = next(it) if has_s0 else None
    prev_ref = next(it) if has_prev else None
    ohy_ref = next(it)
    olru_ref = next(it)
    so_ref = next(it) if emit_state else None
    hpad_ref, x0_ref, z_ref, lpad_ref, xc_ref, af_ref, bf_ref, ab_ref, bb_ref = it
    n = seq_len
    tabs = (ch_ref, cl_ref, sh_ref, sl_ref)
    scan_refs = (af_ref, bf_ref, ab_ref, bb_ref)
    for b in range(n_seq):
        def store_hy(r0, rows_out, b=b):
            ohy_ref[b, r0:r0 + ROW_BLOCK, :] = rows_out

        _hyena_seq(lambda r0, nr, b=b: uh_ref[b, r0:r0 + nr, :], store_hy, hcw_ref, hbias_ref, hr_ref, hi_ref,
                   tabs, hpad_ref, x0_ref, z_ref, n)
        _lru_gates(lambda r0, nr, b=b: ul_ref[b, r0:r0 + nr, 0:GW], lcw_ref, lw_ref, lbias_ref, lsp_ref,
                   lpad_ref, xc_ref, scan_refs, n)
        _lru_scan(s0_ref[b] if has_s0 else None, scan_refs, n)
        for r0 in range(0, n, ROW_BLOCK):
            rows = slice(r0, r0 + ROW_BLOCK)
            h = bf_ref[8 + r0:8 + r0 + ROW_BLOCK, :] + bb_ref[8 + r0:8 + r0 + ROW_BLOCK, :]
            olru_ref[b, rows, :] = h * _gelu_tanh(ul_ref[b, rows, GW:2 * GW])
        if emit_state:
            so_ref[b, layer, 0:1, :] = bf_ref[7 + n:8 + n, :]
            so_ref[b, layer, 1:2, :] = bb_ref[8:9, :]
            _pass_other_layers(so_ref.at[b], prev_ref.at[b] if has_prev else None, layer)


def _hy_lru_call(proj3, hy, lru, s0, seq_len, n_seq, emit_state, layer=0, prev=None):
    bsz = proj3.shape[0]
    tabs = _dft_tables(seq_len)
    full = lambda shape: pl.BlockSpec(shape, lambda b: (0,) * len(shape))
    const = lambda shape: pl.BlockSpec(shape, lambda b: (0,) * len(shape), pipeline_mode=pl.Buffered(1))
    in_specs = [pl.BlockSpec((n_seq, seq_len, 3 * GW), lambda b: (b, 0, COL_HY // (3 * GW))),
                pl.BlockSpec((n_seq, seq_len, 2 * GW), lambda b: (b, 0, COL_LRU // (2 * GW))),
                full((3, 3 * GW)), full((1, GW)), const((seq_len, GW)), const((seq_len, GW))]
    in_specs += [const((seq_len, seq_len))] * 4
    in_specs += [full((4, GW)), const((GW, 4 * GW)), full((1, 4 * GW)), full((1, 2 * GW))]
    args = [proj3, proj3, hy["conv"], hy["bias"], hy["hr"], hy["hi"], *tabs,
            lru["conv"], lru["w"], lru["bias"], lru["sp"]]
    if s0 is not None:
        in_specs.append(pl.BlockSpec((n_seq, 2, GW), lambda b: (b, 0, 0)))
        args.append(s0)
    out_specs = [pl.BlockSpec((n_seq, seq_len, GW), lambda b: (b, 0, 0))] * 2
    out_shape = [jax.ShapeDtypeStruct((bsz, seq_len, GW), F32)] * 2
    state_spec = pl.BlockSpec((n_seq, DEPTH, 2, GW), lambda b: (b, 0, 0, 0))
    if prev is not None:
        in_specs.append(state_spec)
        args.append(prev)
    if emit_state:
        out_specs.append(state_spec)
        out_shape.append(jax.ShapeDtypeStruct((bsz, DEPTH, 2, GW), F32))
    scratch = [pltpu.VMEM((seq_len + 16, 3 * GW), F32),
               pltpu.VMEM((seq_len, GW), F32),
               pltpu.VMEM((seq_len, GW), F32),
               pltpu.VMEM((seq_len + 16, GW), F32),
               pltpu.VMEM((seq_len, GW), F32)]
    scratch += [pltpu.VMEM((seq_len + 16, GW), F32)] * 4
    return pl.pallas_call(
        functools.partial(_hy_lru_kernel, seq_len=seq_len, n_seq=n_seq, has_s0=s0 is not None,
                          emit_state=emit_state, layer=layer, has_prev=prev is not None),
        grid=(bsz // n_seq,),
        in_specs=in_specs,
        out_specs=out_specs,
        out_shape=out_shape,
        scratch_shapes=scratch,
        compiler_params=_cparams(1),
        name=f"hyena_rglru_{seq_len}",
    )(*args)


TRI_BASE = 8


def _unit_tri_inverse(a_mats, lower_flags, eye, same_blk):
    c = a_mats[0].shape[0]
    ds = [jnp.where(same_blk[TRI_BASE], a, 0.0) for a in a_mats]
    d_bs = [d.astype(BF16) for d in ds]
    ps = [eye - d for d in ds]
    x_bs = [_dot(d_b, d_b).astype(BF16) for d_b in d_bs]
    rs = [_dot(x_b, jnp.concatenate([p.astype(BF16), x_b], axis=1)) for x_b, p in zip(x_bs, ps)]
    ps = [p + r[:, 0:c] for p, r in zip(ps, rs)]
    ts = [p + _dot(r[:, c:2 * c].astype(BF16), p.astype(BF16)) for p, r in zip(ps, rs)]
    size = 2 * TRI_BASE
    while size <= c:
        half = size // 2
        off = same_blk[size] & jnp.logical_not(same_blk[half])
        e_bs = [jnp.where(off, a, 0.0).astype(BF16) for a in a_mats]
        t_bs = [t.astype(BF16) for t in ts]
        starts = [[blk * size + (half if low else 0) for blk in range(c // size)] for low in lower_flags]
        rows = [jnp.concatenate([t[r0:r0 + half, :] for r0 in st], axis=0).astype(BF16)
                for t, st in zip(ts, starts)]
        us = [_dot(r, e_b).astype(BF16) for r, e_b in zip(rows, e_bs)]
        upds = [_dot(u, t_b) for u, t_b in zip(us, t_bs)]
        new_ts = []
        for t, upd, st, low in zip(ts, upds, starts, lower_flags):
            pieces = []
            for n, r0 in enumerate(st):
                changed = t[r0:r0 + half, :] - upd[n * half:(n + 1) * half, :]
                kept = t[r0 - half:r0, :] if low else t[r0 + half:r0 + size, :]
                pieces += [kept, changed] if low else [changed, kept]
            new_ts.append(jnp.concatenate(pieces, axis=0))
        ts = new_ts
        size *= 2
    return ts


def _gdn_kernel(*refs, seq_len, has_s0, emit_state, layer, has_prev):
    it = iter(refs)
    u_ref, sm_ref, conv_ref, dtb_r_ref, dtb_c_ref, an_r_ref, an_c_ref, gn_ref, ones_ref = (
        next(it) for _ in range(9))
    s0_ref = next(it) if has_s0 else None
    prev_ref = next(it) if has_prev else None
    o_ref = next(it)
    so_ref = next(it) if emit_state else None
    (pad_ref, q_ref, k_ref, v_ref, be_ref, gc_ref, gr_ref, oa_ref, st_ref,
     us_ref, wq_ref, at_ref, kd_ref, dr_ref) = it
    c = GDN_C
    nc = seq_len // c
    ones_bd = ones_ref[...]

    def emit_conv(r0, acc):
        t = _silu(acc)
        rows = slice(r0, r0 + ROW_BLOCK // 2)
        q_ref[rows, :] = t[:, 0:GW]
        k_ref[rows, :] = t[:, GW:2 * GW]
        v_ref[rows, :] = t[:, 2 * GW:3 * GW]

    _dwconv_tiles(lambda r0, nr: u_ref[r0:r0 + nr, 0:3 * GW], 3 * GW, conv_ref, pad_ref, seq_len, 4,
                  ROW_BLOCK // 2, emit_conv)
    for r0 in range(0, seq_len, NORM_BLOCK):
        rows = slice(r0, r0 + NORM_BLOCK)
        qq = q_ref[rows, :]
        kk = k_ref[rows, :]
        q_ref[rows, :] = qq * lax.rsqrt(_seg_sum(qq * qq, ones_bd) + EPS) * (HD ** -0.5)
        k_ref[rows, :] = kk * lax.rsqrt(_seg_sum(kk * kk, ones_bd) + EPS)
    sm = sm_ref[...]
    be_ref[...] = jax.nn.sigmoid(sm)
    gc_ref[...] = an_r_ref[...] * _softplus(sm + dtb_r_ref[...])
    sm_t = sm.T
    g_r = an_c_ref[...] * _softplus(sm_t[SM_A:SM_A + 8, :] + dtb_c_ref[...])
    for i in range(nc):
        gr_ref[i] = g_r[:, i * c:(i + 1) * c]
    oa_ref[...] = jnp.zeros(oa_ref.shape, F32)
    if has_s0:
        st_ref[...] = s0_ref[...]
    else:
        st_ref[...] = jnp.zeros(st_ref.shape, F32)

    lower, upper = _tri_masks(c)
    lo_f = lower.astype(F32).astype(BF16)
    up_f = upper.astype(F32).astype(BF16)
    row_i =lax.broadcasted_iota(jnp.int32, (c, c), 0)
    col_i = lax.broadcasted_iota(jnp.int32, (c, c), 1)
    eye = (row_i == col_i).astype(F32)
    same_blk = {}
    size = TRI_BASE
    while size <= c:
        same_blk[size] = (row_i // size) == (col_i // size)
        size *= 2
    lane256 = _lane_iota((c, GW))
    lane512 = _lane_iota((c, 2 * GW))
    bd_row = lax.broadcasted_iota(jnp.int32, (GW, GW), 0)
    bd_col = lax.broadcasted_iota(jnp.int32, (GW, GW), 1)
    bd_mask = (bd_row // HD) == (bd_col // HD)
    kt_row = lax.broadcasted_iota(jnp.int32, (GW, c), 0)

    def stack_heads(x_b, lane):
        zero = jnp.zeros_like(x_b)
        return jnp.concatenate([jnp.where(((lane % GW) // HD) == h, x_b, zero) for h in range(NH)], axis=0)

    dirs = (0, 1)
    tri_c = (lo_f, up_f)
    tri_r = (up_f, lo_f)
    incl = (lower, upper)
    strict = (col_i < row_i, col_i > row_i)
    last = (c - 1, 0)

    chunks_per_step = 2

    def solve_body(step, carry):
        cks = [step * chunks_per_step + n for n in range(chunks_per_step)]
        units = [(n, d) for n in range(chunks_per_step) for d in dirs]
        rows = [pl.ds(pl.multiple_of(ci * c, c), c) for ci in cks]
        q_c = [q_ref[r, :] for r in rows]
        k_c = [k_ref[r, :] for r in rows]
        v_c = [v_ref[r, :] for r in rows]
        be = [be_ref[r, :] for r in rows]
        g_c = [gc_ref[r, :] for r in rows]
        g_r = [gr_ref[ci] for ci in cks]
        cum_c = {(n, d): _cumsum_rows(tri_c[d], g_c[n]) for n, d in units}
        cum_r = {(n, d): _cumsum_cols(g_r[n], tri_r[d]) for n, d in units}
        kq = []
        for n in range(chunks_per_step):
            k_t = k_c[n].T.astype(BF16)
            k_t_heads = jnp.concatenate(
                [jnp.where((kt_row // HD) == h, k_t, jnp.zeros_like(k_t)) for h in range(NH)], axis=1)
            kq.append(_dot(jnp.concatenate([k_c[n], q_c[n]], axis=0).astype(BF16), k_t_heads))
        a_mats, attn, flags = [], {}, []
        for n, d in units:
            for h in range(NH):
                j = d * NH + h
                diff = cum_c[n, d][:, SM_A + j:SM_A + j + 1] - cum_r[n, d][j:j + 1, :]
                dec = jnp.where(incl[d], jnp.exp(jnp.minimum(diff, 0.0)), 0.0)
                a_mats.append(jnp.where(
                    strict[d], kq[n][0:c, h * c:(h + 1) * c] * be[n][:, SM_B + j:SM_B + j + 1] * dec, 0.0))
                attn[n, d, h] = (kq[n][c:2 * c, h * c:(h + 1) * c] * dec).astype(BF16)
                flags.append(d == 0)
        t_mats = _unit_tri_inverse(a_mats, flags, eye, same_blk)
        t_cat = {u: jnp.concatenate([t.astype(BF16) for t in t_mats[idx * NH:(idx + 1) * NH]], axis=1)
                 for idx, u in enumerate(units)}
        cols = {(n, d): [cum_c[n, d][:, SM_A + d * NH + h:SM_A + d * NH + h + 1] for h in range(NH)]
                for n, d in units}
        lasts = {(n, d): [cum_c[n, d][last[d]:last[d] + 1, SM_A + d * NH + h:SM_A + d * NH + h + 1]
                          for h in range(NH)] for n, d in units}
        b_l = {(n, d): _head_lanes([be[n][:, SM_B + d * NH + h:SM_B + d * NH + h + 1] for h in range(NH)], c)
               for n, d in units}
        e_l = {(n, d): _head_lanes([jnp.exp(cc) for cc in cols[n, d]], c) for n, d in units}
        rhs = {(n, d): jnp.concatenate([v_c[n] * b_l[n, d], k_c[n] * b_l[n, d] * e_l[n, d]], axis=1)
               for n, d in units}
        uw = {u: _dot(t_cat[u], stack_heads(rhs[u].astype(BF16), lane512)) for u in units}
        for n, d in units:
            ci = cks[n]
            us_ref[d, ci] = uw[n, d][:, 0:GW]
            wq_ref[d, ci, 0:c, :] = uw[n, d][:, GW:2 * GW].astype(BF16)
            wq_ref[d, ci, c:2 * c, :] = (q_c[n] * e_l[n, d]).astype(BF16)
            at_ref[d, ci] = jnp.concatenate([attn[n, d, h] for h in range(NH)], axis=1)
            k_dec = k_c[n] * _head_lanes([jnp.exp(lasts[n, d][h] - cols[n, d][h]) for h in range(NH)], c)
            kd_ref[d, ci] = k_dec.T.astype(BF16)
            dr_ref[d, ci] = _head_lanes([jnp.exp(lv) for lv in lasts[n, d]], 1)
        return carry

    lax.fori_loop(0, nc // chunks_per_step, solve_body, 0)

    def state_body(i, carry):
        cis = (i, nc - 1 - i)
        ss = [st_ref[d] for d in dirs]
        r2 = [_dot(wq_ref[d, cis[d]], ss[d].astype(BF16)) for d in dirs]
        v_new_b = [(us_ref[d, cis[d]] - r2[d][0:c, :]).astype(BF16) for d in dirs]
        upd = [_dot(kd_ref[d, cis[d]], v_new_b[d]) for d in dirs]
        o_c = [r2[d][c:2 * c, :] + _dot(at_ref[d, cis[d]], stack_heads(v_new_b[d], lane256)) for d in dirs]
        for d in dirs:
            st_ref[d] = ss[d] * dr_ref[d, cis[d]] + jnp.where(bd_mask, upd[d], 0.0)
            rows = pl.ds(pl.multiple_of(cis[d] * c, c), c)
            oa_ref[rows, :] += o_c[d]
        return carry

    lax.fori_loop(0, nc, state_body, 0)

    gn = gn_ref[...]
    for r0 in range(0, seq_len, NORM_BLOCK):
        rows = slice(r0, r0 + NORM_BLOCK)
        o = oa_ref[rows, :]
        o = o * lax.rsqrt(_seg_sum(o * o, ones_bd) * (1.0 / HD) + EPS) * gn
        o_ref[rows, :] = o * _silu(u_ref[rows, 3 * GW:4 * GW])
    if emit_state:
        for d in range(2):
            for h in range(NH):
                blk = st_ref[d, h * HD:(h + 1) * HD, (h // 2) * 128:(h // 2 + 1) * 128]
                if h % 2 == 1:
                    blk = pltpu.roll(blk, HD, axis=1)
                so_ref[layer, d, h] = blk[:, 0:HD]
        _pass_other_layers(so_ref, prev_ref, layer)


def _gdn_call(proj3, p, s0, seq_len, emit_state, layer=0, prev=None):
    bsz = proj3.shape[0]
    nc = seq_len // GDN_C
    full = lambda shape: pl.BlockSpec(shape, lambda b: (0,) * len(shape))
    in_specs = [pl.BlockSpec((None, seq_len, 4 * GW), lambda b: (b, 0, COL_GDN // (4 * GW))),
                pl.BlockSpec((None, seq_len, 128), lambda b: (b, 0, COL_SMALL // 128)),
                full((4, 3 * GW)), full((1, 128)), full((8, 1)), full((1, 128)), full((8, 1)),
                full((1, GW)), full((GW, GW))]
    args = [proj3, proj3, p["conv"], p["dtb_row"], p["dtb_col"], p["an_row"], p["an_col"], p["gn"], p["ones"]]
    if s0 is not None:
        in_specs.append(pl.BlockSpec((None, 2, GW, GW), lambda b: (b, 0, 0, 0)))
        args.append(s0)
    out_specs = [pl.BlockSpec((None, seq_len, GW), lambda b: (b, 0, 0))]
    out_shape = [jax.ShapeDtypeStruct((bsz, seq_len, GW), F32)]
    state_spec = pl.BlockSpec((None, DEPTH, 2, NH, HD, HD), lambda b: (b, 0, 0, 0, 0, 0))
    if prev is not None:
        in_specs.append(state_spec)
        args.append(prev)
    if emit_state:
        out_specs.append(state_spec)
        out_shape.append(jax.ShapeDtypeStruct((bsz, DEPTH, 2, NH, HD, HD), F32))
    scratch = [pltpu.VMEM((seq_len + 16, 3 * GW), F32),
               pltpu.VMEM((seq_len, GW), F32),
               pltpu.VMEM((seq_len, GW), F32),
               pltpu.VMEM((seq_len, GW), F32),
               pltpu.VMEM((seq_len, 128), F32),
               pltpu.VMEM((seq_len, 128), F32),
               pltpu.VMEM((nc, 8, GDN_C), F32),
               pltpu.VMEM((seq_len, GW), F32),
               pltpu.VMEM((2, GW, GW), F32),
               pltpu.VMEM((2, nc, GDN_C, GW), F32),
               pltpu.VMEM((2, nc, 2 * GDN_C, GW), BF16),
               pltpu.VMEM((2, nc, GDN_C, NH * GDN_C), BF16),
               pltpu.VMEM((2, nc, GW, GDN_C), BF16),
               pltpu.VMEM((2, nc, 1, GW), F32)]
    return pl.pallas_call(
        functools.partial(_gdn_kernel, seq_len=seq_len, has_s0=s0 is not None, emit_state=emit_state,
                          layer=layer, has_prev=prev is not None),
        grid=(bsz,),
        in_specs=in_specs,
        out_specs=out_specs,
        out_shape=out_shape,
        scratch_shapes=scratch,
        compiler_params=_cparams(1),
        name=f"gdn_{seq_len}",
    )(*args)


def _ffn_kernel(x_ref, ohy_ref, ossd_ref, olru_ref, ogdn_ref, mod_ref, g_ref, wo_ref, wg_ref, wu_ref,
                wd_ref, gfin_ref, o_ref, *, final_norm):
    mod = mod_ref[...]
    ga_m = mod[:, 2 * D_MODEL:3 * D_MODEL]
    sh_f = mod[:, 3 * D_MODEL:4 * D_MODEL]
    sc_f = mod[:, 4 * D_MODEL:5 * D_MODEL]
    ga_f = mod[:, 5 * D_MODEL:6 * D_MODEL]
    mo = None
    for i, r in enumerate((ohy_ref, ossd_ref, olru_ref, ogdn_ref)):
        part = _dot(r[...].astype(BF16), wo_ref[i * GW:(i + 1) * GW, :])
        mo = part if mo is None else mo + part
    x = x_ref[...] + ga_m * mo
    h = x * lax.rsqrt(jnp.mean(x * x, axis=-1, keepdims=True) + EPS) * g_ref[...]
    h = (h * (1.0 + sc_f) + sh_f).astype(BF16)
    ff = None
    for lo, hi in FF_SPLITS:
        gate = _dot(h, wg_ref[:, lo:hi])
        up = _dot(h, wu_ref[:, lo:hi])
        part = _dot((_silu(gate) * up).astype(BF16), wd_ref[lo:hi, :])
        ff = part if ff is None else ff + part
    x = x + ga_f * ff
    if final_norm:
        x = x * lax.rsqrt(jnp.mean(x * x, axis=-1, keepdims=True) + EPS) * gfin_ref[...]
    o_ref[...] = x


def _ffn_call(x2d, outs, mods_l, g, weights, layer, g_final, seq_len, is_lat, final_norm):
    n = x2d.shape[0]
    tm = ROW_TILE
    per_seq = seq_len // tm
    mod_map = (lambda t: (1 + t // per_seq, 0, 0)) if is_lat else (lambda t: (0, 0, 0))
    tile = lambda w: pl.BlockSpec((tm, w), lambda t: (t, 0))
    res = lambda shape: pl.BlockSpec(shape, lambda t: (0, 0), pipeline_mode=pl.Buffered(1))
    wres = lambda shape: pl.BlockSpec((None,) + shape, lambda t: (layer, 0, 0), pipeline_mode=pl.Buffered(1))
    return pl.pallas_call(
        functools.partial(_ffn_kernel, final_norm=final_norm),
        grid=(n // tm,),
        in_specs=[tile(D_MODEL), tile(GW), tile(GW), tile(GW), tile(GW),
                  pl.BlockSpec((None, 1, 6 * D_MODEL), mod_map),
                  res((1, D_MODEL)), wres((D_MODEL, D_MODEL)), wres((D_MODEL, D_FF)), wres((D_MODEL, D_FF)),
                  wres((D_FF, D_MODEL)), res((1, D_MODEL))],
        out_specs=tile(D_MODEL),
        out_shape=jax.ShapeDtypeStruct((n, D_MODEL), F32),
        compiler_params=_cparams(1),
        name="outproj_ffn",
    )(x2d, *outs, mods_l, g.reshape(1, D_MODEL), *weights, g_final.reshape(1, D_MODEL))


def _grid_pos_embed(n_tokens):
    rows = n_tokens // GRID_W
    rr, cc = np.meshgrid(np.arange(rows, dtype=np.float32), np.arange(GRID_W, dtype=np.float32), indexing="ij")
    quarter = D_MODEL // 4
    omega = (1.0 / (np.float32(POS_BASE) ** (np.arange(quarter, dtype=np.float32) / quarter))).astype(np.float32)

    def enc(pos):
        ang = pos.reshape(-1)[:, None] * omega[None, :]
        return np.concatenate([np.sin(ang), np.cos(ang)], axis=-1)

    return jnp.asarray(np.concatenate([enc(rr), enc(cc)], axis=-1).astype(np.float32))


def _pad_small(vec8, offset):
    row = jnp.zeros((1, 128), F32).at[0, offset:offset + 8].set(vec8)
    return row, vec8.reshape(8, 1)


def _pack_w_in(w):
    hy_in, ssd_in, lru_in = 768, 776, 512
    o_ssd = hy_in
    o_lru = o_ssd + ssd_in
    o_gdn = o_lru + lru_in
    w = w.astype(BF16)
    small = jnp.concatenate([w[..., o_ssd + 768:o_ssd + 776],
                             w[..., o_gdn + 1024:o_gdn + 1040],
                             jnp.zeros(w.shape[:-1] + (IN_PACK - 3096,), BF16)], axis=-1)
    return (w[..., 0:o_ssd + 768],
            w[..., o_lru:o_gdn + 1024],
            small)


def _ssd_state_in(s):
    st = jnp.swapaxes(s, -1, -2)
    zero = jnp.zeros_like(st[:, :, 0])
    rows = [jnp.concatenate([st[:, :, h] if h // 2 == g else zero for h in range(NH)], axis=-1)
            for g in range(2)]
    return jnp.concatenate(rows, axis=-2)


def _gdn_state_in(s):
    zero = jnp.zeros_like(s[:, :, 0])
    rows = [jnp.concatenate([s[:, :, h] if h == g else zero for h in range(NH)], axis=-1) for g in range(NH)]
    return jnp.concatenate(rows, axis=-2)


def kernel(x_prompt, x_sample, state_lru, state_ssd, state_gdn, c, c_ctx, w_mod, b_mod, g_mix, g_ffn, g_final, w_in, w_out, hy_conv, hy_w1, hy_b1, hy_w2, hy_b2, hy_w3, hy_freq, hy_bias, ssd_conv, ssd_dt_bias, ssd_a_log, ssd_d, ssd_norm, lru_conv, lru_w_r, lru_b_r, lru_w_i, lru_b_i, lru_lambda, gdn_conv, gdn_dt_bias, gdn_a_log, gdn_norm, w_gate, w_up, w_down):
    b_ctx, l_ctx, _ = x_prompt.shape
    b_lat, l_lat, _ = x_sample.shape

    cvec = jnp.zeros((16, D_MODEL), F32).at[0].set(c_ctx).at[1:1 + b_lat].set(c)
    mods = _mod_call(cvec, w_mod, b_mod).reshape(DEPTH, 16, 1, 6 * D_MODEL)

    hw = {"w1": jnp.zeros((DEPTH, 128, 128), F32).at[:, :HY_EMB, :HY_HIDDEN].set(hy_w1),
          "b1": jnp.zeros((DEPTH, 1, 128), F32).at[:, 0, :HY_HIDDEN].set(hy_b1),
          "w2": jnp.zeros((DEPTH, 128, 128), F32).at[:, :HY_HIDDEN, :HY_HIDDEN].set(hy_w2),
          "b2": jnp.zeros((DEPTH, 1, 128), F32).at[:, 0, :HY_HIDDEN].set(hy_b2),
          "w3": jnp.zeros((DEPTH, 128, 2 * GW), F32).at[:, :HY_HIDDEN, :].set(hy_w3),
          "freq": jnp.zeros((DEPTH, 2, 128), F32).at[:, :, :HY_HIDDEN].set(hy_freq)}
    filt = {l_ctx: _hyfilt_call(l_ctx, hw), l_lat: _hyfilt_call(l_lat, hw)}

    ones_bd = jnp.asarray(np.kron(np.eye(NH, dtype=np.float32), np.ones((HD, HD), np.float32))).astype(BF16)
    pos = _grid_pos_embed(l_lat)

    x_ctx = x_prompt.reshape(b_ctx * l_ctx, D_MODEL)
    x_lat = x_sample.reshape(b_lat * l_lat, D_MODEL)
    w_pack = _pack_w_in(w_in)
    ffn_w = tuple(w.astype(BF16) for w in (w_out, w_gate, w_up, w_down))
    s_lru = s_ssd = s_gdn = None
    for l in range(DEPTH):
        ssd_dtb_r, ssd_dtb_c = _pad_small(ssd_dt_bias[l].reshape(8), SM_DT)
        ssd_an_r, ssd_an_c = _pad_small(-jnp.exp(ssd_a_log[l].reshape(8)), SM_DT)
        ssd_p = {"conv": ssd_conv[l], "dtb_row": ssd_dtb_r, "dtb_col": ssd_dtb_c, "an_row": ssd_an_r,
                 "an_col": ssd_an_c, "dvec": jnp.repeat(ssd_d[l], HD).reshape(1, GW),
                 "gn": ssd_norm[l].reshape(1, GW)}
        eye_h = jnp.eye(NH, dtype=F32)
        blockdiag = lambda w: (w[:, :, :, None, :] * eye_h[None, :, None, :, None]).reshape(2, GW, GW)
        w_r, w_i = blockdiag(lru_w_r[l]), blockdiag(lru_w_i[l])
        lru_p = {"conv": lru_conv[l],
                 "w": jnp.concatenate([w_r[0], w_r[1], w_i[0], w_i[1]], axis=1).astype(BF16),
                 "bias": jnp.concatenate([lru_b_r[l].reshape(1, 2 * GW), lru_b_i[l].reshape(1, 2 * GW)], axis=1),
                 "sp": jax.nn.softplus(-lru_lambda[l]).reshape(1, 2 * GW)}
        gdn_dtb_r, gdn_dtb_c = _pad_small(gdn_dt_bias[l].reshape(8), SM_A)
        gdn_an_r, gdn_an_c = _pad_small(-jnp.exp(gdn_a_log[l].reshape(8)), SM_A)
        gdn_p = {"conv": gdn_conv[l], "dtb_row": gdn_dtb_r, "dtb_col": gdn_dtb_c, "an_row": gdn_an_r,
                 "an_col": gdn_an_c, "gn": jnp.tile(gdn_norm[l], NH).reshape(1, GW), "ones": ones_bd}
        last = l == DEPTH - 1

        proj = _inproj_call(x_ctx, None, mods[l], g_mix[l], w_pack, l, l_ctx, False).reshape(b_ctx, l_ctx, IN_PACK)
        hy_p = {"conv": hy_conv[l], "bias": hy_bias[l].reshape(1, GW)}
        o_hy, o_lru, s_lru = _hy_lru_call(
            proj, dict(hy_p, hr=filt[l_ctx][0][l], hi=filt[l_ctx][1][l]), lru_p, None, l_ctx, 4, True, l, s_lru)
        o_ssd, s_ssd = _ssd_call(proj, ssd_p, None, l_ctx, True, l, s_ssd)
        o_gdn, s_gdn = _gdn_call(proj, gdn_p, None, l_ctx, True, l, s_gdn)
        outs = [o.reshape(b_ctx * l_ctx, GW) for o in (o_hy, o_ssd, o_lru, o_gdn)]
        x_ctx = _ffn_call(x_ctx, outs, mods[l], g_ffn[l], ffn_w, l, g_final, l_ctx, False, last)

        if l == 0:
            proj, x_lat = _inproj_call(x_lat, pos, mods[l], g_mix[l], w_pack, l, l_lat, True)
        else:
            proj = _inproj_call(x_lat, None, mods[l], g_mix[l], w_pack, l, l_lat, True)
        proj = proj.reshape(b_lat, l_lat, IN_PACK)
        o_hy, o_lru = _hy_lru_call(
            proj, dict(hy_p, hr=filt[l_lat][0][l], hi=filt[l_lat][1][l]), lru_p, state_lru[:, l], l_lat, 1, False)
        o_ssd, = _ssd_call(proj, ssd_p, _ssd_state_in(state_ssd[:, l]), l_lat, False)
        o_gdn, = _gdn_call(proj, gdn_p, _gdn_state_in(state_gdn[:, l]), l_lat, False)
        outs = [o.reshape(b_lat * l_lat, GW) for o in (o_hy, o_ssd, o_lru, o_gdn)]
        x_lat = _ffn_call(x_lat, outs, mods[l], g_ffn[l], ffn_w, l, g_final, l_lat, True, last)

    return (x_ctx.reshape(b_ctx, l_ctx, D_MODEL), x_lat.reshape(b_lat, l_lat, D_MODEL), s_lru, s_ssd, s_gdn)
```

```python
import functools
import math

import ml_dtypes
import numpy as np
import jax
import jax.numpy as jnp
from jax import lax
from jax.experimental import pallas as pl
from jax.experimental.pallas import tpu as pltpu

F32 = jnp.float32
BF16 = jnp.bfloat16
HI = lax.Precision.HIGHEST

D_MODEL = 1024
DEPTH = 2
EPS = 1e-6
GRID_W = 64
POS_BASE = 10000.0
GW = 256
D_FF = 2816
HD = 64
NH = 4
HY_BANDS = 16
HY_EMB = 1 + 2 * HY_BANDS
HY_HIDDEN = 64
LRU_C = 8.0

COL_HY = 0
COL_SSD = 768
COL_LRU = 1536
COL_GDN = 2048
COL_SMALL = 3072
IN_PACK = 3200
SM_DT, SM_B, SM_A = 0, 8, 16

SSD_Q = 128
GDN_C = 128
ROW_TILE = 512
ROW_BLOCK = 64
NORM_BLOCK = 256
V7X_MXU_DIM = 256
FF_SPLITS = ((0, 6 * V7X_MXU_DIM), (6 * V7X_MXU_DIM, D_FF))
V7X_VMEM_LIMIT = 56 * 1024 * 1024


def _cparams(n_axes=1):
    return pltpu.CompilerParams(dimension_semantics=("arbitrary",) * n_axes,
                                vmem_limit_bytes=V7X_VMEM_LIMIT)


def _silu(x):
    return x * jax.nn.sigmoid(x)


def _softplus(x):
    return jnp.maximum(x, 0.0) + jnp.log1p(jnp.exp(-jnp.abs(x)))


def _gelu_tanh(x):
    return 0.5 * x * (1.0 + jnp.tanh(math.sqrt(2.0 / math.pi) * (x + 0.044715 * (x * x * x))))


def _split_bf16(x):
    hi = x.astype(BF16)
    lo = (x - hi.astype(F32)).astype(BF16)
    return hi, lo


def _dot(a, b):
    return jnp.dot(a, b, preferred_element_type=F32)


def _dot_hi(a, b):
    return jnp.dot(a, b, precision=HI, preferred_element_type=F32)


def _cumsum_rows(tri_b, x):
    x1 = x.astype(BF16)
    r = x - x1.astype(F32)
    x2 = r.astype(BF16)
    x3 = (r - x2.astype(F32)).astype(BF16)
    return _dot(tri_b, x1) + _dot(tri_b, x2) + _dot(tri_b, x3)


def _cumsum_cols(x, tri_b):
    x1 = x.astype(BF16)
    r = x - x1.astype(F32)
    x2 = r.astype(BF16)
    x3 = (r - x2.astype(F32)).astype(BF16)
    return _dot(x1, tri_b) + _dot(x2, tri_b) + _dot(x3, tri_b)


def _dot3_const(ah, al, x):
    xh, xl = _split_bf16(x)
    return _dot(ah, xh) + _dot(ah, xl) + _dot(al, xh)


def _seg_sum(x, ones_bd):
    xh, xl = _split_bf16(x)
    return _dot(xh, ones_bd) + _dot(xl, ones_bd)


def _pass_other_layers(so_ref, prev_ref, layer):
    for other in range(DEPTH):
        if other != layer:
            so_ref[other] = prev_ref[other] if prev_ref is not None else jnp.zeros(so_ref.shape[1:], F32)


def _lane_iota(shape):
    return lax.broadcasted_iota(jnp.int32, shape, len(shape) - 1)


def _head_lanes(cols, rows):
    lane = _lane_iota((rows, GW))
    out = jnp.broadcast_to(cols[NH - 1], (rows, GW))
    for h in range(NH - 2, -1, -1):
        out = jnp.where(lane < (h + 1) * HD, cols[h], out)
    return out


def _dwconv_tiles(load_rows, ch, w_ref, pad_ref, n_rows, taps, tile, emit):
    left = taps // 2
    pad_ref[0:8, 0:ch] = jnp.zeros((8, ch), F32)
    pad_ref[8 + n_rows:16 + n_rows, 0:ch] = jnp.zeros((8, ch), F32)
    for r0 in range(0, n_rows, tile):
        pad_ref[8 + r0:8 + r0 + tile, 0:ch] = load_rows(r0, tile)
    w_rows = [w_ref[j:j + 1, :] for j in range(taps)]
    for r0 in range(0, n_rows, tile):
        acc = None
        for j in range(taps):
            off = 8 - left + j + r0
            term = pad_ref[off:off + tile, 0:ch] * w_rows[j]
            acc = term if acc is None else acc + term
        emit(r0, acc)


def _mod_kernel(c_ref, w_ref, b_ref, o_ref):
    s_hi, s_lo = _split_bf16(_silu(c_ref[...]))
    w_hi, w_lo = _split_bf16(w_ref[...])
    o_ref[...] = _dot(s_hi, w_hi) + _dot(s_hi, w_lo) + _dot(s_lo, w_hi) + b_ref[...]


def _mod_call(cvec, w_mod, b_mod):
    tn = 1536
    n = w_mod.shape[-1]
    return pl.pallas_call(
        _mod_kernel,
        grid=(DEPTH, n // tn),
        in_specs=[pl.BlockSpec((16, D_MODEL), lambda l, j: (0, 0)),
                  pl.BlockSpec((None, D_MODEL, tn), lambda l, j: (l, 0, j)),
                  pl.BlockSpec((None, 1, tn), lambda l, j: (l, 0, j))],
        out_specs=pl.BlockSpec((None, 16, tn), lambda l, j: (l, 0, j)),
        out_shape=jax.ShapeDtypeStruct((DEPTH, 16, n), F32),
        compiler_params=_cparams(2),
        name="adaln_mod",
    )(cvec, w_mod, b_mod.reshape(DEPTH, 1, n))


def _inproj_kernel(*refs, add_pos):
    if add_pos:
        x_ref, pos_ref, mod_ref, g_ref, wa_ref, wb_ref, wc_ref, o_ref, x0_ref = refs
        x = x_ref[...] + pos_ref[...]
        x0_ref[...] = x
    else:
        x_ref, mod_ref, g_ref, wa_ref, wb_ref, wc_ref, o_ref = refs
        x = x_ref[...]
    mod = mod_ref[...]
    sh = mod[:, 0:D_MODEL]
    sc = mod[:, D_MODEL:2 * D_MODEL]
    h = x * lax.rsqrt(jnp.mean(x * x, axis=-1, keepdims=True) + EPS) * g_ref[...]
    h = (h * (1.0 + sc) + sh).astype(BF16)
    o_ref[:, 0:COL_LRU] = _dot(h, wa_ref[...])
    o_ref[:, COL_LRU:COL_SMALL] = _dot(h, wb_ref[...])
    o_ref[:, COL_SMALL:IN_PACK] = _dot(h, wc_ref[...])


def _inproj_call(x2d, pos, mods_l, g, w_pack, layer, seq_len, is_lat):
    n = x2d.shape[0]
    tm = ROW_TILE
    per_seq = max(seq_len // tm, 1)
    mod_map = (lambda t: (1 + t // per_seq, 0, 0)) if is_lat else (lambda t: (0, 0, 0))
    in_specs = [pl.BlockSpec((tm, D_MODEL), lambda t: (t, 0))]
    args = [x2d]
    if pos is not None:
        in_specs.append(pl.BlockSpec((tm, D_MODEL), lambda t: (t % per_seq, 0)))
        args.append(pos)
    in_specs += [pl.BlockSpec((None, 1, 6 * D_MODEL), mod_map),
                 pl.BlockSpec((1, D_MODEL), lambda t: (0, 0)),
                 pl.BlockSpec((None, D_MODEL, COL_LRU), lambda t: (layer, 0, 0)),
                 pl.BlockSpec((None, D_MODEL, COL_SMALL - COL_LRU), lambda t: (layer, 0, 0)),
                 pl.BlockSpec((None, D_MODEL, IN_PACK - COL_SMALL), lambda t: (layer, 0, 0))]
    args += [mods_l, g.reshape(1, D_MODEL), *w_pack]
    out_specs = pl.BlockSpec((tm, IN_PACK), lambda t: (t, 0))
    out_shape = jax.ShapeDtypeStruct((n, IN_PACK), F32)
    if pos is not None:
        out_specs = [out_specs, pl.BlockSpec((tm, D_MODEL), lambda t: (t, 0))]
        out_shape = [out_shape, jax.ShapeDtypeStruct((n, D_MODEL), F32)]
    return pl.pallas_call(
        functools.partial(_inproj_kernel, add_pos=pos is not None),
        grid=(n // tm,),
        in_specs=in_specs,
        out_specs=out_specs,
        out_shape=out_shape,
        compiler_params=_cparams(1),
        name="inproj_lat0" if pos is not None else "inproj",
    )(*args)


@functools.lru_cache(maxsize=None)
def _dft_tables(seq_len):
    n = 2 * seq_len - 1
    idx = np.arange(seq_len, dtype=np.int64)
    ang = (2.0 * np.pi / n) * ((idx[:, None] * idx[None, :]) % n).astype(np.float64)
    out = []
    for tab in (np.cos(ang), np.sin(ang)):
        hi = tab.astype(ml_dtypes.bfloat16)
        lo = (tab - hi.astype(np.float64)).astype(ml_dtypes.bfloat16)
        out += [hi, lo]
    return tuple(out)


@functools.lru_cache(maxsize=None)
def _hyena_consts(seq_len):
    t = np.linspace(0.0, 1.0, seq_len, dtype=np.float32)[:, None]
    w = np.float32(2.0 * math.pi / seq_len) * np.arange(seq_len, dtype=np.float32)[:, None]
    bands = np.linspace(1e-4, HY_BANDS - 1, HY_BANDS, dtype=np.float32)[None, :]
    feats = np.concatenate([t, np.cos(bands * w), -np.sin(bands * w)], axis=-1).astype(np.float32)
    feats_p = np.zeros((seq_len, 128), np.float32)
    feats_p[:, :HY_EMB] = feats
    max_decay = math.log(1e-2) / 0.3
    min_decay = math.log(1e-2) / 1.5
    deltas = np.abs(np.linspace(min_decay, max_decay, GW, dtype=np.float32))
    decay = np.exp(-t * deltas).astype(np.float32)
    return feats_p, np.concatenate([decay, decay], axis=1)


def _hyfilt_kernel(feat_ref, dec_ref, w1_ref, b1_ref, w2_ref, b2_ref, w3_ref, fr_ref,
                   ch_ref, cl_ref, sh_ref, sl_ref, hr_ref, hi_ref, *, seq_len):
    fr = fr_ref[...]
    h = jnp.sin(fr[0:1, :] * (_dot_hi(feat_ref[...], w1_ref[...]) + b1_ref[...]))
    h = jnp.sin(fr[1:2, :] * (_dot_hi(h, w2_ref[...]) + b2_ref[...]))
    h = _dot_hi(h, w3_ref[...]) * dec_ref[...]
    den = jnp.sum(jnp.abs(h[:, :GW]) + jnp.abs(h[:, GW:]), axis=0, keepdims=True)
    h = h / jnp.concatenate([den, den], axis=1)
    hc = _dot3_const(ch_ref[...], cl_ref[...], h)
    hs = _dot3_const(sh_ref[...], sl_ref[...], h)
    n = 2 * seq_len - 1
    row = lax.broadcasted_iota(jnp.int32, (seq_len, GW), 0)
    wgt = jnp.where(row == 0, 1.0 / n, 2.0 / n)
    hr_ref[...] = (hc[:, :GW] + hc[:, GW:]) * wgt
    hi_ref[...] = (hs[:, GW:] - hs[:, :GW]) * wgt


def _hyfilt_call(seq_len, hw):
    feats, decay = _hyena_consts(seq_len)
    tabs = _dft_tables(seq_len)
    full = lambda shape: pl.BlockSpec(shape, lambda l: (0,) * len(shape))
    lay = lambda shape: pl.BlockSpec((None,) + shape, lambda l: (l,) + (0,) * len(shape))
    return pl.pallas_call(
        functools.partial(_hyfilt_kernel, seq_len=seq_len),
        grid=(DEPTH,),
        in_specs=[full((seq_len, 128)), full((seq_len, 2 * GW)),
                  lay((128, 128)), lay((1, 128)), lay((128, 128)), lay((1, 128)), lay((128, 2 * GW)),
                  lay((2, 128))] + [full((seq_len, seq_len))] * 4,
        out_specs=[lay((seq_len, GW)), lay((seq_len, GW))],
        out_shape=[jax.ShapeDtypeStruct((DEPTH, seq_len, GW), F32)] * 2,
        compiler_params=_cparams(1),
        name=f"hyena_filter_{seq_len}",
    )(feats, decay, hw["w1"], hw["b1"], hw["w2"], hw["b2"], hw["w3"], hw["freq"], *tabs)


def _hyena_seq(load_rows, store_rows, cw_ref, bias_ref, hr_ref, hi_ref, tabs, pad_ref, x0_ref, z_ref, seq_len):
    ch_ref, cl_ref, sh_ref, sl_ref = tabs
    blk = ROW_BLOCK // 2

    def emit_conv(r0, uc):
        x0_ref[r0:r0 + blk, :] = uc[:, 0:GW]
        z_ref[r0:r0 + blk, :] = uc[:, 2 * GW:3 * GW] * uc[:, GW:2 * GW]

    _dwconv_tiles(load_rows, 3 * GW, cw_ref, pad_ref, seq_len, 3, blk, emit_conv)
    z = z_ref[...]
    za = _dot3_const(ch_ref[...], cl_ref[...], z)
    zb = _dot3_const(sh_ref[...], sl_ref[...], z)
    hr = hr_ref[...]
    hi = hi_ref[...]
    yr = za * hr + zb * hi
    yi = za * hi - zb * hr
    y = _dot3_const(ch_ref[...], cl_ref[...], yr) - _dot3_const(sh_ref[...], sl_ref[...], yi)
    bias = bias_ref[...]
    for r0 in range(0, seq_len, ROW_BLOCK):
        rows = slice(r0, r0 + ROW_BLOCK)
        store_rows(r0, x0_ref[rows, :] * (y[rows, :] + z[rows, :] * bias))


def _tri_masks(q):
    row = lax.broadcasted_iota(jnp.int32, (q, q), 0)
    col = lax.broadcasted_iota(jnp.int32, (q, q), 1)
    return col <= row, col >= row


def _ssd_kernel(*refs, seq_len, has_s0, emit_state, layer, has_prev):
    it = iter(refs)
    zx_ref, sm_ref, conv_ref, dtb_r_ref, dtb_c_ref, an_r_ref, an_c_ref, dvec_ref, gn_ref = (
        next(it) for _ in range(9))
    s0_ref = next(it) if has_s0 else None
    prev_ref = next(it) if has_prev else None
    o_ref = next(it)
    so_ref = next(it) if emit_state else None
    pad_ref, xs_ref, bm_ref, cm_ref, ac_ref, bt_ref, ar_ref, dtr_ref, y_ref, st_ref, upd_ref, ecs_ref = it
    q = SSD_Q
    nc = seq_len // q

    dvec = dvec_ref[...]

    def emit_conv(r0, acc):
        t = _silu(acc)
        rows = slice(r0, r0 + ROW_BLOCK)
        xs_ref[rows, :] = t[:, 0:GW]
        y_ref[rows, :] = t[:, 0:GW] * dvec
        bm_ref[rows, :] = t[:, GW:GW + 128]
        cm_ref[rows, :] = t[:, GW + 128:GW + 256].astype(BF16)

    _dwconv_tiles(lambda r0, nr: zx_ref[r0:r0 + nr, GW:3 * GW], 2 * GW, conv_ref, pad_ref, seq_len, 4,
                  ROW_BLOCK, emit_conv)
    sm = sm_ref[...]
    ac_ref[...] = _softplus(sm + dtb_r_ref[...]) * an_r_ref[...]
    sm_t = sm.T
    dtr = _softplus(sm_t[SM_DT:SM_DT + 8, :] + dtb_c_ref[...])
    ar = dtr * an_c_ref[...]
    for c in range(nc):
        bt_ref[c] = bm_ref[c * q:(c + 1) * q, :].T.astype(BF16)
        ar_ref[c] = ar[:, c * q:(c + 1) * q]
        dtr_ref[c] = dtr[:, c * q:(c + 1) * q]
    if has_s0:
        st_ref[...] = s0_ref[...]
    else:
        st_ref[...] = jnp.zeros(st_ref.shape, F32)

    lower, upper = _tri_masks(q)
    lo_f = lower.astype(F32).astype(BF16)
    up_f = upper.astype(F32).astype(BF16)
    lane128 = _lane_iota((q, 128))
    lane256 = _lane_iota((q, GW))
    st_row = lax.broadcasted_iota(jnp.int32, (128, GW), 0)
    st_col = lax.broadcasted_iota(jnp.int32, (128, GW), 1)
    st_mask = (st_row // HD) == (st_col // 128)
    dirs = (0, 1)
    tri_c = (lo_f, up_f)
    tri_r = (up_f, lo_f)
    masks = (lower, upper)
    last = (q - 1, 0)
    spread = [((st_row == d * NH + st_col // HD)).astype(F32).astype(BF16) for d in dirs]

    chunks_per_step = 2

    def chunk_body(step, carry):
        cks = [step * chunks_per_step + n for n in range(chunks_per_step)]
        units = [(n, d) for n in range(chunks_per_step) for d in dirs]
        rows = [pl.ds(pl.multiple_of(ci * q, q), q) for ci in cks]
        xs_b = [xs_ref[r, :].astype(BF16) for r in rows]
        c_c = [cm_ref[r, :] for r in rows]
        b_t = [bt_ref[ci] for ci in cks]
        a_c = [ac_ref[r, :] for r in rows]
        a_r = [ar_ref[ci] for ci in cks]
        dt_r = [dtr_ref[ci] for ci in cks]
        stack = [jnp.concatenate(
            [jnp.where((lane256 // HD) == h, x, jnp.zeros_like(x)) for h in range(NH)], axis=0) for x in xs_b]
        c_g = [jnp.concatenate([jnp.where(lane128 < HD, c, jnp.zeros_like(c)),
                                jnp.where(lane128 >= HD, c, jnp.zeros_like(c))], axis=0) for c in c_c]
        cum_c = {(n, d): _cumsum_rows(tri_c[d], a_c[n]) for n, d in units}
        cum_r = {(n, d): _cumsum_cols(a_r[n], tri_r[d]) for n, d in units}
        g_mat = [_dot(c_g[n], b_t[n]) for n in range(chunks_per_step)]
        w_cat, bw_cat = {}, {}
        for n, d in units:
            w_parts, bw_parts = [], []
            for h in range(NH):
                j = d * NH + h
                g = h // 2
                diff = cum_c[n, d][:, j:j + 1] - cum_r[n, d][j:j + 1, :]
                m = jnp.where(masks[d], jnp.exp(jnp.minimum(diff, 0.0)), 0.0)
                w_parts.append((g_mat[n][g * q:(g + 1) * q, :] * m * dt_r[n][j:j + 1, :]).astype(BF16))
                to_end = jnp.exp(cum_r[n, d][j:j + 1, last[d]:last[d] + 1] - cum_r[n, d][j:j + 1, :])
                bw_parts.append((b_t[n].astype(F32) * (dt_r[n][j:j + 1, :] * to_end)).astype(BF16))
            w_cat[n, d] = jnp.concatenate(w_parts, axis=1)
            bw_cat[n, d] = jnp.concatenate(bw_parts, axis=1)
        y_diag = {(n, d): _dot(w_cat[n, d], stack[n]) for n, d in units}
        upd = {(n, d): _dot(bw_cat[n, d], stack[n]) for n, d in units}
        ecs = {}
        for n, d in units:
            e_hi, e_lo = _split_bf16(jnp.exp(cum_c[n, d]))
            ecs[n, d] = _dot(e_hi, spread[d]) + _dot(e_lo, spread[d])
        for n in range(chunks_per_step):
            y_ref[rows[n], :] += y_diag[n, 0] + y_diag[n, 1]
        for n, d in units:
            upd_ref[d, cks[n]] = jnp.where(st_mask, upd[n, d], 0.0)
            ecs_ref[d, cks[n]] = ecs[n, d]
        return carry

    lax.fori_loop(0, nc // chunks_per_step, chunk_body, 0)

    def state_body(i, carry):
        cis = (i, nc - 1 - i)
        rows = [pl.ds(pl.multiple_of(ci * q, q), q) for ci in cis]
        ss = [st_ref[d] for d in dirs]
        y_off = [_dot(cm_ref[rows[d], :], ss[d].astype(BF16)) for d in dirs]
        for d in dirs:
            ecs = ecs_ref[d, cis[d]]
            y_ref[rows[d], :] += y_off[d] * ecs
            st_ref[d] = ss[d] * ecs[last[d]:last[d] + 1, :] + upd_ref[d, cis[d]]
        return carry

    lax.fori_loop(0, nc, state_body, 0)

    gn = gn_ref[...]
    for r0 in range(0, seq_len, ROW_BLOCK):
        rows = slice(r0, r0 + ROW_BLOCK)
        y = y_ref[rows, :] * _silu(zx_ref[rows, 0:GW])
        o_ref[rows, :] = y * lax.rsqrt(jnp.mean(y * y, axis=-1, keepdims=True) + EPS) * gn
    if emit_state:
        for d in range(2):
            st_t = st_ref[d].T
            for h in range(NH):
                blk = st_t[h * HD:(h + 1) * HD, :]
                if h // 2 == 1:
                    blk = pltpu.roll(blk, HD, axis=1)
                so_ref[layer, d, h] = blk[:, 0:HD]
        _pass_other_layers(so_ref, prev_ref, layer)


def _ssd_call(proj3, p, s0, seq_len, emit_state, layer=0, prev=None):
    bsz = proj3.shape[0]
    nc = seq_len // SSD_Q
    full = lambda shape: pl.BlockSpec(shape, lambda b: (0,) * len(shape))
    in_specs = [pl.BlockSpec((None, seq_len, 3 * GW), lambda b: (b, 0, COL_SSD // (3 * GW))),
                pl.BlockSpec((None, seq_len, 128), lambda b: (b, 0, COL_SMALL // 128)),
                full((4, 2 * GW)), full((1, 128)), full((8, 1)), full((1, 128)), full((8, 1)),
                full((1, GW)), full((1, GW))]
    args = [proj3, proj3, p["conv"], p["dtb_row"], p["dtb_col"], p["an_row"], p["an_col"], p["dvec"], p["gn"]]
    if s0 is not None:
        in_specs.append(pl.BlockSpec((None, 2, 128, GW), lambda b: (b, 0, 0, 0)))
        args.append(s0)
    out_specs = [pl.BlockSpec((None, seq_len, GW), lambda b: (b, 0, 0))]
    out_shape = [jax.ShapeDtypeStruct((bsz, seq_len, GW), F32)]
    state_spec = pl.BlockSpec((None, DEPTH, 2, NH, HD, HD), lambda b: (b, 0, 0, 0, 0, 0))
    if prev is not None:
        in_specs.append(state_spec)
        args.append(prev)
    if emit_state:
        out_specs.append(state_spec)
        out_shape.append(jax.ShapeDtypeStruct((bsz, DEPTH, 2, NH, HD, HD), F32))
    scratch = [pltpu.VMEM((seq_len + 16, 2 * GW), F32),
               pltpu.VMEM((seq_len, GW), F32),
               pltpu.VMEM((seq_len, 128), F32),
               pltpu.VMEM((seq_len, 128), BF16),
               pltpu.VMEM((seq_len, 128), F32),
               pltpu.VMEM((nc, 128, SSD_Q), BF16),
               pltpu.VMEM((nc, 8, SSD_Q), F32),
               pltpu.VMEM((nc, 8, SSD_Q), F32),
               pltpu.VMEM((seq_len, GW), F32),
               pltpu.VMEM((2, 128, GW), F32),
               pltpu.VMEM((2, nc, 128, GW), F32),
               pltpu.VMEM((2, nc, SSD_Q, GW), F32)]
    return pl.pallas_call(
        functools.partial(_ssd_kernel, seq_len=seq_len, has_s0=s0 is not None, emit_state=emit_state,
                          layer=layer, has_prev=prev is not None),
        grid=(bsz,),
        in_specs=in_specs,
        out_specs=out_specs,
        out_shape=out_shape,
        scratch_shapes=scratch,
        compiler_params=_cparams(1),
        name=f"ssd_{seq_len}",
    )(*args)


def _lru_gates(load_rows, conv_ref, w_ref, bias_ref, sp_ref, pad_ref, xc_ref, scan_refs, n):
    blk = 2 * ROW_BLOCK

    def emit_conv(r0, xc):
        xc_ref[r0:r0 + blk, :] = xc

    _dwconv_tiles(load_rows, GW, conv_ref, pad_ref, n, 4, blk, emit_conv)
    for r0 in range(0, n, blk):
        xc = xc_ref[r0:r0 + blk, :]
        xc_b = xc.astype(BF16)
        for d in range(2):
            a_ref, b_ref = scan_refs[2 * d], scan_refs[2 * d + 1]
            rc = slice(d * GW, (d + 1) * GW)
            ic = slice((2 + d) * GW, (3 + d) * GW)
            r = jax.nn.sigmoid(_dot(xc_b, w_ref[:, rc]) + bias_ref[:, rc])
            i = jax.nn.sigmoid(_dot(xc_b, w_ref[:, ic]) + bias_ref[:, ic])
            a = jnp.exp(-LRU_C * r * sp_ref[:, rc])
            a_ref[8 + r0:8 + r0 + blk, :] = a
            b_ref[8 + r0:8 + r0 + blk, :] = jnp.sqrt(jnp.maximum(1.0 - a * a, 0.0)) * i * xc


def _lru_scan(s0, scan_refs, n):
    af_ref, bf_ref, ab_ref, bb_ref = scan_refs
    ones8 = jnp.ones((8, GW), F32)
    zeros8 = jnp.zeros((8, GW), F32)
    for a_ref_, b_ref_ in ((af_ref, bf_ref), (ab_ref, bb_ref)):
        a_ref_[0:8, :] = ones8
        a_ref_[8 + n:16 + n, :] = ones8
        b_ref_[0:8, :] = zeros8
        b_ref_[8 + n:16 + n, :] = zeros8
    if s0 is not None:
        bf_ref[8:9, :] = bf_ref[8:9, :] + af_ref[8:9, :] * s0[0:1, :]
        bb_ref[7 + n:8 + n, :] = bb_ref[7 + n:8 + n, :] + ab_ref[7 + n:8 + n, :] * s0[1:2, :]

    d = 1
    while d < n:
        if d < 8:
            a_cur = af_ref[8:8 + n, :]
            af_ref[8:8 + n, :] = a_cur * af_ref[8 - d:8 - d + n, :]
            bf_ref[8:8 + n, :] = a_cur * bf_ref[8 - d:8 - d + n, :] + bf_ref[8:8 + n, :]
            a_cur = ab_ref[8:8 + n, :]
            ab_ref[8:8 + n, :] = a_cur * ab_ref[8 + d:8 + d + n, :]
            bb_ref[8:8 + n, :] = a_cur * bb_ref[8 + d:8 + d + n, :] + bb_ref[8:8 + n, :]
        else:
            m = n - d
            a_cur = af_ref[8 + d:8 + n, :]
            a_new = a_cur * af_ref[8:8 + m, :]
            b_new = a_cur * bf_ref[8:8 + m, :] + bf_ref[8 + d:8 + n, :]
            af_ref[8 + d:8 + n, :] = a_new
            bf_ref[8 + d:8 + n, :] = b_new
            a_cur = ab_ref[8:8 + m, :]
            a_new = a_cur * ab_ref[8 + d:8 + n, :]
            b_new = a_cur * bb_ref[8 + d:8 + n, :] + bb_ref[8:8 + m, :]
            ab_ref[8:8 + m, :] = a_new
            bb_ref[8:8 + m, :] = b_new
        d *= 2


def _hy_lru_kernel(*refs, seq_len, n_seq, has_s0, emit_state, layer, has_prev):
    it = iter(refs)
    (uh_ref, ul_ref, hcw_ref, hbias_ref, hr_ref, hi_ref, ch_ref, cl_ref, sh_ref, sl_ref,
     lcw_ref, lw_ref, lbias_ref, lsp_ref) = (next(it) for _ in range(14))
    s0_ref = next(it) if has_s0 else None
    prev_ref = next(it) if has_prev else None
    ohy_ref = next(it)
    olru_ref = next(it)
    so_ref = next(it) if emit_state else None
    hpad_ref, x0_ref, z_ref, lpad_ref, xc_ref, af_ref, bf_ref, ab_ref, bb_ref = it
    n = seq_len
    tabs = (ch_ref, cl_ref, sh_ref, sl_ref)
    scan_refs = (af_ref, bf_ref, ab_ref, bb_ref)
    for b in range(n_seq):
        def store_hy(r0, rows_out, b=b):
            ohy_ref[b, r0:r0 + ROW_BLOCK, :] = rows_out

        _hyena_seq(lambda r0, nr, b=b: uh_ref[b, r0:r0 + nr, :], store_hy, hcw_ref, hbias_ref, hr_ref, hi_ref,
                   tabs, hpad_ref, x0_ref, z_ref, n)
        _lru_gates(lambda r0, nr, b=b: ul_ref[b, r0:r0 + nr, 0:GW], lcw_ref, lw_ref, lbias_ref, lsp_ref,
                   lpad_ref, xc_ref, scan_refs, n)
        _lru_scan(s0_ref[b] if has_s0 else None, scan_refs, n)
        for r0 in range(0, n, ROW_BLOCK):
            rows = slice(r0, r0 + ROW_BLOCK)
            h = bf_ref[8 + r0:8 + r0 + ROW_BLOCK, :] + bb_ref[8 + r0:8 + r0 + ROW_BLOCK, :]
            olru_ref[b, rows, :] = h * _gelu_tanh(ul_ref[b, rows, GW:2 * GW])
        if emit_state:
            so_ref[b, layer, 0:1, :] = bf_ref[7 + n:8 + n, :]
            so_ref[b, layer, 1:2, :] = bb_ref[8:9, :]
            _pass_other_layers(so_ref.at[b], prev_ref.at[b] if has_prev else None, layer)


def _hy_lru_call(proj3, hy, lru, s0, seq_len, n_seq, emit_state, layer=0, prev=None):
    bsz = proj3.shape[0]
    tabs = _dft_tables(seq_len)
    full = lambda shape: pl.BlockSpec(shape, lambda b: (0,) * len(shape))
    const = lambda shape: pl.BlockSpec(shape, lambda b: (0,) * len(shape), pipeline_mode=pl.Buffered(1))
    in_specs = [pl.BlockSpec((n_seq, seq_len, 3 * GW), lambda b: (b, 0, COL_HY // (3 * GW))),
                pl.BlockSpec((n_seq, seq_len, 2 * GW), lambda b: (b, 0, COL_LRU // (2 * GW))),
                full((3, 3 * GW)), full((1, GW)), const((seq_len, GW)), const((seq_len, GW))]
    in_specs += [const((seq_len, seq_len))] * 4
    in_specs += [full((4, GW)), const((GW, 4 * GW)), full((1, 4 * GW)), full((1, 2 * GW))]
    args = [proj3, proj3, hy["conv"], hy["bias"], hy["hr"], hy["hi"], *tabs,
            lru["conv"], lru["w"], lru["bias"], lru["sp"]]
    if s0 is not None:
        in_specs.append(pl.BlockSpec((n_seq, 2, GW), lambda b: (b, 0, 0)))
        args.append(s0)
    out_specs = [pl.BlockSpec((n_seq, seq_len, GW), lambda b: (b, 0, 0))] * 2
    out_shape = [jax.ShapeDtypeStruct((bsz, seq_len, GW), F32)] * 2
    state_spec = pl.BlockSpec((n_seq, DEPTH, 2, GW), lambda b: (b, 0, 0, 0))
    if prev is not None:
        in_specs.append(state_spec)
        args.append(prev)
    if emit_state:
        out_specs.append(state_spec)
        out_shape.append(jax.ShapeDtypeStruct((bsz, DEPTH, 2, GW), F32))
    scratch = [pltpu.VMEM((seq_len + 16, 3 * GW), F32),
               pltpu.VMEM((seq_len, GW), F32),
               pltpu.VMEM((seq_len, GW), F32),
               pltpu.VMEM((seq_len + 16, GW), F32),
               pltpu.VMEM((seq_len, GW), F32)]
    scratch += [pltpu.VMEM((seq_len + 16, GW), F32)] * 4
    return pl.pallas_call(
        functools.partial(_hy_lru_kernel, seq_len=seq_len, n_seq=n_seq, has_s0=s0 is not None,
                          emit_state=emit_state, layer=layer, has_prev=prev is not None),
        grid=(bsz // n_seq,),
        in_specs=in_specs,
        out_specs=out_specs,
        out_shape=out_shape,
        scratch_shapes=scratch,
        compiler_params=_cparams(1),
        name=f"hyena_rglru_{seq_len}",
    )(*args)


TRI_BASE = 8


def _unit_tri_inverse(a_mats, lower_flags, eye, same_blk):
    c = a_mats[0].shape[0]
    ds = [jnp.where(same_blk[TRI_BASE], a, 0.0) for a in a_mats]
    d_bs = [d.astype(BF16) for d in ds]
    ps = [eye - d for d in ds]
    x_bs = [_dot(d_b, d_b).astype(BF16) for d_b in d_bs]
    rs = [_dot(x_b, jnp.concatenate([p.astype(BF16), x_b], axis=1)) for x_b, p in zip(x_bs, ps)]
    ps = [p + r[:, 0:c] for p, r in zip(ps, rs)]
    ts = [p + _dot(r[:, c:2 * c].astype(BF16), p.astype(BF16)) for p, r in zip(ps, rs)]
    size = 2 * TRI_BASE
    while size <= c:
        half = size // 2
        off = same_blk[size] & jnp.logical_not(same_blk[half])
        e_bs = [jnp.where(off, a, 0.0).astype(BF16) for a in a_mats]
        t_bs = [t.astype(BF16) for t in ts]
        starts = [[blk * size + (half if low else 0) for blk in range(c // size)] for low in lower_flags]
        rows = [jnp.concatenate([t[r0:r0 + half, :] for r0 in st], axis=0).astype(BF16)
                for t, st in zip(ts, starts)]
        us = [_dot(r, e_b).astype(BF16) for r, e_b in zip(rows, e_bs)]
        upds = [_dot(u, t_b) for u, t_b in zip(us, t_bs)]
        new_ts = []
        for t, upd, st, low in zip(ts, upds, starts, lower_flags):
            pieces = []
            for n, r0 in enumerate(st):
                changed = t[r0:r0 + half, :] - upd[n * half:(n + 1) * half, :]
                kept = t[r0 - half:r0, :] if low else t[r0 + half:r0 + size, :]
                pieces += [kept, changed] if low else [changed, kept]
            new_ts.append(jnp.concatenate(pieces, axis=0))
        ts = new_ts
        size *= 2
    return ts


def _gdn_kernel(*refs, seq_len, has_s0, emit_state, layer, has_prev):
    it = iter(refs)
    u_ref, sm_ref, conv_ref, dtb_r_ref, dtb_c_ref, an_r_ref, an_c_ref, gn_ref, ones_ref = (
        next(it) for _ in range(9))
    s0_ref = next(it) if has_s0 else None
    prev_ref = next(it) if has_prev else None
    o_ref = next(it)
    so_ref = next(it) if emit_state else None
    (pad_ref, q_ref, k_ref, v_ref, be_ref, gc_ref, gr_ref, oa_ref, st_ref,
     us_ref, wq_ref, at_ref, kd_ref, dr_ref) = it
    c = GDN_C
    nc = seq_len // c
    ones_bd = ones_ref[...]

    def emit_conv(r0, acc):
        t = _silu(acc)
        rows = slice(r0, r0 + ROW_BLOCK // 2)
        q_ref[rows, :] = t[:, 0:GW]
        k_ref[rows, :] = t[:, GW:2 * GW]
        v_ref[rows, :] = t[:, 2 * GW:3 * GW]

    _dwconv_tiles(lambda r0, nr: u_ref[r0:r0 + nr, 0:3 * GW], 3 * GW, conv_ref, pad_ref, seq_len, 4,
                  ROW_BLOCK // 2, emit_conv)
    for r0 in range(0, seq_len, NORM_BLOCK):
        rows = slice(r0, r0 + NORM_BLOCK)
        qq = q_ref[rows, :]
        kk = k_ref[rows, :]
        q_ref[rows, :] = qq * lax.rsqrt(_seg_sum(qq * qq, ones_bd) + EPS) * (HD ** -0.5)
        k_ref[rows, :] = kk * lax.rsqrt(_seg_sum(kk * kk, ones_bd) + EPS)
    sm = sm_ref[...]
    be_ref[...] = jax.nn.sigmoid(sm)
    gc_ref[...] = an_r_ref[...] * _softplus(sm + dtb_r_ref[...])
    sm_t = sm.T
    g_r = an_c_ref[...] * _softplus(sm_t[SM_A:SM_A + 8, :] + dtb_c_ref[...])
    for i in range(nc):
        gr_ref[i] = g_r[:, i * c:(i + 1) * c]
    oa_ref[...] = jnp.zeros(oa_ref.shape, F32)
    if has_s0:
        st_ref[...] = s0_ref[...]
    else:
        st_ref[...] = jnp.zeros(st_ref.shape, F32)

    lower, upper = _tri_masks(c)
    lo_f = lower.astype(F32).astype(BF16)
    up_f = upper.astype(F32).astype(BF16)
    row_i =lax.broadcasted_iota(jnp.int32, (c, c), 0)
    col_i = lax.broadcasted_iota(jnp.int32, (c, c), 1)
    eye = (row_i == col_i).astype(F32)
    same_blk = {}
    size = TRI_BASE
    while size <= c:
        same_blk[size] = (row_i // size) == (col_i // size)
        size *= 2
    lane256 = _lane_iota((c, GW))
    lane512 = _lane_iota((c, 2 * GW))
    bd_row = lax.broadcasted_iota(jnp.int32, (GW, GW), 0)
    bd_col = lax.broadcasted_iota(jnp.int32, (GW, GW), 1)
    bd_mask = (bd_row // HD) == (bd_col // HD)
    kt_row = lax.broadcasted_iota(jnp.int32, (GW, c), 0)

    def stack_heads(x_b, lane):
        zero = jnp.zeros_like(x_b)
        return jnp.concatenate([jnp.where(((lane % GW) // HD) == h, x_b, zero) for h in range(NH)], axis=0)

    dirs = (0, 1)
    tri_c = (lo_f, up_f)
    tri_r = (up_f, lo_f)
    incl = (lower, upper)
    strict = (col_i < row_i, col_i > row_i)
    last = (c - 1, 0)

    chunks_per_step = 2

    def solve_body(step, carry):
        cks = [step * chunks_per_step + n for n in range(chunks_per_step)]
        units = [(n, d) for n in range(chunks_per_step) for d in dirs]
        rows = [pl.ds(pl.multiple_of(ci * c, c), c) for ci in cks]
        q_c = [q_ref[r, :] for r in rows]
        k_c = [k_ref[r, :] for r in rows]
        v_c = [v_ref[r, :] for r in rows]
        be = [be_ref[r, :] for r in rows]
        g_c = [gc_ref[r, :] for r in rows]
        g_r = [gr_ref[ci] for ci in cks]
        cum_c = {(n, d): _cumsum_rows(tri_c[d], g_c[n]) for n, d in units}
        cum_r = {(n, d): _cumsum_cols(g_r[n], tri_r[d]) for n, d in units}
        kq = []
        for n in range(chunks_per_step):
            k_t = k_c[n].T.astype(BF16)
            k_t_heads = jnp.concatenate(
                [jnp.where((kt_row // HD) == h, k_t, jnp.zeros_like(k_t)) for h in range(NH)], axis=1)
            kq.append(_dot(jnp.concatenate([k_c[n], q_c[n]], axis=0).astype(BF16), k_t_heads))
        a_mats, attn, flags = [], {}, []
        for n, d in units:
            for h in range(NH):
                j = d * NH + h
                diff = cum_c[n, d][:, SM_A + j:SM_A + j + 1] - cum_r[n, d][j:j + 1, :]
                dec = jnp.where(incl[d], jnp.exp(jnp.minimum(diff, 0.0)), 0.0)
                a_mats.append(jnp.where(
                    strict[d], kq[n][0:c, h * c:(h + 1) * c] * be[n][:, SM_B + j:SM_B + j + 1] * dec, 0.0))
                attn[n, d, h] = (kq[n][c:2 * c, h * c:(h + 1) * c] * dec).astype(BF16)
                flags.append(d == 0)
        t_mats = _unit_tri_inverse(a_mats, flags, eye, same_blk)
        t_cat = {u: jnp.concatenate([t.astype(BF16) for t in t_mats[idx * NH:(idx + 1) * NH]], axis=1)
                 for idx, u in enumerate(units)}
        cols = {(n, d): [cum_c[n, d][:, SM_A + d * NH + h:SM_A + d * NH + h + 1] for h in range(NH)]
                for n, d in units}
        lasts = {(n, d): [cum_c[n, d][last[d]:last[d] + 1, SM_A + d * NH + h:SM_A + d * NH + h + 1]
                          for h in range(NH)] for n, d in units}
        b_l = {(n, d): _head_lanes([be[n][:, SM_B + d * NH + h:SM_B + d * NH + h + 1] for h in range(NH)], c)
               for n, d in units}
        e_l = {(n, d): _head_lanes([jnp.exp(cc) for cc in cols[n, d]], c) for n, d in units}
        rhs = {(n, d): jnp.concatenate([v_c[n] * b_l[n, d], k_c[n] * b_l[n, d] * e_l[n, d]], axis=1)
               for n, d in units}
        uw = {u: _dot(t_cat[u], stack_heads(rhs[u].astype(BF16), lane512)) for u in units}
        for n, d in units:
            ci = cks[n]
            us_ref[d, ci] = uw[n, d][:, 0:GW]
            wq_ref[d, ci, 0:c, :] = uw[n, d][:, GW:2 * GW].astype(BF16)
            wq_ref[d, ci, c:2 * c, :] = (q_c[n] * e_l[n, d]).astype(BF16)
            at_ref[d, ci] = jnp.concatenate([attn[n, d, h] for h in range(NH)], axis=1)
            k_dec = k_c[n] * _head_lanes([jnp.exp(lasts[n, d][h] - cols[n, d][h]) for h in range(NH)], c)
            kd_ref[d, ci] = k_dec.T.astype(BF16)
            dr_ref[d, ci] = _head_lanes([jnp.exp(lv) for lv in lasts[n, d]], 1)
        return carry

    lax.fori_loop(0, nc // chunks_per_step, solve_body, 0)

    def state_body(i, carry):
        cis = (i, nc - 1 - i)
        ss = [st_ref[d] for d in dirs]
        r2 = [_dot(wq_ref[d, cis[d]], ss[d].astype(BF16)) for d in dirs]
        v_new_b = [(us_ref[d, cis[d]] - r2[d][0:c, :]).astype(BF16) for d in dirs]
        upd = [_dot(kd_ref[d, cis[d]], v_new_b[d]) for d in dirs]
        o_c = [r2[d][c:2 * c, :] + _dot(at_ref[d, cis[d]], stack_heads(v_new_b[d], lane256)) for d in dirs]
        for d in dirs:
            st_ref[d] = ss[d] * dr_ref[d, cis[d]] + jnp.where(bd_mask, upd[d], 0.0)
            rows = pl.ds(pl.multiple_of(cis[d] * c, c), c)
            oa_ref[rows, :] += o_c[d]
        return carry

    lax.fori_loop(0, nc, state_body, 0)

    gn = gn_ref[...]
    for r0 in range(0, seq_len, NORM_BLOCK):
        rows = slice(r0, r0 + NORM_BLOCK)
        o = oa_ref[rows, :]
        o = o * lax.rsqrt(_seg_sum(o * o, ones_bd) * (1.0 / HD) + EPS) * gn
        o_ref[rows, :] = o * _silu(u_ref[rows, 3 * GW:4 * GW])
    if emit_state:
        for d in range(2):
            for h in range(NH):
                blk = st_ref[d, h * HD:(h + 1) * HD, (h // 2) * 128:(h // 2 + 1) * 128]
                if h % 2 == 1:
                    blk = pltpu.roll(blk, HD, axis=1)
                so_ref[layer, d, h] = blk[:, 0:HD]
        _pass_other_layers(so_ref, prev_ref, layer)


def _gdn_call(proj3, p, s0, seq_len, emit_state, layer=0, prev=None):
    bsz = proj3.shape[0]
    nc = seq_len // GDN_C
    full = lambda shape: pl.BlockSpec(shape, lambda b: (0,) * len(shape))
    in_specs = [pl.BlockSpec((None, seq_len, 4 * GW), lambda b: (b, 0, COL_GDN // (4 * GW))),
                pl.BlockSpec((None, seq_len, 128), lambda b: (b, 0, COL_SMALL // 128)),
                full((4, 3 * GW)), full((1, 128)), full((8, 1)), full((1, 128)), full((8, 1)),
                full((1, GW)), full((GW, GW))]
    args = [proj3, proj3, p["conv"], p["dtb_row"], p["dtb_col"], p["an_row"], p["an_col"], p["gn"], p["ones"]]
    if s0 is not None:
        in_specs.append(pl.BlockSpec((None, 2, GW, GW), lambda b: (b, 0, 0, 0)))
        args.append(s0)
    out_specs = [pl.BlockSpec((None, seq_len, GW), lambda b: (b, 0, 0))]
    out_shape = [jax.ShapeDtypeStruct((bsz, seq_len, GW), F32)]
    state_spec = pl.BlockSpec((None, DEPTH, 2, NH, HD, HD), lambda b: (b, 0, 0, 0, 0, 0))
    if prev is not None:
        in_specs.append(state_spec)
        args.append(prev)
    if emit_state:
        out_specs.append(state_spec)
        out_shape.append(jax.ShapeDtypeStruct((bsz, DEPTH, 2, NH, HD, HD), F32))
    scratch = [pltpu.VMEM((seq_len + 16, 3 * GW), F32),
               pltpu.VMEM((seq_len, GW), F32),
               pltpu.VMEM((seq_len, GW), F32),
               pltpu.VMEM((seq_len, GW), F32),
               pltpu.VMEM((seq_len, 128), F32),
               pltpu.VMEM((seq_len, 128), F32),
               pltpu.VMEM((nc, 8, GDN_C), F32),
               pltpu.VMEM((seq_len, GW), F32),
               pltpu.VMEM((2, GW, GW), F32),
               pltpu.VMEM((2, nc, GDN_C, GW), F32),
               pltpu.VMEM((2, nc, 2 * GDN_C, GW), BF16),
               pltpu.VMEM((2, nc, GDN_C, NH * GDN_C), BF16),
               pltpu.VMEM((2, nc, GW, GDN_C), BF16),
               pltpu.VMEM((2, nc, 1, GW), F32)]
    return pl.pallas_call(
        functools.partial(_gdn_kernel, seq_len=seq_len, has_s0=s0 is not None, emit_state=emit_state,
                          layer=layer, has_prev=prev is not None),
        grid=(bsz,),
        in_specs=in_specs,
        out_specs=out_specs,
        out_shape=out_shape,
        scratch_shapes=scratch,
        compiler_params=_cparams(1),
        name=f"gdn_{seq_len}",
    )(*args)


def _ffn_kernel(x_ref, ohy_ref, ossd_ref, olru_ref, ogdn_ref, mod_ref, g_ref, wo_ref, wg_ref, wu_ref,
                wd_ref, gfin_ref, o_ref, *, final_norm):
    mod = mod_ref[...]
    ga_m = mod[:, 2 * D_MODEL:3 * D_MODEL]
    sh_f = mod[:, 3 * D_MODEL:4 * D_MODEL]
    sc_f = mod[:, 4 * D_MODEL:5 * D_MODEL]
    ga_f = mod[:, 5 * D_MODEL:6 * D_MODEL]
    mo = None
    for i, r in enumerate((ohy_ref, ossd_ref, olru_ref, ogdn_ref)):
        part = _dot(r[...].astype(BF16), wo_ref[i * GW:(i + 1) * GW, :])
        mo = part if mo is None else mo + part
    x = x_ref[...] + ga_m * mo
    h = x * lax.rsqrt(jnp.mean(x * x, axis=-1, keepdims=True) + EPS) * g_ref[...]
    h = (h * (1.0 + sc_f) + sh_f).astype(BF16)
    ff = None
    for lo, hi in FF_SPLITS:
        gate = _dot(h, wg_ref[:, lo:hi])
        up = _dot(h, wu_ref[:, lo:hi])
        part = _dot((_silu(gate) * up).astype(BF16), wd_ref[lo:hi, :])
        ff = part if ff is None else ff + part
    x = x + ga_f * ff
    if final_norm:
        x = x * lax.rsqrt(jnp.mean(x * x, axis=-1, keepdims=True) + EPS) * gfin_ref[...]
    o_ref[...] = x


def _ffn_call(x2d, outs, mods_l, g, weights, layer, g_final, seq_len, is_lat, final_norm):
    n = x2d.shape[0]
    tm = ROW_TILE
    per_seq = seq_len // tm
    mod_map = (lambda t: (1 + t // per_seq, 0, 0)) if is_lat else (lambda t: (0, 0, 0))
    tile = lambda w: pl.BlockSpec((tm, w), lambda t: (t, 0))
    res = lambda shape: pl.BlockSpec(shape, lambda t: (0, 0), pipeline_mode=pl.Buffered(1))
    wres = lambda shape: pl.BlockSpec((None,) + shape, lambda t: (layer, 0, 0), pipeline_mode=pl.Buffered(1))
    return pl.pallas_call(
        functools.partial(_ffn_kernel, final_norm=final_norm),
        grid=(n // tm,),
        in_specs=[tile(D_MODEL), tile(GW), tile(GW), tile(GW), tile(GW),
                  pl.BlockSpec((None, 1, 6 * D_MODEL), mod_map),
                  res((1, D_MODEL)), wres((D_MODEL, D_MODEL)), wres((D_MODEL, D_FF)), wres((D_MODEL, D_FF)),
                  wres((D_FF, D_MODEL)), res((1, D_MODEL))],
        out_specs=tile(D_MODEL),
        out_shape=jax.ShapeDtypeStruct((n, D_MODEL), F32),
        compiler_params=_cparams(1),
        name="outproj_ffn",
    )(x2d, *outs, mods_l, g.reshape(1, D_MODEL), *weights, g_final.reshape(1, D_MODEL))


def _grid_pos_embed(n_tokens):
    rows = n_tokens // GRID_W
    rr, cc = np.meshgrid(np.arange(rows, dtype=np.float32), np.arange(GRID_W, dtype=np.float32), indexing="ij")
    quarter = D_MODEL // 4
    omega = (1.0 / (np.float32(POS_BASE) ** (np.arange(quarter, dtype=np.float32) / quarter))).astype(np.float32)

    def enc(pos):
        ang = pos.reshape(-1)[:, None] * omega[None, :]
        return np.concatenate([np.sin(ang), np.cos(ang)], axis=-1)

    return jnp.asarray(np.concatenate([enc(rr), enc(cc)], axis=-1).astype(np.float32))


def _pad_small(vec8, offset):
    row = jnp.zeros((1, 128), F32).at[0, offset:offset + 8].set(vec8)
    return row, vec8.reshape(8, 1)


def _pack_w_in(w):
    hy_in, ssd_in, lru_in = 768, 776, 512
    o_ssd = hy_in
    o_lru = o_ssd + ssd_in
    o_gdn = o_lru + lru_in
    w = w.astype(BF16)
    small = jnp.concatenate([w[..., o_ssd + 768:o_ssd + 776],
                             w[..., o_gdn + 1024:o_gdn + 1040],
                             jnp.zeros(w.shape[:-1] + (IN_PACK - 3096,), BF16)], axis=-1)
    return (w[..., 0:o_ssd + 768],
            w[..., o_lru:o_gdn + 1024],
            small)


def _ssd_state_in(s):
    st = jnp.swapaxes(s, -1, -2)
    zero = jnp.zeros_like(st[:, :, 0])
    rows = [jnp.concatenate([st[:, :, h] if h // 2 == g else zero for h in range(NH)], axis=-1)
            for g in range(2)]
    return jnp.concatenate(rows, axis=-2)


def _gdn_state_in(s):
    zero = jnp.zeros_like(s[:, :, 0])
    rows = [jnp.concatenate([s[:, :, h] if h == g else zero for h in range(NH)], axis=-1) for g in range(NH)]
    return jnp.concatenate(rows, axis=-2)


def kernel(x_prompt, x_sample, state_lru, state_ssd, state_gdn, c, c_ctx, w_mod, b_mod, g_mix, g_ffn, g_final, w_in, w_out, hy_conv, hy_w1, hy_b1, hy_w2, hy_b2, hy_w3, hy_freq, hy_bias, ssd_conv, ssd_dt_bias, ssd_a_log, ssd_d, ssd_norm, lru_conv, lru_w_r, lru_b_r, lru_w_i, lru_b_i, lru_lambda, gdn_conv, gdn_dt_bias, gdn_a_log, gdn_norm, w_gate, w_up, w_down):
    b_ctx, l_ctx, _ = x_prompt.shape
    b_lat, l_lat, _ = x_sample.shape

    cvec = jnp.zeros((16, D_MODEL), F32).at[0].set(c_ctx).at[1:1 + b_lat].set(c)
    mods = _mod_call(cvec, w_mod, b_mod).reshape(DEPTH, 16, 1, 6 * D_MODEL)

    hw = {"w1": jnp.zeros((DEPTH, 128, 128), F32).at[:, :HY_EMB, :HY_HIDDEN].set(hy_w1),
          "b1": jnp.zeros((DEPTH, 1, 128), F32).at[:, 0, :HY_HIDDEN].set(hy_b1),
          "w2": jnp.zeros((DEPTH, 128, 128), F32).at[:, :HY_HIDDEN, :HY_HIDDEN].set(hy_w2),
          "b2": jnp.zeros((DEPTH, 1, 128), F32).at[:, 0, :HY_HIDDEN].set(hy_b2),
          "w3": jnp.zeros((DEPTH, 128, 2 * GW), F32).at[:, :HY_HIDDEN, :].set(hy_w3),
          "freq": jnp.zeros((DEPTH, 2, 128), F32).at[:, :, :HY_HIDDEN].set(hy_freq)}
    filt = {l_ctx: _hyfilt_call(l_ctx, hw), l_lat: _hyfilt_call(l_lat, hw)}

    ones_bd = jnp.asarray(np.kron(np.eye(NH, dtype=np.float32), np.ones((HD, HD), np.float32))).astype(BF16)
    pos = _grid_pos_embed(l_lat)

    x_ctx = x_prompt.reshape(b_ctx * l_ctx, D_MODEL)
    x_lat = x_sample.reshape(b_lat * l_lat, D_MODEL)
    w_pack = _pack_w_in(w_in)
    ffn_w = tuple(w.astype(BF16) for w in (w_out, w_gate, w_up, w_down))
    s_lru = s_ssd = s_gdn = None
    for l in range(DEPTH):
        ssd_dtb_r, ssd_dtb_c = _pad_small(ssd_dt_bias[l].reshape(8), SM_DT)
        ssd_an_r, ssd_an_c = _pad_small(-jnp.exp(ssd_a_log[l].reshape(8)), SM_DT)
        ssd_p = {"conv": ssd_conv[l], "dtb_row": ssd_dtb_r, "dtb_col": ssd_dtb_c, "an_row": ssd_an_r,
                 "an_col": ssd_an_c, "dvec": jnp.repeat(ssd_d[l], HD).reshape(1, GW),
                 "gn": ssd_norm[l].reshape(1, GW)}
        eye_h = jnp.eye(NH, dtype=F32)
        blockdiag = lambda w: (w[:, :, :, None, :] * eye_h[None, :, None, :, None]).reshape(2, GW, GW)
        w_r, w_i = blockdiag(lru_w_r[l]), blockdiag(lru_w_i[l])
        lru_p = {"conv": lru_conv[l],
                 "w": jnp.concatenate([w_r[0], w_r[1], w_i[0], w_i[1]], axis=1).astype(BF16),
                 "bias": jnp.concatenate([lru_b_r[l].reshape(1, 2 * GW), lru_b_i[l].reshape(1, 2 * GW)], axis=1),
                 "sp": jax.nn.softplus(-lru_lambda[l]).reshape(1, 2 * GW)}
        gdn_dtb_r, gdn_dtb_c = _pad_small(gdn_dt_bias[l].reshape(8), SM_A)
        gdn_an_r, gdn_an_c = _pad_small(-jnp.exp(gdn_a_log[l].reshape(8)), SM_A)
        gdn_p = {"conv": gdn_conv[l], "dtb_row": gdn_dtb_r, "dtb_col": gdn_dtb_c, "an_row": gdn_an_r,
                 "an_col": gdn_an_c, "gn": jnp.tile(gdn_norm[l], NH).reshape(1, GW), "ones": ones_bd}
        last = l == DEPTH - 1

        proj = _inproj_call(x_ctx, None, mods[l], g_mix[l], w_pack, l, l_ctx, False).reshape(b_ctx, l_ctx, IN_PACK)
        hy_p = {"conv": hy_conv[l], "bias": hy_bias[l].reshape(1, GW)}
        o_hy, o_lru, s_lru = _hy_lru_call(
            proj, dict(hy_p, hr=filt[l_ctx][0][l], hi=filt[l_ctx][1][l]), lru_p, None, l_ctx, 4, True, l, s_lru)
        o_ssd, s_ssd = _ssd_call(proj, ssd_p, None, l_ctx, True, l, s_ssd)
        o_gdn, s_gdn = _gdn_call(proj, gdn_p, None, l_ctx, True, l, s_gdn)
        outs = [o.reshape(b_ctx * l_ctx, GW) for o in (o_hy, o_ssd, o_lru, o_gdn)]
        x_ctx = _ffn_call(x_ctx, outs, mods[l], g_ffn[l], ffn_w, l, g_final, l_ctx, False, last)

        if l == 0:
            proj, x_lat = _inproj_call(x_lat, pos, mods[l], g_mix[l], w_pack, l, l_lat, True)
        else:
            proj = _inproj_call(x_lat, None, mods[l], g_mix[l], w_pack, l, l_lat, True)
        proj = proj.reshape(b_lat, l_lat, IN_PACK)
        o_hy, o_lru = _hy_lru_call(
            proj, dict(hy_p, hr=filt[l_lat][0][l], hi=filt[l_lat][1][l]), lru_p, state_lru[:, l], l_lat, 1, False)
        o_ssd, = _ssd_call(proj, ssd_p, _ssd_state_in(state_ssd[:, l]), l_lat, False)
        o_gdn, = _gdn_call(proj, gdn_p, _gdn_state_in(state_gdn[:, l]), l_lat, False)
        outs = [o.reshape(b_lat * l_lat, GW) for o in (o_hy, o_ssd, o_lru, o_gdn)]
        x_lat = _ffn_call(x_lat, outs, mods[l], g_ffn[l], ffn_w, l, g_final, l_lat, True, last)

    return (x_ctx.reshape(b_ctx, l_ctx, D_MODEL), x_lat.reshape(b_lat, l_lat, D_MODEL), s_lru, s_ssd, s_gdn)
```
